```python
import jax
import jax.numpy as jnp
from jax import lax

D_MODEL = 2048
BATCH = 4
SEQ = 4096
DEPTH = 2

GRID_W = 64
CTX_LEN = 256
N_EVEN = (DEPTH + 1) // 2
N_ODD = DEPTH // 2
MIX_HALF = D_MODEL // 2
CHUNK = 128
A_GROUPS = 4
A_GW = MIX_HALF // A_GROUPS
POOL_WINDOWS = (2, 4, 8, 16)
N_POOL = len(POOL_WINDOWS)
B_GW = MIX_HALF // N_POOL
EVEN_IN = 3 * MIX_HALF
CONV_W = 31
HEAD_DIM = 64
N_Q_HEADS = MIX_HALF // HEAD_DIM
N_KV_HEADS = 2
Q_PER_KV = N_Q_HEADS // N_KV_HEADS
KV_W = N_KV_HEADS * HEAD_DIM
ODD_IN = 3 * MIX_HALF + 2 * KV_W
ATT_BLK = 128
WINDOW = 128
AXIS_DIM = HEAD_DIM // 2
ROPE_BASE = 10000.0
N_EXPERTS = 64
N_EXPERT_GROUPS = 8
TOPK_GROUPS = 4
TOP_K = 8
EXPERT_FF = 512
ROUTED_SCALE = 2.5
MOE_BLK = 256
EPS = 1e-6

kernel_name = 'hybrid_flow_backbone_gmlp_pool_conformer_swa_moe'


def rms_norm(x, g):
    xf = x.astype(jnp.float32)
    y = xf * lax.rsqrt(jnp.mean(xf * xf, axis=-1, keepdims=True) + EPS)
    return y.astype(x.dtype) * g


def layer_norm(x, g, b):
    xf = x.astype(jnp.float32)
    mu = jnp.mean(xf, axis=-1, keepdims=True)
    var = jnp.mean(jnp.square(xf - mu), axis=-1, keepdims=True)
    return ((xf - mu) * lax.rsqrt(var + EPS)).astype(x.dtype) * g + b


def modulate(h, shift, scale):
    return h * (1 + scale) + shift


def chunk_gating(u, v, ln_g, ln_b, w_s, b_s):
    bsz, length, _ = v.shape
    vn = layer_norm(v, ln_g, ln_b).reshape(bsz, length // CHUNK, CHUNK, A_GROUPS, A_GW)
    mixed = jnp.einsum('gpq,bnqgc->bnpgc', w_s, vn) + b_s.T[None, None, :, :, None]
    return u * mixed.reshape(bsz, length, MIX_HALF)


def multiscale_pool(z, w_pool, scale):
    bsz, length, _ = z.shape
    zg = z.reshape(bsz, length, N_POOL, B_GW).astype(jnp.float32)
    cs = jnp.pad(jnp.cumsum(zg, axis=1), ((0, 0), (1, 0), (0, 0), (0, 0)))
    t = jnp.arange(length)
    outs = []
    for gi, w in enumerate(POOL_WINDOWS):
        lo = jnp.clip(t - w // 2, 0, length)
        hi = jnp.clip(t + w // 2, 0, length)
        mean = (cs[:, hi, gi] - cs[:, lo, gi]) / (hi - lo).astype(jnp.float32)[None, :, None]
        outs.append(mean - zg[:, :, gi])
    pooled = jnp.stack(outs, axis=2).astype(z.dtype)
    mixed = jnp.einsum('blgc,gcd->blgd', pooled, w_pool)
    return mixed.reshape(bsz, length, MIX_HALF) * scale


def even_mixer(h, w_in, ln_g, ln_b, w_s, b_s, w_pool, pool_scale, w_out):
    p = h @ w_in
    u = jax.nn.gelu(p[..., :MIX_HALF], approximate=False)
    v = jax.nn.gelu(p[..., MIX_HALF:2 * MIX_HALF], approximate=False)
    z = p[..., 2 * MIX_HALF:]
    y = jnp.concatenate([chunk_gating(u, v, ln_g, ln_b, w_s, b_s),
                         multiscale_pool(z, w_pool, pool_scale)], axis=-1)
    return y @ w_out


def conv_module(a, gate, conv_w, conv_b, ln_g, ln_b):
    z = a * jax.nn.sigmoid(gate)
    z = lax.conv_general_dilated(z, conv_w[:, None, :], window_strides=(1,),
                                 padding=[(CONV_W // 2, CONV_W // 2)],
                                 dimension_numbers=('NWC', 'WIO', 'NWC'),
                                 feature_group_count=MIX_HALF) + conv_b
    return jax.nn.silu(layer_norm(z, ln_g, ln_b))


def axial_rope(seq):
    rows = seq // GRID_W
    row = jnp.broadcast_to(jnp.arange(rows)[:, None], (rows, GRID_W)).reshape(-1).astype(jnp.float32)
    col = jnp.broadcast_to(jnp.arange(GRID_W)[None, :], (rows, GRID_W)).reshape(-1).astype(jnp.float32)
    inv = ROPE_BASE ** (-jnp.arange(0, AXIS_DIM, 2, dtype=jnp.float32) / AXIS_DIM)
    ang_r = row[:, None] * inv
    ang_c = col[:, None] * inv
    ang = jnp.concatenate([ang_r, ang_r, ang_c, ang_c], axis=-1)
    return jnp.cos(ang), jnp.sin(ang)


def apply_rope(x, cos, sin):
    xs = x.reshape(x.shape[:-1] + (2, 2, AXIS_DIM // 2))
    rot = jnp.stack([-xs[..., 1, :], xs[..., 0, :]], axis=-2).reshape(x.shape)
    return x * cos[:, None].astype(x.dtype) + rot * sin[:, None].astype(x.dtype)


def latent_attention(q, k, v, kc, vc, sink):
    bsz, seq = q.shape[:2]
    nb = seq // ATT_BLK
    scale = HEAD_DIM ** -0.5
    qb = q.reshape(bsz, nb, ATT_BLK, N_KV_HEADS, Q_PER_KV, HEAD_DIM)

    def band(t):
        tp = jnp.pad(t, ((0, 0), (ATT_BLK, ATT_BLK), (0, 0), (0, 0)))
        tp = tp.reshape(bsz, nb + 2, ATT_BLK, N_KV_HEADS, HEAD_DIM)
        return jnp.concatenate([tp[:, :-2], tp[:, 1:-1], tp[:, 2:]], axis=2)

    kb, vb = band(k), band(v)
    s_loc = jnp.einsum('bnqkgd,bnskd->bnkgqs', qb, kb).astype(jnp.float32) * scale
    blk = jnp.arange(nb)[:, None]
    qpos = blk * ATT_BLK + jnp.arange(ATT_BLK)
    kpos = (blk - 1) * ATT_BLK + jnp.arange(3 * ATT_BLK)
    valid = ((jnp.abs(qpos[:, :, None] - kpos[:, None, :]) <= WINDOW)
             & (kpos[:, None, :] >= 0) & (kpos[:, None, :] < seq))
    s_loc = jnp.where(valid[None, :, None, None], s_loc, -jnp.inf)
    s_ctx = jnp.einsum('bnqkgd,bckd->bnkgqc', qb, kc).astype(jnp.float32) * scale
    sk = sink.astype(jnp.float32).reshape(N_KV_HEADS, Q_PER_KV)[:, :, None, None]
    m = jnp.maximum(jnp.maximum(s_loc.max(-1, keepdims=True), s_ctx.max(-1, keepdims=True)), sk)
    e_loc = jnp.exp(s_loc - m)
    e_ctx = jnp.exp(s_ctx - m)
    den = e_loc.sum(-1, keepdims=True) + e_ctx.sum(-1, keepdims=True) + jnp.exp(sk - m)
    o = (jnp.einsum('bnkgqs,bnskd->bnqkgd', (e_loc / den).astype(v.dtype), vb)
         + jnp.einsum('bnkgqc,bckd->bnqkgd', (e_ctx / den).astype(v.dtype), vc))
    return o.reshape(bsz, seq, N_Q_HEADS * HEAD_DIM)


def context_attention(qc, kc, vc, sink):
    bsz, clen = qc.shape[:2]
    q = qc.reshape(bsz, clen, N_KV_HEADS, Q_PER_KV, HEAD_DIM)
    s = jnp.einsum('bqkgd,bskd->bkgqs', q, kc).astype(jnp.float32) * HEAD_DIM ** -0.5
    sk = sink.astype(jnp.float32).reshape(N_KV_HEADS, Q_PER_KV)[:, :, None, None]
    m = jnp.maximum(s.max(-1, keepdims=True), sk)
    e = jnp.exp(s - m)
    den = e.sum(-1, keepdims=True) + jnp.exp(sk - m)
    o = jnp.einsum('bkgqs,bskd->bqkgd', (e / den).astype(vc.dtype), vc)
    return o.reshape(bsz, clen, N_Q_HEADS * HEAD_DIM)


def context_kv(hc, w_in):
    bsz, clen, _ = hc.shape
    kv = hc @ w_in[:, 3 * MIX_HALF:]
    kc = kv[..., :KV_W].reshape(bsz, clen, N_KV_HEADS, HEAD_DIM)
    vc = kv[..., KV_W:].reshape(bsz, clen, N_KV_HEADS, HEAD_DIM)
    return kc, vc


def odd_mixer_latent(h, kc, vc, w_in, conv_w, conv_b, ln_g, ln_b, sink, w_out, cos, sin):
    bsz, seq, _ = h.shape
    p = h @ w_in
    a = p[..., :MIX_HALF]
    gate = p[..., MIX_HALF:2 * MIX_HALF]
    q = apply_rope(p[..., 2 * MIX_HALF:3 * MIX_HALF].reshape(bsz, seq, N_Q_HEADS, HEAD_DIM), cos, sin)
    k = apply_rope(p[..., 3 * MIX_HALF:3 * MIX_HALF + KV_W].reshape(bsz, seq, N_KV_HEADS, HEAD_DIM), cos, sin)
    v = p[..., 3 * MIX_HALF + KV_W:].reshape(bsz, seq, N_KV_HEADS, HEAD_DIM)
    y = jnp.concatenate([conv_module(a, gate, conv_w, conv_b, ln_g, ln_b),
                         latent_attention(q, k, v, kc, vc, sink)], axis=-1)
    return y @ w_out


def odd_mixer_context(hc, kc, vc, w_in, conv_w, conv_b, ln_g, ln_b, sink, w_out):
    bsz, clen, _ = hc.shape
    p = hc @ w_in[:, :3 * MIX_HALF]
    a = p[..., :MIX_HALF]
    gate = p[..., MIX_HALF:2 * MIX_HALF]
    q = p[..., 2 * MIX_HALF:].reshape(bsz, clen, N_Q_HEADS, HEAD_DIM)
    y = jnp.concatenate([conv_module(a, gate, conv_w, conv_b, ln_g, ln_b),
                         context_attention(q, kc, vc, sink)], axis=-1)
    return y @ w_out


def moe_ffn(h, router_w, router_bias, w_gate, w_up, w_down, ws_gate, ws_up, ws_down):
    n_tok, d = h.shape
    scores = jax.nn.sigmoid((h @ router_w).astype(jnp.float32))
    biased = scores + router_bias.astype(jnp.float32)
    grp = biased.reshape(n_tok, N_EXPERT_GROUPS, N_EXPERTS // N_EXPERT_GROUPS)
    grp_score = lax.top_k(grp, 2)[0].sum(-1)
    _, top_grp = lax.top_k(grp_score, TOPK_GROUPS)
    grp_mask = jax.nn.one_hot(top_grp, N_EXPERT_GROUPS, dtype=jnp.float32).sum(1) > 0
    expert_mask = jnp.repeat(grp_mask, N_EXPERTS // N_EXPERT_GROUPS, axis=1)
    _, idx = lax.top_k(jnp.where(expert_mask, biased, -jnp.inf), TOP_K)
    gates = jnp.take_along_axis(scores, idx, axis=1)
    gates = gates / gates.sum(-1, keepdims=True) * ROUTED_SCALE
    n_assign = n_tok * TOP_K
    flat_e = idx.reshape(-1)
    order = jnp.argsort(flat_e)
    sorted_e = flat_e[order]
    sizes = jnp.bincount(flat_e, length=N_EXPERTS)
    padded = (sizes + MOE_BLK - 1) // MOE_BLK * MOE_BLK
    start = jnp.cumsum(sizes) - sizes
    pad_end = jnp.cumsum(padded)
    pad_start = pad_end - padded
    slot = pad_start[sorted_e] + jnp.arange(n_assign) - start[sorted_e]
    n_blocks = -(-n_assign // MOE_BLK) + N_EXPERTS
    n_rows = n_blocks * MOE_BLK
    row_tok = jnp.full((n_rows,), n_tok, jnp.int32).at[slot].set((order // TOP_K).astype(jnp.int32))
    row_w = jnp.zeros((n_rows,), jnp.float32).at[slot].set(gates.reshape(-1)[order])
    blk_exp = jnp.minimum(jnp.searchsorted(pad_end, jnp.arange(n_blocks) * MOE_BLK, side='right'),
                          N_EXPERTS - 1)
    h_pad = jnp.concatenate([h, jnp.zeros((1, d), h.dtype)], axis=0)

    def step(acc, blk):
        rows, rw, e = blk
        xb = h_pad[rows]
        hid = jax.nn.silu(xb @ w_gate[e]) * (xb @ w_up[e])
        yb = (hid @ w_down[e]).astype(jnp.float32) * rw[:, None]
        return acc.at[rows].add(yb), None

    acc, _ = lax.scan(step, jnp.zeros((n_tok + 1, d), jnp.float32),
                      (row_tok.reshape(n_blocks, MOE_BLK), row_w.reshape(n_blocks, MOE_BLK), blk_exp))
    shared = (jax.nn.silu(h @ ws_gate) * (h @ ws_up)) @ ws_down
    return acc[:n_tok].astype(h.dtype) + shared


def setup_inputs(seed: int = 0) -> dict:
    key = jax.random.key(seed)
    keys = iter(jax.random.split(key, 40))

    def normal(shape, std):
        return jax.random.normal(next(keys), shape, jnp.float32) * std

    def near_one(shape):
        return 1.0 + normal(shape, 0.1)

    d, m, f, e = D_MODEL, MIX_HALF, EXPERT_FF, N_EXPERTS
    return {
        'x': normal((BATCH, SEQ, d), 1.0),
        'c': normal((BATCH, d), 1.0),
        'ctx': normal((BATCH, CTX_LEN, d), 1.0),
        'c_ctx': normal((d,), 1.0),
        'ada_w': normal((DEPTH, d, 6 * d), 0.5 * d ** -0.5),
        'ada_b': normal((DEPTH, 6 * d), 0.02),
        'norm1_g': near_one((DEPTH, d)),
        'norm2_g': near_one((DEPTH, d)),
        'even_w_in': normal((N_EVEN, d, EVEN_IN), d ** -0.5),
        'gmlp_ln_g': near_one((N_EVEN, m)),
        'gmlp_ln_b': normal((N_EVEN, m), 0.02),
        'gmlp_ws': normal((N_EVEN, A_GROUPS, CHUNK, CHUNK), CHUNK ** -0.5),
        'gmlp_bs': near_one((N_EVEN, A_GROUPS, CHUNK)),
        'pool_w': normal((N_EVEN, N_POOL, B_GW, B_GW), B_GW ** -0.5),
        'pool_scale': near_one((N_EVEN, m)),
        'even_w_out': normal((N_EVEN, 2 * m, d), (2 * m) ** -0.5),
        'odd_w_in': normal((N_ODD, d, ODD_IN), d ** -0.5),
        'conv_w': normal((N_ODD, CONV_W, m), CONV_W ** -0.5),
        'conv_b': normal((N_ODD, m), 0.02),
        'conv_ln_g': near_one((N_ODD, m)),
        'conv_ln_b': normal((N_ODD, m), 0.02),
        'attn_sink': normal((N_ODD, N_Q_HEADS), 0.5),
        'odd_w_out': normal((N_ODD, 2 * m, d), (2 * m) ** -0.5),
        'router_w': normal((DEPTH, d, e), d ** -0.5),
        'router_bias': normal((DEPTH, e), 0.01),
        'exp_w_gate': normal((DEPTH, e, d, f), d ** -0.5),
        'exp_w_up': normal((DEPTH, e, d, f), d ** -0.5),
        'exp_w_down': normal((DEPTH, e, f, d), f ** -0.5),
        'shared_w_gate': normal((DEPTH, d, f), d ** -0.5),
        'shared_w_up': normal((DEPTH, d, f), d ** -0.5),
        'shared_w_down': normal((DEPTH, f, d), f ** -0.5),
        'final_g': near_one((d,)),
    }


def reference(x, c, ctx, c_ctx, ada_w, ada_b, norm1_g, norm2_g, even_w_in, gmlp_ln_g, gmlp_ln_b,
              gmlp_ws, gmlp_bs, pool_w, pool_scale, even_w_out, odd_w_in, conv_w, conv_b, conv_ln_g,
              conv_ln_b, attn_sink, odd_w_out, router_w, router_bias, exp_w_gate, exp_w_up, exp_w_down,
              shared_w_gate, shared_w_up, shared_w_down, final_g):
    bsz, seq, d = x.shape
    cos, sin = axial_rope(seq)
    xc = ctx
    for i in range(DEPTH):
        j = i // 2
        even = i % 2 == 0
        last = i == DEPTH - 1
        need_ctx = not (last and even)
        mod = jax.nn.silu(c) @ ada_w[i] + ada_b[i]
        sh1, sc1, g1, sh2, sc2, g2 = jnp.split(mod[:, None, :], 6, axis=-1)
        h = modulate(rms_norm(x, norm1_g[i]), sh1, sc1)
        if need_ctx:
            mod_c = jax.nn.silu(c_ctx) @ ada_w[i] + ada_b[i]
            sh1c, sc1c, g1c, sh2c, sc2c, g2c = jnp.split(mod_c, 6)
            hc = modulate(rms_norm(xc, norm1_g[i]), sh1c, sc1c)
        if even:
            even_args = (even_w_in[j], gmlp_ln_g[j], gmlp_ln_b[j], gmlp_ws[j], gmlp_bs[j],
                         pool_w[j], pool_scale[j], even_w_out[j])
            y = even_mixer(h, *even_args)
            if not last:
                yc = even_mixer(hc, *even_args)
        else:
            conv_args = (conv_w[j], conv_b[j], conv_ln_g[j], conv_ln_b[j], attn_sink[j], odd_w_out[j])
            kc, vc = context_kv(hc, odd_w_in[j])
            y = odd_mixer_latent(h, kc, vc, odd_w_in[j], *conv_args, cos, sin)
            if not last:
                yc = odd_mixer_context(hc, kc, vc, odd_w_in[j], *conv_args)
        x = x + g1 * y
        h = modulate(rms_norm(x, norm2_g[i]), sh2, sc2).reshape(bsz * seq, d)
        moe_args = (router_w[i], router_bias[i], exp_w_gate[i], exp_w_up[i], exp_w_down[i],
                    shared_w_gate[i], shared_w_up[i], shared_w_down[i])
        if last:
            x = x + g2 * moe_ffn(h, *moe_args).reshape(bsz, seq, d)
        else:
            xc = xc + g1c * yc
            hc = modulate(rms_norm(xc, norm2_g[i]), sh2c, sc2c)
            n_ctx = hc.shape[0] * hc.shape[1]
            t = moe_ffn(jnp.concatenate([h, hc.reshape(n_ctx, d)], axis=0), *moe_args)
            x = x + g2 * t[:bsz * seq].reshape(bsz, seq, d)
            xc = xc + g2c * t[bsz * seq:].reshape(xc.shape)
    return rms_norm(x, final_g)
```

```python
import functools

import jax
import jax.numpy as jnp
from jax import lax
from jax.experimental import pallas as pl
from jax.experimental.pallas import tpu as pltpu

F32 = jnp.float32
BF16 = jnp.bfloat16

D = 2048
BATCH = 4
SEQ = 4096
DEPTH = 2
GRID_W = 64
CTX_LEN = 256
HALF = D // 2
CHUNK = 128
A_GROUPS = 4
A_GW = HALF // A_GROUPS
POOL_WINDOWS = (2, 4, 8, 16)
B_GW = HALF // len(POOL_WINDOWS)
CONV_W = 31
HEAD_DIM = 64
N_Q_HEADS = HALF // HEAD_DIM
N_KV_HEADS = 2
Q_PER_KV = N_Q_HEADS // N_KV_HEADS
KV_W = N_KV_HEADS * HEAD_DIM
ATT_BLK = 128
WINDOW = 128
AXIS_DIM = HEAD_DIM // 2
ROPE_BASE = 10000.0
N_EXPERTS = 64
N_EXPERT_GROUPS = 8
TOPK_GROUPS = 4
TOP_K = 8
EXPERT_FF = 512
ROUTED_SCALE = 2.5
EPS = 1e-6

N_LAT = BATCH * SEQ
N_CTX = BATCH * CTX_LEN
CTX_GROUP = BATCH
HALO = 128
CONV_HALO = 16
MOE_BM = 256

MIB = 1024 * 1024


def _cparams(sem, vmem_mib):
    return pltpu.CompilerParams(dimension_semantics=sem, vmem_limit_bytes=vmem_mib * MIB)


def _dot(a, b):
    return jnp.dot(a, b, preferred_element_type=F32)


def _rms_mod(x, g, sh, sc):
    y = x * lax.rsqrt(jnp.mean(x * x, axis=-1, keepdims=True) + EPS) * g
    return y * (1.0 + sc) + sh


def _layer_norm(x, g, b):
    mu = jnp.mean(x, axis=-1, keepdims=True)
    xc = x - mu
    var = jnp.mean(xc * xc, axis=-1, keepdims=True)
    return xc * lax.rsqrt(var + EPS) * g + b


def _gelu(x):
    return 0.5 * x * (1.0 + lax.erf(x * (2.0 ** -0.5)))


def _silu(x):
    return x * jax.nn.sigmoid(x)


def _mod_spec(chunk, group_fn):
    return pl.BlockSpec((None, None, 1, D), lambda i, *_: (group_fn(i), chunk, 0, 0))


def _lat_group(tm):
    return lambda i: i // (SEQ // tm)


def _ctx_group(_i):
    return CTX_GROUP


def _ada_kernel(c_ref, w_ref, b_ref, o_ref):
    s = _silu(c_ref[...]).astype(BF16)
    o_ref[...] = _dot(s, w_ref[...].astype(BF16)) + b_ref[...]


def _ada_mod(c, c_ctx, ada_w, ada_b):
    tn = 1024
    cv = jnp.zeros((8, D), F32).at[:BATCH].set(c).at[CTX_GROUP].set(c_ctx)
    out = pl.pallas_call(
        _ada_kernel,
        grid=(DEPTH, 6 * D // tn),
        in_specs=[pl.BlockSpec((8, D), lambda l, j: (0, 0)),
                  pl.BlockSpec((None, D, tn), lambda l, j: (l, 0, j)),
                  pl.BlockSpec((None, 1, tn), lambda l, j: (l, 0, j))],
        out_specs=pl.BlockSpec((None, 8, tn), lambda l, j: (l, 0, j)),
        out_shape=jax.ShapeDtypeStruct((DEPTH, 8, 6 * D), F32),
        compiler_params=_cparams(("arbitrary", "arbitrary"), 40),
        name="ada_mod",
    )(cv, ada_w, ada_b.reshape(DEPTH, 1, 6 * D))
    return out.reshape(DEPTH, 8, 6, 1, D)


def _even_in_kernel(x_ref, g_ref, sh_ref, sc_ref, w_ref, uv_ref, z_ref, h_ref):
    j = pl.program_id(1)

    @pl.when(j == 0)
    def _():
        h_ref[...] = _rms_mod(x_ref[...], g_ref[...], sh_ref[...], sc_ref[...]).astype(BF16)

    acc = _dot(h_ref[...], w_ref[...])

    @pl.when(j < 2)
    def _():
        uv_ref[...] = _gelu(acc).astype(BF16)

    @pl.when(j == 2)
    def _():
        z_ref[...] = acc


def _even_in(x, n_rows, mod_l, norm_g, w_in, group_fn, tm):
    return pl.pallas_call(
        _even_in_kernel,
        grid=(n_rows // tm, 3),
        in_specs=[pl.BlockSpec((tm, D), lambda i, j: (i, 0)),
                  pl.BlockSpec((1, D), lambda i, j: (0, 0)),
                  _mod_spec(0, group_fn), _mod_spec(1, group_fn),
                  pl.BlockSpec((D, HALF), lambda i, j: (0, j))],
        out_specs=[pl.BlockSpec((tm, HALF), lambda i, j: (i, jnp.minimum(j, 1))),
                   pl.BlockSpec((tm, HALF), lambda i, j: (i, 0))],
        out_shape=[jax.ShapeDtypeStruct((n_rows, 2 * HALF), BF16),
                   jax.ShapeDtypeStruct((n_rows, HALF), F32)],
        scratch_shapes=[pltpu.VMEM((tm, D), BF16)],
        compiler_params=_cparams(("arbitrary", "arbitrary"), 40),
        name="even_in",
    )(x, norm_g, mod_l, mod_l, w_in)


def _band(d, w):
    inside = lax.bitcast_convert_type(d + w // 2, jnp.uint32) < jnp.uint32(w)
    return jnp.where(inside, 1.0, 0.0).astype(BF16)


def _split_bf16(x):
    hi = x.astype(BF16)
    lo = (x - hi.astype(F32)).astype(BF16)
    return hi, lo


def _even_mix_kernel(u_ref, v_ref, z_ref, zp_ref, zn_ref, lng_ref, lnb_ref, ws_ref, bs_ref,
                     wp_ref, ps_ref, y_ref, *, tm, seq_len):
    nt = seq_len // tm
    it = pl.program_id(0) % nt
    vn = _layer_norm(v_ref[...].astype(F32), lng_ref[...], lnb_ref[...]).astype(BF16)
    for g in range(A_GROUPS):
        cs = slice(g * A_GW, (g + 1) * A_GW)
        for c in range(tm // CHUNK):
            rs = slice(c * CHUNK, (c + 1) * CHUNK)
            mixed = _dot(ws_ref[g], vn[rs, cs]) + bs_ref[g]
            y_ref[rs, cs] = (u_ref[rs, cs].astype(F32) * mixed).astype(BF16)
    z = z_ref[...]
    zp = jnp.where(it > 0, zp_ref[...], 0.0)
    zn = jnp.where(it < nt - 1, zn_ref[...], 0.0)
    z_hi, z_lo = _split_bf16(z)
    zp_hi, zp_lo = _split_bf16(zp)
    zn_hi, zn_lo = _split_bf16(zn)
    d_main = (lax.broadcasted_iota(jnp.int32, (tm, tm), 1)
              - lax.broadcasted_iota(jnp.int32, (tm, tm), 0))
    d_halo = (lax.broadcasted_iota(jnp.int32, (tm, HALO), 1)
              - lax.broadcasted_iota(jnp.int32, (tm, HALO), 0))
    pos = it * tm + lax.broadcasted_iota(jnp.int32, (tm, 1), 0)
    for g, w in enumerate(POOL_WINDOWS):
        cs = slice(g * B_GW, (g + 1) * B_GW)
        bm_ = _band(d_main, w)
        bp = _band(d_halo - HALO, w)
        bn = _band(d_halo + tm, w)
        tot = (_dot(bm_, z_hi[:, cs]) + _dot(bm_, z_lo[:, cs])
               + _dot(bp, zp_hi[:, cs]) + _dot(bp, zp_lo[:, cs])
               + _dot(bn, zn_hi[:, cs]) + _dot(bn, zn_lo[:, cs]))
        cnt = (jnp.minimum(pos + w // 2, seq_len) - jnp.maximum(pos - w // 2, 0)).astype(F32)
        pooled = (tot / cnt - z[:, cs]).astype(BF16)
        y_ref[:, HALF + g * B_GW:HALF + (g + 1) * B_GW] = (
            _dot(pooled, wp_ref[g]) * ps_ref[:, cs]).astype(BF16)


def _even_mix(uv, z, n_rows, seq_len, ln_g, ln_b, ws, bs, wp, ps, tm):
    hb = tm // HALO
    n_hblk = n_rows // HALO
    kern = functools.partial(_even_mix_kernel, tm=tm, seq_len=seq_len)
    const = lambda *shape: pl.BlockSpec(shape, lambda i: (0,) * len(shape))
    return pl.pallas_call(
        kern,
        grid=(n_rows // tm,),
        in_specs=[pl.BlockSpec((tm, HALF), lambda i: (i, 0)),
                  pl.BlockSpec((tm, HALF), lambda i: (i, 1)),
                  pl.BlockSpec((tm, HALF), lambda i: (i, 0)),
                  pl.BlockSpec((HALO, HALF), lambda i: (jnp.maximum(i * hb - 1, 0), 0)),
                  pl.BlockSpec((HALO, HALF), lambda i: (jnp.minimum((i + 1) * hb, n_hblk - 1), 0)),
                  const(1, HALF), const(1, HALF),
                  const(A_GROUPS, CHUNK, CHUNK), const(A_GROUPS, CHUNK, 1),
                  const(len(POOL_WINDOWS), B_GW, B_GW), const(1, HALF)],
        out_specs=pl.BlockSpec((tm, D), lambda i: (i, 0)),
        out_shape=jax.ShapeDtypeStruct((n_rows, D), BF16),
        compiler_params=_cparams(("arbitrary",), 40),
        name="even_mix",
    )(uv, uv, z, z, z, ln_g, ln_b, ws, bs, wp, ps)


def _mix_out_kernel(ya_ref, yb_ref, w_ref, x_ref, g1_ref, n2_ref, sh_ref, sc_ref, rwh_ref, rwl_ref,
                    xo_ref, h_ref, lg_ref):
    o = _dot(ya_ref[...], w_ref[0]) + _dot(yb_ref[...], w_ref[1])
    xn = x_ref[...] + g1_ref[...] * o
    xo_ref[...] = xn
    h = _rms_mod(xn, n2_ref[...], sh_ref[...], sc_ref[...])
    h_ref[...] = h
    h_hi, h_lo = _split_bf16(h)
    lg_ref[...] = _dot(h_hi, rwh_ref[...]) + _dot(h_hi, rwl_ref[...]) + _dot(h_lo, rwh_ref[...])


def _mix_out(ya, ya_col, yb, yb_col, w_out, x, n_rows, mod_l, norm2_g, rw_hi, rw_lo, group_fn, tm):
    const = lambda *shape: pl.BlockSpec(shape, lambda i: (0,) * len(shape))
    return pl.pallas_call(
        _mix_out_kernel,
        grid=(n_rows // tm,),
        in_specs=[pl.BlockSpec((tm, HALF), lambda i: (i, ya_col)),
                  pl.BlockSpec((tm, HALF), lambda i: (i, yb_col)),
                  const(2, HALF, D),
                  pl.BlockSpec((tm, D), lambda i: (i, 0)),
                  _mod_spec(2, group_fn), const(1, D), _mod_spec(3, group_fn), _mod_spec(4, group_fn),
                  const(D, N_EXPERTS), const(D, N_EXPERTS)],
        out_specs=[pl.BlockSpec((tm, D), lambda i: (i, 0)),
                   pl.BlockSpec((tm, D), lambda i: (i, 0)),
                   pl.BlockSpec((tm, N_EXPERTS), lambda i: (i, 0))],
        out_shape=[jax.ShapeDtypeStruct((n_rows, D), F32),
                   jax.ShapeDtypeStruct((n_rows, D), F32),
                   jax.ShapeDtypeStruct((n_rows, N_EXPERTS), F32)],
        compiler_params=_cparams(("arbitrary",), 48),
        name="mix_out",
    )(ya, yb, w_out, x, mod_l, norm2_g, mod_l, mod_l, rw_hi, rw_lo)


def _rope(x, cos, sin_signed, first_half):
    partner = jnp.where(first_half, pltpu.roll(x, 128 - AXIS_DIM // 2, 1), pltpu.roll(x, AXIS_DIM // 2, 1))
    return x * cos + partner * sin_signed


def _odd_in_kernel(x_ref, g_ref, sh_ref, sc_ref, w_ref, wkv_ref, cos_ref, sin_ref,
                   zc_ref, q_ref, k_ref, v_ref, h_ref, a_ref):
    j = pl.program_id(1)
    tm = x_ref.shape[0]

    @pl.when(j == 0)
    def _():
        h_ref[...] = _rms_mod(x_ref[...], g_ref[...], sh_ref[...], sc_ref[...]).astype(BF16)

    first_half = (lax.broadcasted_iota(jnp.int32, (tm, 128), 1) % AXIS_DIM) < (AXIS_DIM // 2)

    @pl.when(j == 0)
    def _():
        a_ref[...] = _dot(h_ref[...], w_ref[...])

    @pl.when(j == 1)
    def _():
        zc_ref[...] = a_ref[...] * jax.nn.sigmoid(_dot(h_ref[...], w_ref[...]))

    @pl.when(j == 2)
    def _():
        q = _dot(h_ref[...], w_ref[...])
        cos = cos_ref[...]
        sin = sin_ref[...]
        for b in range(HALF // 128):
            cs = slice(b * 128, (b + 1) * 128)
            q_ref[:, cs] = (_rope(q[:, cs], cos, sin, first_half) * (HEAD_DIM ** -0.5)).astype(BF16)

    @pl.when(j == 3)
    def _():
        kv = _dot(h_ref[...], wkv_ref[...])
        k_ref[...] = _rope(kv[:, :KV_W], cos_ref[...], sin_ref[...], first_half).astype(BF16)
        v_ref[...] = kv[:, KV_W:].astype(BF16)


def _odd_in(x, n_rows, mod_l, norm_g, w_main, w_kv, cos_t, sin_t, group_fn, tm):
    n_pos_blk = cos_t.shape[0] // tm
    row = lambda w: pl.BlockSpec((tm, w), lambda i, j: (i, 0))
    return pl.pallas_call(
        _odd_in_kernel,
        grid=(n_rows // tm, 4),
        in_specs=[pl.BlockSpec((tm, D), lambda i, j: (i, 0)),
                  pl.BlockSpec((1, D), lambda i, j: (0, 0)),
                  _mod_spec(0, group_fn), _mod_spec(1, group_fn),
                  pl.BlockSpec((D, HALF), lambda i, j: (0, jnp.minimum(j, 2))),
                  pl.BlockSpec((D, 2 * KV_W), lambda i, j: (0, 0)),
                  pl.BlockSpec((tm, 128), lambda i, j: (i % n_pos_blk, 0)),
                  pl.BlockSpec((tm, 128), lambda i, j: (i % n_pos_blk, 0))],
        out_specs=[row(HALF), row(HALF), row(KV_W), row(KV_W)],
        out_shape=[jax.ShapeDtypeStruct((n_rows, HALF), F32),
                   jax.ShapeDtypeStruct((n_rows, HALF), BF16),
                   jax.ShapeDtypeStruct((n_rows, KV_W), BF16),
                   jax.ShapeDtypeStruct((n_rows, KV_W), BF16)],
        scratch_shapes=[pltpu.VMEM((tm, D), BF16), pltpu.VMEM((tm, HALF), F32)],
        compiler_params=_cparams(("arbitrary", "arbitrary"), 40),
        name="odd_in",
    )(x, norm_g, mod_l, mod_l, w_main, w_kv, cos_t, sin_t)


def _conv_kernel(z_ref, zp_ref, zn_ref, w_ref, b_ref, lng_ref, lnb_ref, y_ref, ze_ref, c_ref, *, tm, seq_len):
    nt = seq_len // tm
    it = pl.program_id(0) % nt
    ze_ref[0:CONV_HALO, :] = jnp.where(it > 0, zp_ref[...], 0.0)
    ze_ref[CONV_HALO:CONV_HALO + tm, :] = z_ref[...]
    ze_ref[CONV_HALO + tm:, :] = jnp.where(it < nt - 1, zn_ref[...], 0.0)
    rc = 64
    base = CONV_HALO - CONV_W // 2

    def lane_block(cb, _):
        cs = pl.ds(pl.multiple_of(cb * 128, 128), 128)
        for r in range(tm // rc):
            acc = jnp.zeros((rc, 128), F32)
            for t in range(CONV_W):
                acc = acc + w_ref[t:t + 1, cs] * ze_ref[base + r * rc + t:base + r * rc + t + rc, cs]
            c_ref[r * rc:(r + 1) * rc, cs] = acc
        return 0

    lax.fori_loop(0, HALF // 128, lane_block, 0)
    y = _layer_norm(c_ref[...] + b_ref[...], lng_ref[...], lnb_ref[...])
    y_ref[...] = _silu(y).astype(BF16)


def _conv_module(zc, n_rows, seq_len, conv_w, conv_b, ln_g, ln_b, tm):
    hb = tm // CONV_HALO
    n_hblk = n_rows // CONV_HALO
    kern = functools.partial(_conv_kernel, tm=tm, seq_len=seq_len)
    const = lambda *shape: pl.BlockSpec(shape, lambda i: (0,) * len(shape))
    return pl.pallas_call(
        kern,
        grid=(n_rows // tm,),
        in_specs=[pl.BlockSpec((tm, HALF), lambda i: (i, 0)),
                  pl.BlockSpec((CONV_HALO, HALF), lambda i: (jnp.maximum(i * hb - 1, 0), 0)),
                  pl.BlockSpec((CONV_HALO, HALF), lambda i: (jnp.minimum((i + 1) * hb, n_hblk - 1), 0)),
                  const(CONV_W, HALF), const(1, HALF), const(1, HALF), const(1, HALF)],
        out_specs=pl.BlockSpec((tm, HALF), lambda i: (i, 0)),
        out_shape=jax.ShapeDtypeStruct((n_rows, HALF), BF16),
        scratch_shapes=[pltpu.VMEM((tm + 2 * CONV_HALO, HALF), F32), pltpu.VMEM((tm, HALF), F32)],
        compiler_params=_cparams(("arbitrary",), 32),
        name="conv_module",
    )(zc, zc, zc, conv_w, conv_b, ln_g, ln_b)


def _attn_kernel(sink_ref, q_ref, kp_ref, kc_ref, kn_ref, vp_ref, vc_ref, vn_ref, kx_ref, vx_ref, o_ref):
    i = pl.program_id(1)
    n_keys = 3 * ATT_BLK + CTX_LEN
    kb = jnp.concatenate([kp_ref[...], kc_ref[...], kn_ref[...], kx_ref[...]], axis=0)
    vb = jnp.concatenate([vp_ref[...], vc_ref[...], vn_ref[...], vx_ref[...]], axis=0)
    key_lane = lax.broadcasted_iota(jnp.int32, (n_keys, 2 * HEAD_DIM), 1)
    k_head = [jnp.where(key_lane < HEAD_DIM, kb, jnp.zeros_like(kb)),
              jnp.where(key_lane >= HEAD_DIM, kb, jnp.zeros_like(kb))]
    r = lax.broadcasted_iota(jnp.int32, (ATT_BLK, n_keys), 0)
    c = lax.broadcasted_iota(jnp.int32, (ATT_BLK, n_keys), 1)
    kpos = c + (i - 1) * ATT_BLK
    band_ok = (lax.bitcast_convert_type(c - r, jnp.uint32) <= jnp.uint32(2 * WINDOW))
    in_seq = lax.bitcast_convert_type(kpos, jnp.uint32) < jnp.uint32(SEQ)
    bias = jnp.where(c >= 3 * ATT_BLK, 0.0, jnp.where(band_ok, jnp.where(in_seq, 0.0, -jnp.inf), -jnp.inf))
    out_lane = lax.broadcasted_iota(jnp.int32, (ATT_BLK, 2 * HEAD_DIM), 1)
    for g in range(Q_PER_KV):
        qg = q_ref[:, g * 128:(g + 1) * 128]
        outs = []
        for kvh in range(N_KV_HEADS):
            sk = sink_ref[kvh * Q_PER_KV + g]
            s = lax.dot_general(qg, k_head[kvh], (((1,), (1,)), ((), ())), preferred_element_type=F32) + bias
            m = jnp.maximum(jnp.max(s, axis=-1, keepdims=True), sk)
            e = jnp.exp(s - m)
            den = jnp.sum(e, axis=-1, keepdims=True) + jnp.exp(sk - m)
            outs.append(_dot(e.astype(BF16), vb) / den)
        o_ref[:, g * 128:(g + 1) * 128] = jnp.where(out_lane < HEAD_DIM, outs[0], outs[1]).astype(BF16)


def _attention(q, k, v, kx, vx, sink):
    nb = SEQ // ATT_BLK
    blk = lambda w, fn: pl.BlockSpec((ATT_BLK, w), fn)
    prev = lambda b, i, s: (b * nb + jnp.maximum(i - 1, 0), 0)
    cur = lambda b, i, s: (b * nb + i, 0)
    nxt = lambda b, i, s: (b * nb + jnp.minimum(i + 1, nb - 1), 0)
    ctx = pl.BlockSpec((CTX_LEN, KV_W), lambda b, i, s: (b, 0))
    grid_spec = pltpu.PrefetchScalarGridSpec(
        num_scalar_prefetch=1,
        grid=(BATCH, nb),
        in_specs=[blk(HALF, cur), blk(KV_W, prev), blk(KV_W, cur), blk(KV_W, nxt),
                  blk(KV_W, prev), blk(KV_W, cur), blk(KV_W, nxt), ctx, ctx],
        out_specs=blk(HALF, cur),
    )
    return pl.pallas_call(
        _attn_kernel,
        grid_spec=grid_spec,
        out_shape=jax.ShapeDtypeStruct((N_LAT, HALF), BF16),
        compiler_params=_cparams(("arbitrary", "arbitrary"), 32),
        name="attention",
    )(sink, q, k, k, k, v, v, v, kx, vx)


def _moe_kernel(be_ref, na_ref, src_ref, srcn_ref, dst_ref, h_hbm, wg_ref, wu_ref, wd_ref, y_hbm,
                xb, yb, wgb, wub, wdb, sem_in, sem_out):
    i = pl.program_id(0)
    n_steps = pl.num_programs(0)
    n_act = na_ref[0]
    slot = i % 2
    bm = xb.shape[1]

    def gather(idx_ref, s):
        def body(r, _):
            pltpu.make_async_copy(h_hbm.at[pl.ds(idx_ref[0, 0, r], 1), :], xb.at[s, pl.ds(r, 1), :],
                                  sem_in.at[s]).start()
            return 0
        lax.fori_loop(0, bm, body, 0, unroll=8)

    def wait_gather(s):
        pltpu.make_async_copy(h_hbm.at[pl.ds(0, bm), :], xb.at[s], sem_in.at[s]).wait()

    def wait_scatter(s):
        pltpu.make_async_copy(yb.at[s], y_hbm.at[pl.ds(0, bm), :], sem_out.at[s]).wait()

    @pl.when(jnp.logical_and(i == 0, n_act > 0))
    def _():
        gather(src_ref, 0)

    @pl.when(i + 1 < n_act)
    def _():
        gather(srcn_ref, 1 - slot)

    @pl.when(jnp.logical_and(i >= 2, i - 2 < n_act))
    def _():
        wait_scatter(slot)

    @pl.when(i < n_act)
    def _():
        new_expert = jnp.logical_or(i == 0, be_ref[i] != be_ref[jnp.maximum(i - 1, 0)])

        @pl.when(new_expert)
        def _():
            wgb[...] = wg_ref[...].astype(BF16)
            wub[...] = wu_ref[...].astype(BF16)
            wdb[...] = wd_ref[...].astype(BF16)

        wait_gather(slot)
        x = xb[slot].astype(BF16)
        hid = (_silu(_dot(x, wgb[...])) * _dot(x, wub[...])).astype(BF16)
        yb[slot] = _dot(hid, wdb[...])

        def body(r, _):
            pltpu.make_async_copy(yb.at[slot, pl.ds(r, 1), :], y_hbm.at[pl.ds(dst_ref[0, 0, r], 1), :],
                                  sem_out.at[slot]).start()
            return 0
        lax.fori_loop(0, bm, body, 0, unroll=8)

    @pl.when(i == n_steps - 1)
    def _():
        @pl.when(jnp.logical_and(i >= 1, i - 1 < n_act))
        def _():
            wait_scatter(1 - slot)

        @pl.when(i < n_act)
        def _():
            wait_scatter(slot)


def _moe_experts(h2, blk_exp, n_act, src, dst, w_gate, w_up, w_down, n_out_rows):
    n_blocks = blk_exp.shape[0]
    bm = MOE_BM
    smem_blk = lambda fn: pl.BlockSpec((1, 1, bm), fn, memory_space=pltpu.SMEM)
    grid_spec = pltpu.PrefetchScalarGridSpec(
        num_scalar_prefetch=2,
        grid=(n_blocks,),
        in_specs=[smem_blk(lambda i, be, na: (i, 0, 0)),
                  smem_blk(lambda i, be, na: (jnp.minimum(i + 1, n_blocks - 1), 0, 0)),
                  smem_blk(lambda i, be, na: (i, 0, 0)),
                  pl.BlockSpec(memory_space=pl.ANY),
                  pl.BlockSpec((None, D, EXPERT_FF), lambda i, be, na: (be[i], 0, 0)),
                  pl.BlockSpec((None, D, EXPERT_FF), lambda i, be, na: (be[i], 0, 0)),
                  pl.BlockSpec((None, EXPERT_FF, D), lambda i, be, na: (be[i], 0, 0))],
        out_specs=pl.BlockSpec(memory_space=pl.ANY),
        scratch_shapes=[pltpu.VMEM((2, bm, D), F32), pltpu.VMEM((2, bm, D), F32),
                        pltpu.VMEM((D, EXPERT_FF), BF16), pltpu.VMEM((D, EXPERT_FF), BF16),
                        pltpu.VMEM((EXPERT_FF, D), BF16),
                        pltpu.SemaphoreType.DMA((2,)), pltpu.SemaphoreType.DMA((2,))],
    )
    src3 = src.reshape(n_blocks, 1, bm)
    return pl.pallas_call(
        _moe_kernel,
        grid_spec=grid_spec,
        out_shape=jax.ShapeDtypeStruct((n_out_rows, D), F32),
        compiler_params=_cparams(("arbitrary",), 52),
        name="moe_experts",
    )(blk_exp, n_act, src3, src3, dst.reshape(n_blocks, 1, bm), h2, w_gate, w_up, w_down)


def _shared_kernel(h_ref, wg_ref, wu_ref, wd_ref, o_ref):
    x = h_ref[...].astype(BF16)
    hid = (_silu(_dot(x, wg_ref[...])) * _dot(x, wu_ref[...])).astype(BF16)
    o_ref[...] = _dot(hid, wd_ref[...])


def _shared_expert(h2, wg, wu, wd, tm):
    n_rows = h2.shape[0]
    const = lambda *shape: pl.BlockSpec(shape, lambda i: (0,) * len(shape))
    return pl.pallas_call(
        _shared_kernel,
        grid=(n_rows // tm,),
        in_specs=[pl.BlockSpec((tm, D), lambda i: (i, 0)),
                  const(D, EXPERT_FF), const(D, EXPERT_FF), const(EXPERT_FF, D)],
        out_specs=pl.BlockSpec((tm, D), lambda i: (i, 0)),
        out_shape=jax.ShapeDtypeStruct((n_rows, D), F32),
        compiler_params=_cparams(("arbitrary",), 40),
        name="shared_expert",
    )(h2, wg, wu, wd)


def _combine_kernel(*refs, final):
    x_ref, sh_ref, gt_ref, g2_ref = refs[:4]
    y_refs = refs[4:4 + TOP_K]
    fg_ref = refs[4 + TOP_K] if final else None
    o_ref = refs[-1]
    gt = gt_ref[...]
    acc = sh_ref[...]
    for k in range(TOP_K):
        acc = acc + y_refs[k][...] * gt[:, k:k + 1]
    xn = x_ref[...] + g2_ref[...] * acc
    if final:
        xn = xn * lax.rsqrt(jnp.mean(xn * xn, axis=-1, keepdims=True) + EPS) * fg_ref[...]
    o_ref[...] = xn


def _combine(x, shared, gates, mod_l, y_tk, n_tok, final_g, tm):
    nt = n_tok // tm
    group_fn = lambda i: jnp.minimum(i // (SEQ // tm), CTX_GROUP)
    row = pl.BlockSpec((tm, D), lambda i: (i, 0))
    in_specs = [row, row, pl.BlockSpec((tm, TOP_K), lambda i: (i, 0)), _mod_spec(5, group_fn)]
    in_specs += [pl.BlockSpec((tm, D), functools.partial(lambda i, k: (k * nt + i, 0), k=k)) for k in range(TOP_K)]
    args = [x, shared, gates, mod_l] + [y_tk] * TOP_K
    final = final_g is not None
    if final:
        in_specs.append(pl.BlockSpec((1, D), lambda i: (0, 0)))
        args.append(final_g)
    return pl.pallas_call(
        functools.partial(_combine_kernel, final=final),
        grid=(nt,),
        in_specs=in_specs,
        out_specs=row,
        out_shape=jax.ShapeDtypeStruct((n_tok, D), F32),
        compiler_params=_cparams(("arbitrary",), 40),
        name="moe_combine",
    )(*args)


def _route(logits, router_bias):
    n_tok = logits.shape[0]
    per_grp = N_EXPERTS // N_EXPERT_GROUPS
    scores = jax.nn.sigmoid(logits)
    biased = scores + router_bias.astype(F32)
    grp = biased.reshape(n_tok, N_EXPERT_GROUPS, per_grp)
    grp_score = lax.top_k(grp, 2)[0].sum(-1)
    _, top_grp = lax.top_k(grp_score, TOPK_GROUPS)
    grp_mask = jax.nn.one_hot(top_grp, N_EXPERT_GROUPS, dtype=F32).sum(1) > 0
    expert_mask = jnp.repeat(grp_mask, per_grp, axis=1)
    _, idx = lax.top_k(jnp.where(expert_mask, biased, -jnp.inf), TOP_K)
    gates = jnp.take_along_axis(scores, idx, axis=1)
    gates = gates / gates.sum(-1, keepdims=True) * ROUTED_SCALE
    return idx, gates


def _dispatch_plan(idx, n_tok):
    bm = MOE_BM
    n_assign = n_tok * TOP_K
    n_blocks = n_assign // bm + N_EXPERTS
    n_rows = n_blocks * bm
    flat_e = idx.reshape(-1).astype(jnp.int32)
    order = jnp.argsort(flat_e).astype(jnp.int32)
    sizes = jnp.bincount(flat_e, length=N_EXPERTS).astype(jnp.int32)
    padded = (sizes + bm - 1) // bm * bm
    start = jnp.cumsum(sizes) - sizes
    pad_end = jnp.cumsum(padded)
    pad_start = pad_end - padded
    blk_exp = jnp.minimum(jnp.searchsorted(pad_end, jnp.arange(n_blocks, dtype=jnp.int32) * bm, side='right'),
                          N_EXPERTS - 1).astype(jnp.int32)
    n_act = (pad_end[-1] // bm).astype(jnp.int32).reshape(1)
    s = jnp.arange(n_rows, dtype=jnp.int32)
    e_s = jnp.repeat(blk_exp, bm)
    off = s - pad_start[e_s]
    valid = off < sizes[e_s]
    asg = order[jnp.clip(start[e_s] + off, 0, n_assign - 1)]
    pad_rank = jnp.cumsum(jnp.logical_not(valid).astype(jnp.int32)) - 1
    src = jnp.where(valid, asg // TOP_K, 0).astype(jnp.int32)
    dst = jnp.where(valid, (asg % TOP_K) * n_tok + asg // TOP_K, n_assign + pad_rank).astype(jnp.int32)
    return blk_exp, n_act, src, dst, n_rows


def _moe_layer(x, h2, logits, router_bias, mod_l, wts, final_g):
    w_gate, w_up, w_down, ws_gate, ws_up, ws_down = wts
    n_tok = h2.shape[0]
    idx, gates = _route(logits, router_bias)
    blk_exp, n_act, src, dst, n_rows = _dispatch_plan(idx, n_tok)
    y_tk = _moe_experts(h2, blk_exp, n_act, src, dst, w_gate, w_up, w_down, n_rows)
    shared = _shared_expert(h2, ws_gate.astype(BF16), ws_up.astype(BF16), ws_down.astype(BF16), 512)
    return _combine(x, shared, gates, mod_l, y_tk, n_tok, final_g, 128)


def _rope_tables():
    rows = SEQ // GRID_W
    row = jnp.broadcast_to(jnp.arange(rows)[:, None], (rows, GRID_W)).reshape(-1).astype(F32)
    col = jnp.broadcast_to(jnp.arange(GRID_W)[None, :], (rows, GRID_W)).reshape(-1).astype(F32)
    inv = ROPE_BASE ** (-jnp.arange(0, AXIS_DIM, 2, dtype=F32) / AXIS_DIM)
    ang_r = row[:, None] * inv
    ang_c = col[:, None] * inv
    ang = jnp.concatenate([ang_r, ang_r, ang_c, ang_c], axis=-1)
    cos, sin = jnp.cos(ang), jnp.sin(ang)
    sign = jnp.where((jnp.arange(HEAD_DIM) % AXIS_DIM) < AXIS_DIM // 2, -1.0, 1.0).astype(F32)
    return jnp.tile(cos, (1, 2)), jnp.tile(sin * sign, (1, 2))


def _head_perm():
    g = jnp.arange(Q_PER_KV)[:, None, None]
    kvh = jnp.arange(N_KV_HEADS)[None, :, None]
    d = jnp.arange(HEAD_DIM)[None, None, :]
    return ((kvh * Q_PER_KV + g) * HEAD_DIM + d).reshape(-1)


def kernel(x, c, ctx, c_ctx, ada_w, ada_b, norm1_g, norm2_g, even_w_in, gmlp_ln_g, gmlp_ln_b, gmlp_ws, gmlp_bs, pool_w, pool_scale, even_w_out, odd_w_in, conv_w, conv_b, conv_ln_g, conv_ln_b, attn_sink, odd_w_out, router_w, router_bias, exp_w_gate, exp_w_up, exp_w_down, shared_w_gate, shared_w_up, shared_w_down, final_g):
    mod = _ada_mod(c, c_ctx, ada_w, ada_b)
    xl = x.reshape(N_LAT, D)
    xc = ctx.reshape(N_CTX, D)
    row = lambda a: a.reshape(1, -1)

    def router_split(i):
        hi = router_w[i].astype(BF16)
        lo = (router_w[i] - hi.astype(F32)).astype(BF16)
        return hi, lo

    def moe_weights(i):
        return (exp_w_gate[i], exp_w_up[i], exp_w_down[i], shared_w_gate[i], shared_w_up[i], shared_w_down[i])

    w_in0 = even_w_in[0].astype(BF16)
    w_out0 = even_w_out[0].astype(BF16).reshape(2, HALF, D)
    ws0 = gmlp_ws[0].astype(BF16)
    bs0 = gmlp_bs[0].reshape(A_GROUPS, CHUNK, 1)
    wp0 = pool_w[0].astype(BF16)
    rw_hi, rw_lo = router_split(0)
    streams = []
    for xs, n_rows, seq_len, group_fn_of in ((xl, N_LAT, SEQ, _lat_group), (xc, N_CTX, CTX_LEN, lambda tm: _ctx_group)):
        uv, z = _even_in(xs, n_rows, mod[0], row(norm1_g[0]), w_in0, group_fn_of(512), 512)
        y = _even_mix(uv, z, n_rows, seq_len, row(gmlp_ln_g[0]), row(gmlp_ln_b[0]), ws0, bs0, wp0,
                      row(pool_scale[0]), 256)
        streams.append(_mix_out(y, 0, y, 1, w_out0, xs, n_rows, mod[0], row(norm2_g[0]), rw_hi, rw_lo,
                                group_fn_of(256), 256))
    x_cat = jnp.concatenate([streams[0][0], streams[1][0]], axis=0)
    h_cat = jnp.concatenate([streams[0][1], streams[1][1]], axis=0)
    lg_cat = jnp.concatenate([streams[0][2], streams[1][2]], axis=0)
    x_cat = _moe_layer(x_cat, h_cat, lg_cat, router_bias[0], mod[0], moe_weights(0), None)

    perm = _head_perm()
    w_in1 = odd_w_in[0]
    w_main = jnp.concatenate([w_in1[:, :2 * HALF], w_in1[:, 2 * HALF:3 * HALF][:, perm]], axis=1).astype(BF16)
    w_kv = w_in1[:, 3 * HALF:].astype(BF16)
    w_out1 = odd_w_out[0]
    w_out1 = jnp.stack([w_out1[:HALF], w_out1[HALF:][perm]], axis=0).astype(BF16)
    cos_t, sin_t = _rope_tables()
    ones_t = jnp.ones((512, 128), F32)
    zeros_t = jnp.zeros((512, 128), F32)
    rw_hi, rw_lo = router_split(1)
    x1 = x_cat[:N_LAT]
    xc1 = x_cat[N_LAT:]
    zc, q, k, v = _odd_in(x1, N_LAT, mod[1], row(norm1_g[1]), w_main, w_kv, cos_t, sin_t, _lat_group(512), 512)
    _, _, kx, vx = _odd_in(xc1, N_CTX, mod[1], row(norm1_g[1]), w_main, w_kv, ones_t, zeros_t, _ctx_group, 512)
    y_conv = _conv_module(zc, N_LAT, SEQ, conv_w[0], row(conv_b[0]), row(conv_ln_g[0]), row(conv_ln_b[0]), 256)
    y_attn = _attention(q, k, v, kx, vx, attn_sink[0].astype(F32))
    x2, h2, lg = _mix_out(y_conv, 0, y_attn, 0, w_out1, x1, N_LAT, mod[1], row(norm2_g[1]), rw_hi, rw_lo,
                          _lat_group(256), 256)
    out = _moe_layer(x2, h2, lg, router_bias[1], mod[1], moe_weights(1), row(final_g))
    return out.reshape(BATCH, SEQ, D)
```

```python
import functools

import jax
import jax.numpy as jnp
from jax import lax
from jax.experimental import pallas as pl
from jax.experimental.pallas import tpu as pltpu

F32 = jnp.float32
BF16 = jnp.bfloat16
U32 = jnp.uint32
I32 = jnp.int32

D = 2048
BATCH = 4
SEQ = 4096
DEPTH = 2
GRID_W = 64
CTX_LEN = 256
HALF = D // 2
CHUNK = 128
A_GROUPS = 4
A_GW = HALF // A_GROUPS
POOL_WINDOWS = (2, 4, 8, 16)
B_GW = HALF // len(POOL_WINDOWS)
CONV_W = 31
HEAD_DIM = 64
N_Q_HEADS = HALF // HEAD_DIM
N_KV_HEADS = 2
Q_PER_KV = N_Q_HEADS // N_KV_HEADS
KV_W = N_KV_HEADS * HEAD_DIM
ATT_BLK = 128
WINDOW = 128
AXIS_DIM = HEAD_DIM // 2
ROPE_BASE = 10000.0
N_EXPERTS = 64
N_EXPERT_GROUPS = 8
PER_GROUP = N_EXPERTS // N_EXPERT_GROUPS
TOPK_GROUPS = 4
TOP_K = 8
EXPERT_FF = 512
ROUTED_SCALE = 2.5
EPS = 1e-6

N_LAT = BATCH * SEQ
N_CTX = BATCH * CTX_LEN
N_ALL = N_LAT + N_CTX
CTX_GROUP = BATCH
HALO = 128
CONV_HALO = 16
MOE_BM = 256

MIB = 1024 * 1024


def _cparams(sem, vmem_mib):
    return pltpu.CompilerParams(dimension_semantics=sem, vmem_limit_bytes=vmem_mib * MIB)


def _dot(a, b):
    return jnp.dot(a, b, preferred_element_type=F32)


def _dot_nt(a, b):
    return lax.dot_general(a, b, (((1,), (1,)), ((), ())), preferred_element_type=F32)


def _rms_mod(x, g, sh, sc):
    y = x * lax.rsqrt(jnp.mean(x * x, axis=-1, keepdims=True) + EPS) * g
    return y * (1.0 + sc) + sh


def _layer_norm(x, g, b):
    mu = jnp.mean(x, axis=-1, keepdims=True)
    xc = x - mu
    var = jnp.mean(xc * xc, axis=-1, keepdims=True)
    return xc * lax.rsqrt(var + EPS) * g + b


def _gelu(x):
    return 0.5 * x * (1.0 + lax.erf(x * (2.0 ** -0.5)))


def _silu(x):
    return x * jax.nn.sigmoid(x)


def _split_bf16(x):
    hi = x.astype(BF16)
    lo = (x - hi.astype(F32)).astype(BF16)
    return hi, lo


def _pack_halves(y):
    n = y.shape[1] // 2
    lo = lax.bitcast_convert_type(y[:, :n].astype(BF16).astype(F32), U32) >> 16
    hi = lax.bitcast_convert_type(y[:, n:].astype(BF16).astype(F32), U32) & U32(0xFFFF0000)
    return hi | lo


def _unpack_halves(p):
    lo = lax.bitcast_convert_type(p << 16, F32)
    hi = lax.bitcast_convert_type(p & U32(0xFFFF0000), F32)
    return lo, hi


def _unpack_bf16(p):
    lo, hi = _unpack_halves(p)
    return jnp.concatenate([lo.astype(BF16), hi.astype(BF16)], axis=1)


def _mod_spec(chunk, group_fn):
    return pl.BlockSpec((None, None, 1, D), lambda i, *_: (group_fn(i), chunk, 0, 0))


def _group_fn(tm):
    return lambda i: jnp.minimum(i // (SEQ // tm), CTX_GROUP)


def _const_spec(*shape):
    return pl.BlockSpec(shape, lambda *_: (0,) * len(shape))


def _ada_kernel(c_ref, w_ref, b_ref, o_ref):
    s = _silu(c_ref[...]).astype(BF16)
    o_ref[...] = _dot(s, w_ref[...].astype(BF16)) + b_ref[...]


def _ada_mod(c, c_ctx, ada_w, ada_b):
    tn = 1024
    cv = jnp.zeros((8, D), F32).at[:BATCH].set(c).at[CTX_GROUP].set(c_ctx)
    out = pl.pallas_call(
        _ada_kernel,
        grid=(DEPTH, 6 * D // tn),
        in_specs=[pl.BlockSpec((8, D), lambda l, j: (0, 0)),
                  pl.BlockSpec((None, D, tn), lambda l, j: (l, 0, j)),
                  pl.BlockSpec((None, 1, tn), lambda l, j: (l, 0, j))],
        out_specs=pl.BlockSpec((None, 8, tn), lambda l, j: (l, 0, j)),
        out_shape=jax.ShapeDtypeStruct((DEPTH, 8, 6 * D), F32),
        compiler_params=_cparams(("arbitrary", "arbitrary"), 40),
        name="ada_mod",
    )(cv, ada_w, ada_b.reshape(DEPTH, 1, 6 * D))
    return out.reshape(DEPTH, 8, 6, 1, D)


def _even_in_kernel(x_ref, g_ref, sh_ref, sc_ref, w_ref, uv_ref, z_ref, h_ref):
    j = pl.program_id(1)

    @pl.when(j == 0)
    def _():
        h_ref[...] = _rms_mod(x_ref[...], g_ref[...], sh_ref[...], sc_ref[...]).astype(BF16)

    acc = _dot(h_ref[...], w_ref[...])

    @pl.when(j < 2)
    def _():
        uv_ref[...] = _gelu(acc).astype(BF16)

    @pl.when(j == 2)
    def _():
        z_ref[...] = acc


def _even_in(x, mod_l, norm_g, w_in, tm):
    n_rows = x.shape[0]
    group_fn = _group_fn(tm)
    return pl.pallas_call(
        _even_in_kernel,
        grid=(n_rows // tm, 3),
        in_specs=[pl.BlockSpec((tm, D), lambda i, j: (i, 0)),
                  _const_spec(1, D),
                  _mod_spec(0, group_fn), _mod_spec(1, group_fn),
                  pl.BlockSpec((D, HALF), lambda i, j: (0, j))],
        out_specs=[pl.BlockSpec((tm, HALF), lambda i, j: (i, jnp.minimum(j, 1))),
                   pl.BlockSpec((tm, HALF), lambda i, j: (i, 0))],
        out_shape=[jax.ShapeDtypeStruct((n_rows, 2 * HALF), BF16),
                   jax.ShapeDtypeStruct((n_rows, HALF), F32)],
        scratch_shapes=[pltpu.VMEM((tm, D), BF16)],
        compiler_params=_cparams(("arbitrary", "arbitrary"), 40),
        name="even_in",
    )(x, norm_g, mod_l, mod_l, w_in)


def _band(d, w):
    inside = lax.bitcast_convert_type(d + w // 2, U32) < U32(w)
    return jnp.where(inside, 1.0, 0.0).astype(BF16)


def _seq_tile(i, tm):
    lat_tiles = N_LAT // tm
    is_lat = i < lat_tiles
    it = i % (SEQ // tm)
    first = jnp.logical_or(jnp.logical_not(is_lat), it == 0)
    last = jnp.logical_or(jnp.logical_not(is_lat), it == SEQ // tm - 1)
    pos0 = jnp.where(is_lat, it * tm, 0)
    seq_len = jnp.where(is_lat, SEQ, CTX_LEN)
    return first, last, pos0, seq_len


def _even_mix_kernel(u_ref, v_ref, z_ref, zp_ref, zn_ref, lng_ref, lnb_ref, ws_ref, bs_ref,
                     wp_ref, ps_ref, y_ref, *, tm):
    first, last, pos0, seq_len = _seq_tile(pl.program_id(0), tm)
    vn = _layer_norm(v_ref[...].astype(F32), lng_ref[...], lnb_ref[...]).astype(BF16)
    for g in range(A_GROUPS):
        cs = slice(g * A_GW, (g + 1) * A_GW)
        for c in range(tm // CHUNK):
            rs = slice(c * CHUNK, (c + 1) * CHUNK)
            mixed = _dot(ws_ref[g], vn[rs, cs]) + bs_ref[g]
            y_ref[rs, cs] = (u_ref[rs, cs].astype(F32) * mixed).astype(BF16)
    z = z_ref[...]
    zp = jnp.where(first, 0.0, zp_ref[...])
    zn = jnp.where(last, 0.0, zn_ref[...])
    z_hi, z_lo = _split_bf16(z)
    zp_hi, zp_lo = _split_bf16(zp)
    zn_hi, zn_lo = _split_bf16(zn)
    d_main = (lax.broadcasted_iota(I32, (tm, tm), 1) - lax.broadcasted_iota(I32, (tm, tm), 0))
    d_halo = (lax.broadcasted_iota(I32, (tm, HALO), 1) - lax.broadcasted_iota(I32, (tm, HALO), 0))
    pos = pos0 + lax.broadcasted_iota(I32, (tm, 1), 0)
    for g, w in enumerate(POOL_WINDOWS):
        cs = slice(g * B_GW, (g + 1) * B_GW)
        bm_ = _band(d_main, w)
        bp = _band(d_halo - HALO, w)
        bn = _band(d_halo + tm, w)
        tot = (_dot(bm_, z_hi[:, cs]) + _dot(bm_, z_lo[:, cs])
               + _dot(bp, zp_hi[:, cs]) + _dot(bp, zp_lo[:, cs])
               + _dot(bn, zn_hi[:, cs]) + _dot(bn, zn_lo[:, cs]))
        cnt = (jnp.minimum(pos + w // 2, seq_len) - jnp.maximum(pos - w // 2, 0)).astype(F32)
        pooled = (tot / cnt - z[:, cs]).astype(BF16)
        y_ref[:, HALF + g * B_GW:HALF + (g + 1) * B_GW] = (
            _dot(pooled, wp_ref[g]) * ps_ref[:, cs]).astype(BF16)


def _even_mix(uv, z, ln_g, ln_b, ws, bs, wp, ps, tm):
    n_rows = z.shape[0]
    hb = tm // HALO
    n_hblk = n_rows // HALO
    return pl.pallas_call(
        functools.partial(_even_mix_kernel, tm=tm),
        grid=(n_rows // tm,),
        in_specs=[pl.BlockSpec((tm, HALF), lambda i: (i, 0)),
                  pl.BlockSpec((tm, HALF), lambda i: (i, 1)),
                  pl.BlockSpec((tm, HALF), lambda i: (i, 0)),
                  pl.BlockSpec((HALO, HALF), lambda i: (jnp.maximum(i * hb - 1, 0), 0)),
                  pl.BlockSpec((HALO, HALF), lambda i: (jnp.minimum((i + 1) * hb, n_hblk - 1), 0)),
                  _const_spec(1, HALF), _const_spec(1, HALF),
                  _const_spec(A_GROUPS, CHUNK, CHUNK), _const_spec(A_GROUPS, CHUNK, 1),
                  _const_spec(len(POOL_WINDOWS), B_GW, B_GW), _const_spec(1, HALF)],
        out_specs=pl.BlockSpec((tm, D), lambda i: (i, 0)),
        out_shape=jax.ShapeDtypeStruct((n_rows, D), BF16),
        compiler_params=_cparams(("arbitrary",), 40),
        name="even_mix",
    )(uv, uv, z, z, z, ln_g, ln_b, ws, bs, wp, ps)


def _mix_out_kernel(ya_ref, yb_ref, w_ref, x_ref, g1_ref, n2_ref, sh_ref, sc_ref, rwh_ref, rwl_ref,
                    xo_ref, hp_ref, lg_ref):
    o = _dot(ya_ref[...], w_ref[0]) + _dot(yb_ref[...], w_ref[1])
    xn = x_ref[...] + g1_ref[...] * o
    xo_ref[...] = xn
    h = _rms_mod(xn, n2_ref[...], sh_ref[...], sc_ref[...])
    hp_ref[...] = _pack_halves(h)
    h_hi, h_lo = _split_bf16(h)
    lg_ref[...] = _dot_nt(rwh_ref[...], h_hi) + _dot_nt(rwl_ref[...], h_hi) + _dot_nt(rwh_ref[...], h_lo)


def _mix_out(ya, ya_col, yb, yb_col, w_out, x, n_rows, mod_l, norm2_g, rw_hi, rw_lo, tm):
    group_fn = _group_fn(tm)
    return pl.pallas_call(
        _mix_out_kernel,
        grid=(n_rows // tm,),
        in_specs=[pl.BlockSpec((tm, HALF), lambda i: (i, ya_col)),
                  pl.BlockSpec((tm, HALF), lambda i: (i, yb_col)),
                  _const_spec(2, HALF, D),
                  pl.BlockSpec((tm, D), lambda i: (i, 0)),
                  _mod_spec(2, group_fn), _const_spec(1, D), _mod_spec(3, group_fn), _mod_spec(4, group_fn),
                  _const_spec(N_EXPERTS, D), _const_spec(N_EXPERTS, D)],
        out_specs=[pl.BlockSpec((tm, D), lambda i: (i, 0)),
                   pl.BlockSpec((tm, HALF), lambda i: (i, 0)),
                   pl.BlockSpec((N_EXPERTS, tm), lambda i: (0, i))],
        out_shape=[jax.ShapeDtypeStruct((n_rows, D), F32),
                   jax.ShapeDtypeStruct((n_rows, HALF), U32),
                   jax.ShapeDtypeStruct((N_EXPERTS, n_rows), F32)],
        compiler_params=_cparams(("arbitrary",), 48),
        name="mix_out",
    )(ya, yb, w_out, x, mod_l, norm2_g, mod_l, mod_l, rw_hi, rw_lo)


def _rope(x, cos, sin_signed, first_half):
    partner = jnp.where(first_half, pltpu.roll(x, 128 - AXIS_DIM // 2, 1), pltpu.roll(x, AXIS_DIM // 2, 1))
    return x * cos + partner * sin_signed


def _odd_in_kernel(x_ref, g_ref, sh_ref, sc_ref, w_ref, wkv_ref, cos_ref, sin_ref,
                   zc_ref, q_ref, k_ref, v_ref, h_ref, a_ref):
    j = pl.program_id(1)
    tm = x_ref.shape[0]

    @pl.when(j == 0)
    def _():
        h_ref[...] = _rms_mod(x_ref[...], g_ref[...], sh_ref[...], sc_ref[...]).astype(BF16)

    first_half = (lax.broadcasted_iota(I32, (tm, 128), 1) % AXIS_DIM) < (AXIS_DIM // 2)

    @pl.when(j == 0)
    def _():
        a_ref[...] = _dot(h_ref[...], w_ref[...])

    @pl.when(j == 1)
    def _():
        zc_ref[...] = a_ref[...] * jax.nn.sigmoid(_dot(h_ref[...], w_ref[...]))

    @pl.when(j == 2)
    def _():
        q = _dot(h_ref[...], w_ref[...])
        cos = cos_ref[...]
        sin = sin_ref[...]
        for b in range(HALF // 128):
            cs = slice(b * 128, (b + 1) * 128)
            q_ref[:, cs] = (_rope(q[:, cs], cos, sin, first_half) * (HEAD_DIM ** -0.5)).astype(BF16)

    @pl.when(j == 3)
    def _():
        kv = _dot(h_ref[...], wkv_ref[...])
        k_ref[...] = _rope(kv[:, :KV_W], cos_ref[...], sin_ref[...], first_half).astype(BF16)
        v_ref[...] = kv[:, KV_W:].astype(BF16)


def _odd_in(x, mod_l, norm_g, w_main, w_kv, cos_t, sin_t, tm):
    n_rows = x.shape[0]
    group_fn = _group_fn(tm)
    lat_tiles = N_LAT // tm
    pos_blk = lambda i, j: (jnp.where(i < lat_tiles, i % (SEQ // tm), SEQ // tm), 0)
    row = lambda w: pl.BlockSpec((tm, w), lambda i, j: (i, 0))
    return pl.pallas_call(
        _odd_in_kernel,
        grid=(n_rows // tm, 4),
        in_specs=[pl.BlockSpec((tm, D), lambda i, j: (i, 0)),
                  _const_spec(1, D),
                  _mod_spec(0, group_fn), _mod_spec(1, group_fn),
                  pl.BlockSpec((D, HALF), lambda i, j: (0, jnp.minimum(j, 2))),
                  _const_spec(D, 2 * KV_W),
                  pl.BlockSpec((tm, 128), pos_blk), pl.BlockSpec((tm, 128), pos_blk)],
        out_specs=[row(HALF), row(HALF), row(KV_W), row(KV_W)],
        out_shape=[jax.ShapeDtypeStruct((n_rows, HALF), F32),
                   jax.ShapeDtypeStruct((n_rows, HALF), BF16),
                   jax.ShapeDtypeStruct((n_rows, KV_W), BF16),
                   jax.ShapeDtypeStruct((n_rows, KV_W), BF16)],
        scratch_shapes=[pltpu.VMEM((tm, D), BF16), pltpu.VMEM((tm, HALF), F32)],
        compiler_params=_cparams(("arbitrary", "arbitrary"), 40),
        name="odd_in",
    )(x, norm_g, mod_l, mod_l, w_main, w_kv, cos_t, sin_t)


def _conv_kernel(z_ref, zp_ref, zn_ref, w_ref, b_ref, lng_ref, lnb_ref, y_ref, ze_ref, c_ref, *, tm):
    first, last, _, _ = _seq_tile(pl.program_id(0), tm)
    ze_ref[0:CONV_HALO, :] = jnp.where(first, 0.0, zp_ref[...])
    ze_ref[CONV_HALO:CONV_HALO + tm, :] = z_ref[...]
    ze_ref[CONV_HALO + tm:, :] = jnp.where(last, 0.0, zn_ref[...])
    rc = 64
    base = CONV_HALO - CONV_W // 2

    def lane_block(cb, _):
        cs = pl.ds(pl.multiple_of(cb * 128, 128), 128)
        for r in range(tm // rc):
            acc = jnp.zeros((rc, 128), F32)
            for t in range(CONV_W):
                acc = acc + w_ref[t:t + 1, cs] * ze_ref[base + r * rc + t:base + r * rc + t + rc, cs]
            c_ref[r * rc:(r + 1) * rc, cs] = acc
        return 0

    lax.fori_loop(0, HALF // 128, lane_block, 0)
    y = _layer_norm(c_ref[...] + b_ref[...], lng_ref[...], lnb_ref[...])
    y_ref[...] = _silu(y).astype(BF16)


def _conv_module(zc, n_rows, conv_w, conv_b, ln_g, ln_b, tm):
    hb = tm // CONV_HALO
    n_hblk = zc.shape[0] // CONV_HALO
    return pl.pallas_call(
        functools.partial(_conv_kernel, tm=tm),
        grid=(n_rows // tm,),
        in_specs=[pl.BlockSpec((tm, HALF), lambda i: (i, 0)),
                  pl.BlockSpec((CONV_HALO, HALF), lambda i: (jnp.maximum(i * hb - 1, 0), 0)),
                  pl.BlockSpec((CONV_HALO, HALF), lambda i: (jnp.minimum((i + 1) * hb, n_hblk - 1), 0)),
                  _const_spec(CONV_W, HALF), _const_spec(1, HALF), _const_spec(1, HALF), _const_spec(1, HALF)],
        out_specs=pl.BlockSpec((tm, HALF), lambda i: (i, 0)),
        out_shape=jax.ShapeDtypeStruct((n_rows, HALF), BF16),
        scratch_shapes=[pltpu.VMEM((tm + 2 * CONV_HALO, HALF), F32), pltpu.VMEM((tm, HALF), F32)],
        compiler_params=_cparams(("arbitrary",), 32),
        name="conv_module",
    )(zc, zc, zc, conv_w, conv_b, ln_g, ln_b)


def _attn_kernel(sink_ref, q_ref, kp_ref, kc_ref, kn_ref, vp_ref, vc_ref, vn_ref, kx_ref, vx_ref, o_ref):
    i = pl.program_id(1)
    n_keys = 3 * ATT_BLK + CTX_LEN
    kb = jnp.concatenate([kp_ref[...], kc_ref[...], kn_ref[...], kx_ref[...]], axis=0)
    vb = jnp.concatenate([vp_ref[...], vc_ref[...], vn_ref[...], vx_ref[...]], axis=0)
    key_lane = lax.broadcasted_iota(I32, (n_keys, 2 * HEAD_DIM), 1)
    k_head = [jnp.where(key_lane < HEAD_DIM, kb, jnp.zeros_like(kb)),
              jnp.where(key_lane >= HEAD_DIM, kb, jnp.zeros_like(kb))]
    r = lax.broadcasted_iota(I32, (ATT_BLK, n_keys), 0)
    c = lax.broadcasted_iota(I32, (ATT_BLK, n_keys), 1)
    kpos = c + (i - 1) * ATT_BLK
    band_ok = lax.bitcast_convert_type(c - r, U32) <= U32(2 * WINDOW)
    in_seq = lax.bitcast_convert_type(kpos, U32) < U32(SEQ)
    bias = jnp.where(c >= 3 * ATT_BLK, 0.0, jnp.where(band_ok, jnp.where(in_seq, 0.0, -jnp.inf), -jnp.inf))
    out_lane = lax.broadcasted_iota(I32, (ATT_BLK, 2 * HEAD_DIM), 1)
    for g in range(Q_PER_KV):
        qg = q_ref[:, g * 128:(g + 1) * 128]
        outs = []
        for kvh in range(N_KV_HEADS):
            sk = sink_ref[kvh * Q_PER_KV + g]
            s = _dot_nt(qg, k_head[kvh]) + bias
            m = jnp.maximum(jnp.max(s, axis=-1, keepdims=True), sk)
            e = jnp.exp(s - m)
            den = jnp.sum(e, axis=-1, keepdims=True) + jnp.exp(sk - m)
            outs.append(_dot(e.astype(BF16), vb) / den)
        o_ref[:, g * 128:(g + 1) * 128] = jnp.where(out_lane < HEAD_DIM, outs[0], outs[1]).astype(BF16)


def _attention(q, k, v, sink):
    nb = SEQ // ATT_BLK
    ctx0 = N_LAT // CTX_LEN
    blk = lambda w, fn: pl.BlockSpec((ATT_BLK, w), fn)
    prev = lambda b, i, s: (b * nb + jnp.maximum(i - 1, 0), 0)
    cur = lambda b, i, s: (b * nb + i, 0)
    nxt = lambda b, i, s: (b * nb + jnp.minimum(i + 1, nb - 1), 0)
    ctx = pl.BlockSpec((CTX_LEN, KV_W), lambda b, i, s: (ctx0 + b, 0))
    grid_spec = pltpu.PrefetchScalarGridSpec(
        num_scalar_prefetch=1,
        grid=(BATCH, nb),
        in_specs=[blk(HALF, cur), blk(KV_W, prev), blk(KV_W, cur), blk(KV_W, nxt),
                  blk(KV_W, prev), blk(KV_W, cur), blk(KV_W, nxt), ctx, ctx],
        out_specs=blk(HALF, cur),
    )
    return pl.pallas_call(
        _attn_kernel,
        grid_spec=grid_spec,
        out_shape=jax.ShapeDtypeStruct((N_LAT, HALF), BF16),
        compiler_params=_cparams(("arbitrary", "arbitrary"), 32),
        name="attention",
    )(sink, q, k, k, k, v, v, v, k, v)


def _first_argmax(x, iota, n):
    m = jnp.max(x, axis=0, keepdims=True)
    first = jnp.min(jnp.where(x == m, iota, n), axis=0, keepdims=True)
    return m, first


def _route_kernel(lg_ref, bias_ref, idx_ref, gate_ref, rank_ref, cnt_ref, carry_ref):
    tm = lg_ref.shape[1]

    @pl.when(pl.program_id(0) == 0)
    def _():
        carry_ref[...] = jnp.zeros_like(carry_ref)

    scores = jax.nn.sigmoid(lg_ref[...])
    biased = scores + bias_ref[...]
    sub = lax.broadcasted_iota(I32, (PER_GROUP, tm), 0)
    blocks = [biased[g * PER_GROUP:(g + 1) * PER_GROUP, :] for g in range(N_EXPERT_GROUPS)]
    gs = []
    for blk in blocks:
        m1, f1 = _first_argmax(blk, sub, PER_GROUP)
        m2 = jnp.max(jnp.where(sub == f1, -jnp.inf, blk), axis=0, keepdims=True)
        gs.append(m1 + m2)
    gs = jnp.concatenate(gs, axis=0)
    giota = lax.broadcasted_iota(I32, (N_EXPERT_GROUPS, tm), 0)
    keep = jnp.zeros((N_EXPERT_GROUPS, tm), F32)
    for _ in range(TOPK_GROUPS):
        _, f = _first_argmax(gs, giota, N_EXPERT_GROUPS)
        hit = giota == f
        keep = jnp.where(hit, 1.0, keep)
        gs = jnp.where(hit, -jnp.inf, gs)
    cur = jnp.concatenate([jnp.where(keep[g:g + 1, :] > 0.0, blocks[g], -jnp.inf)
                           for g in range(N_EXPERT_GROUPS)], axis=0)
    eiota = lax.broadcasted_iota(I32, (N_EXPERTS, tm), 0)
    chosen = jnp.zeros((N_EXPERTS, tm), F32)
    idx, sel = [], []
    for _ in range(TOP_K):
        _, f = _first_argmax(cur, eiota, N_EXPERTS)
        hit = eiota == f
        idx.append(f)
        sel.append(jnp.sum(jnp.where(hit, scores, 0.0), axis=0, keepdims=True))
        cur = jnp.where(hit, -jnp.inf, cur)
        chosen = jnp.where(hit, 1.0, chosen)
    sel = jnp.concatenate(sel, axis=0)
    idx = jnp.concatenate(idx, axis=0)
    gate_ref[...] = sel / jnp.sum(sel, axis=0, keepdims=True) * ROUTED_SCALE
    idx_ref[...] = idx
    before = jnp.where(lax.broadcasted_iota(I32, (tm, tm), 0) < lax.broadcasted_iota(I32, (tm, tm), 1), 1.0, 0.0)
    rank = _dot(chosen.astype(BF16), before.astype(BF16)) + carry_ref[:, 0:1]
    rank_ref[...] = jnp.concatenate(
        [jnp.sum(jnp.where(eiota == idx[k:k + 1, :], rank, 0.0), axis=0, keepdims=True) for k in range(TOP_K)],
        axis=0).astype(I32)
    carry_ref[...] = carry_ref[...] + jnp.sum(chosen, axis=1, keepdims=True)
    cnt_ref[...] = carry_ref[...]


def _route(logits_t, router_bias, tm):
    n_tok = logits_t.shape[1]
    tok = lambda rows: pl.BlockSpec((rows, tm), lambda i: (0, i))
    return pl.pallas_call(
        _route_kernel,
        grid=(n_tok // tm,),
        in_specs=[tok(N_EXPERTS), _const_spec(N_EXPERTS, 1)],
        out_specs=[tok(TOP_K), tok(TOP_K), tok(TOP_K), _const_spec(N_EXPERTS, 128)],
        out_shape=[jax.ShapeDtypeStruct((TOP_K, n_tok), I32),
                   jax.ShapeDtypeStruct((TOP_K, n_tok), F32),
                   jax.ShapeDtypeStruct((TOP_K, n_tok), I32),
                   jax.ShapeDtypeStruct((N_EXPERTS, 128), F32)],
        scratch_shapes=[pltpu.VMEM((N_EXPERTS, 128), F32)],
        compiler_params=_cparams(("arbitrary",), 32),
        name="route",
    )(logits_t, router_bias.astype(F32).reshape(N_EXPERTS, 1))


def _moe_kernel(be_ref, na_ref, src_ref, srcn_ref, dstp_ref, h_hbm, wg_ref, wu_ref, wd_ref, y_hbm,
                xb, yb, wgb, wub, wdb, sem_in, sem_out):
    i = pl.program_id(0)
    n_steps = pl.num_programs(0)
    n_act = na_ref[0]
    cur = i % 2
    oth = 1 - cur
    bm = xb.shape[1]

    def gather(idx_ref, s):
        for r in range(bm):
            pltpu.make_async_copy(h_hbm.at[pl.ds(idx_ref[0, 0, r], 1), :], xb.at[s, pl.ds(r, 1), :],
                                  sem_in.at[s]).start()

    def scatter(idx_ref, s):
        for r in range(bm):
            pltpu.make_async_copy(yb.at[s, pl.ds(r, 1), :], y_hbm.at[pl.ds(idx_ref[0, 0, r], 1), :],
                                  sem_out.at[s]).start()

    def wait_gather(s):
        pltpu.make_async_copy(h_hbm.at[pl.ds(0, bm), :], xb.at[s], sem_in.at[s]).wait()

    def wait_scatter(s):
        pltpu.make_async_copy(yb.at[s], y_hbm.at[pl.ds(0, bm), :], sem_out.at[s]).wait()

    @pl.when(i == 0)
    def _():
        yb[...] = jnp.zeros_like(yb)
        gather(src_ref, 0)

    @pl.when(i <= n_act)
    def _():
        wait_gather(cur)

        @pl.when(i >= 1)
        def _():
            wait_scatter(cur)

        last_blk = n_steps - 2
        new_expert = jnp.logical_or(i == 0, be_ref[jnp.minimum(i, last_blk)] != be_ref[jnp.maximum(i - 1, 0)])

        @pl.when(new_expert)
        def _():
            wgb[...] = wg_ref[...].astype(BF16)
            wub[...] = wu_ref[...].astype(BF16)
            wdb[...] = wd_ref[...].astype(BF16)

        gather(srcn_ref, oth)
        x = _unpack_bf16(xb[cur])
        hid = (_silu(_dot(x, wgb[...])) * _dot(x, wub[...])).astype(BF16)
        yb[cur] = _pack_halves(_dot(hid, wdb[...]))
        scatter(dstp_ref, oth)

    @pl.when(i == n_steps - 1)
    def _():
        s = (n_act + 1) % 2
        wait_gather(s)
        wait_scatter(s)


def _moe_experts(layer, h2p, blk_exp, n_act, src, dstp, w_gate, w_up, w_down, n_out_rows):
    n_blocks = blk_exp.shape[0]
    bm = MOE_BM
    smem_blk = lambda fn: pl.BlockSpec((1, 1, bm), fn, memory_space=pltpu.SMEM)
    wspec = lambda a, b: pl.BlockSpec((None, None, a, b),
                                      lambda i, be, na: (layer, be[jnp.minimum(i, n_blocks - 1)], 0, 0))
    grid_spec = pltpu.PrefetchScalarGridSpec(
        num_scalar_prefetch=2,
        grid=(n_blocks + 1,),
        in_specs=[smem_blk(lambda i, be, na: (jnp.minimum(i, n_blocks - 1), 0, 0)),
                  smem_blk(lambda i, be, na: (jnp.minimum(i + 1, n_blocks - 1), 0, 0)),
                  smem_blk(lambda i, be, na: (i, 0, 0)),
                  pl.BlockSpec(memory_space=pl.ANY),
                  wspec(D, EXPERT_FF), wspec(D, EXPERT_FF), wspec(EXPERT_FF, D)],
        out_specs=pl.BlockSpec(memory_space=pl.ANY),
        scratch_shapes=[pltpu.VMEM((2, bm, HALF), U32), pltpu.VMEM((2, bm, HALF), U32),
                        pltpu.VMEM((D, EXPERT_FF), BF16), pltpu.VMEM((D, EXPERT_FF), BF16),
                        pltpu.VMEM((EXPERT_FF, D), BF16),
                        pltpu.SemaphoreType.DMA((2,)), pltpu.SemaphoreType.DMA((2,))],
    )
    src3 = src.reshape(n_blocks, 1, bm)
    return pl.pallas_call(
        _moe_kernel,
        grid_spec=grid_spec,
        out_shape=jax.ShapeDtypeStruct((n_out_rows, HALF), U32),
        compiler_params=_cparams(("arbitrary",), 48),
        name="moe_experts",
    )(blk_exp, n_act, src3, src3, dstp.reshape(n_blocks + 1, 1, bm), h2p, w_gate, w_up, w_down)


def _shared_kernel(h_ref, wg_ref, wu_ref, wd_ref, o_ref):
    x = _unpack_bf16(h_ref[...])
    hid = (_silu(_dot(x, wg_ref[...])) * _dot(x, wu_ref[...])).astype(BF16)
    o_ref[...] = _dot(hid, wd_ref[...])


def _shared_expert(h2p, wg, wu, wd, tm):
    n_rows = h2p.shape[0]
    return pl.pallas_call(
        _shared_kernel,
        grid=(n_rows // tm,),
        in_specs=[pl.BlockSpec((tm, HALF), lambda i: (i, 0)),
                  _const_spec(D, EXPERT_FF), _const_spec(D, EXPERT_FF), _const_spec(EXPERT_FF, D)],
        out_specs=pl.BlockSpec((tm, D), lambda i: (i, 0)),
        out_shape=jax.ShapeDtypeStruct((n_rows, D), F32),
        compiler_params=_cparams(("arbitrary",), 40),
        name="shared_expert",
    )(h2p, wg, wu, wd)


def _combine_kernel(*refs, final):
    x_ref, sh_ref, gt_ref, g2_ref = refs[:4]
    y_refs = refs[4:4 + TOP_K]
    fg_ref = refs[4 + TOP_K] if final else None
    o_ref = refs[-1]
    gt = gt_ref[...]
    sh = sh_ref[...]
    acc_lo = sh[:, :HALF]
    acc_hi = sh[:, HALF:]
    for k in range(TOP_K):
        lo, hi = _unpack_halves(y_refs[k][...])
        acc_lo = acc_lo + lo * gt[:, k:k + 1]
        acc_hi = acc_hi + hi * gt[:, k:k + 1]
    xn = x_ref[...] + g2_ref[...] * jnp.concatenate([acc_lo, acc_hi], axis=1)
    if final:
        xn = xn * lax.rsqrt(jnp.mean(xn * xn, axis=-1, keepdims=True) + EPS) * fg_ref[...]
    o_ref[...] = xn


def _combine(x, shared, gates, mod_l, y_tk, n_tok, final_g, tm):
    nt = n_tok // tm
    group_fn = _group_fn(tm)
    row = pl.BlockSpec((tm, D), lambda i: (i, 0))
    in_specs = [row, row, pl.BlockSpec((tm, TOP_K), lambda i: (i, 0)), _mod_spec(5, group_fn)]
    in_specs += [pl.BlockSpec((tm, HALF), functools.partial(lambda i, k: (k * nt + i, 0), k=k)) for k in range(TOP_K)]
    args = [x, shared, gates, mod_l] + [y_tk] * TOP_K
    final = final_g is not None
    if final:
        in_specs.append(_const_spec(1, D))
        args.append(final_g)
    return pl.pallas_call(
        functools.partial(_combine_kernel, final=final),
        grid=(nt,),
        in_specs=in_specs,
        out_specs=row,
        out_shape=jax.ShapeDtypeStruct((n_tok, D), F32),
        compiler_params=_cparams(("arbitrary",), 40),
        name="moe_combine",
    )(*args)


def _dispatch_plan(idx_t, rank_t, cnt, n_tok):
    bm = MOE_BM
    n_assign = n_tok * TOP_K
    n_blocks = n_assign // bm + N_EXPERTS
    n_rows = n_blocks * bm
    sizes = cnt[:, 0].astype(I32)
    padded = (sizes + bm - 1) // bm * bm
    pad_end = jnp.cumsum(padded)
    pad_start = pad_end - padded
    blk_exp = jnp.minimum(jnp.searchsorted(pad_end, jnp.arange(n_blocks, dtype=I32) * bm, side='right'),
                          N_EXPERTS - 1).astype(I32)
    n_act = (pad_end[-1] // bm).astype(I32).reshape(1)
    slot = (pad_start[idx_t] + rank_t).reshape(-1)
    inv = jnp.full((n_rows,), -1, I32).at[slot].set(jnp.arange(n_assign, dtype=I32), unique_indices=True)
    valid = inv >= 0
    pad_rank = jnp.cumsum(jnp.logical_not(valid).astype(I32)) - 1
    src = jnp.where(valid, inv % n_tok, 0).astype(I32)
    dst = jnp.where(valid, inv, n_assign + pad_rank).astype(I32)
    dstp = jnp.concatenate([n_rows + jnp.arange(bm, dtype=I32), dst])
    return blk_exp, n_act, src, dstp, n_rows + bm


def _moe_layer(layer, x, h2p, logits_t, router_bias, mod_l, exp_w, shared_w, final_g):
    n_tok = h2p.shape[0]
    idx_t, gates_t, rank_t, cnt = _route(logits_t, router_bias, 512)
    blk_exp, n_act, src, dstp, n_out = _dispatch_plan(idx_t, rank_t, cnt, n_tok)
    y_tk = _moe_experts(layer, h2p, blk_exp, n_act, src, dstp, *exp_w, n_out)
    shared = _shared_expert(h2p, *[w[layer].astype(BF16) for w in shared_w], 512)
    return _combine(x, shared, gates_t.T, mod_l, y_tk, n_tok, final_g, 128)


def _rope_tables():
    rows = SEQ // GRID_W
    row = jnp.broadcast_to(jnp.arange(rows)[:, None], (rows, GRID_W)).reshape(-1).astype(F32)
    col = jnp.broadcast_to(jnp.arange(GRID_W)[None, :], (rows, GRID_W)).reshape(-1).astype(F32)
    inv = ROPE_BASE ** (-jnp.arange(0, AXIS_DIM, 2, dtype=F32) / AXIS_DIM)
    ang_r = row[:, None] * inv
    ang_c = col[:, None] * inv
    ang = jnp.concatenate([ang_r, ang_r, ang_c, ang_c], axis=-1)
    cos, sin = jnp.cos(ang), jnp.sin(ang)
    sign = jnp.where((jnp.arange(HEAD_DIM) % AXIS_DIM) < AXIS_DIM // 2, -1.0, 1.0).astype(F32)
    ident = 512
    cos_t = jnp.concatenate([jnp.tile(cos, (1, 2)), jnp.ones((ident, 128), F32)], axis=0)
    sin_t = jnp.concatenate([jnp.tile(sin * sign, (1, 2)), jnp.zeros((ident, 128), F32)], axis=0)
    return cos_t, sin_t


def _head_perm():
    g = jnp.arange(Q_PER_KV)[:, None, None]
    kvh = jnp.arange(N_KV_HEADS)[None, :, None]
    d = jnp.arange(HEAD_DIM)[None, None, :]
    return ((kvh * Q_PER_KV + g) * HEAD_DIM + d).reshape(-1)


def kernel(x, c, ctx, c_ctx, ada_w, ada_b, norm1_g, norm2_g, even_w_in, gmlp_ln_g, gmlp_ln_b, gmlp_ws, gmlp_bs, pool_w, pool_scale, even_w_out, odd_w_in, conv_w, conv_b, conv_ln_g, conv_ln_b, attn_sink, odd_w_out, router_w, router_bias, exp_w_gate, exp_w_up, exp_w_down, shared_w_gate, shared_w_up, shared_w_down, final_g):
    mod = _ada_mod(c, c_ctx, ada_w, ada_b)
    row = lambda a: a.reshape(1, -1)
    exp_w = (exp_w_gate, exp_w_up, exp_w_down)
    shared_w = (shared_w_gate, shared_w_up, shared_w_down)

    def router_split(i):
        w = router_w[i].T
        hi = w.astype(BF16)
        return hi, (w - hi.astype(F32)).astype(BF16)

    x0 = jnp.concatenate([x.reshape(N_LAT, D), ctx.reshape(N_CTX, D)], axis=0)

    uv, z = _even_in(x0, mod[0], row(norm1_g[0]), even_w_in[0].astype(BF16), 512)
    y = _even_mix(uv, z, row(gmlp_ln_g[0]), row(gmlp_ln_b[0]), gmlp_ws[0].astype(BF16),
                  gmlp_bs[0].reshape(A_GROUPS, CHUNK, 1), pool_w[0].astype(BF16), row(pool_scale[0]), 256)
    x1, h2p, lg = _mix_out(y, 0, y, 1, even_w_out[0].astype(BF16).reshape(2, HALF, D), x0, N_ALL, mod[0],
                           row(norm2_g[0]), *router_split(0), 256)
    x1 = _moe_layer(0, x1, h2p, lg, router_bias[0], mod[0], exp_w, shared_w, None)

    perm = _head_perm()
    w_in1 = odd_w_in[0]
    w_main = jnp.concatenate([w_in1[:, :2 * HALF], w_in1[:, 2 * HALF:3 * HALF][:, perm]], axis=1).astype(BF16)
    w_kv = w_in1[:, 3 * HALF:].astype(BF16)
    w_out1 = odd_w_out[0]
    w_out1 = jnp.stack([w_out1[:HALF], w_out1[HALF:][perm]], axis=0).astype(BF16)
    cos_t, sin_t = _rope_tables()
    zc, q, k, v = _odd_in(x1, mod[1], row(norm1_g[1]), w_main, w_kv, cos_t, sin_t, 512)
    y_conv = _conv_module(zc, N_LAT, conv_w[0], row(conv_b[0]), row(conv_ln_g[0]), row(conv_ln_b[0]), 256)
    y_attn = _attention(q, k, v, attn_sink[0].astype(F32))
    x2, h2p, lg = _mix_out(y_conv, 0, y_attn, 0, w_out1, x1, N_LAT, mod[1], row(norm2_g[1]),
                           *router_split(1), 256)
    out = _moe_layer(1, x2, h2p, lg, router_bias[1], mod[1], exp_w, shared_w, row(final_g))
    return out.reshape(BATCH, SEQ, D)
```

```python
import functools

import jax
import jax.numpy as jnp
from jax import lax
from jax.experimental import pallas as pl
from jax.experimental.pallas import tpu as pltpu

F32 = jnp.float32
BF16 = jnp.bfloat16
U32 = jnp.uint32
I32 = jnp.int32

D = 2048
BATCH = 4
SEQ = 4096
DEPTH = 2
GRID_W = 64
CTX_LEN = 256
HALF = D // 2
CHUNK = 128
A_GROUPS = 4
A_GW = HALF // A_GROUPS
POOL_WINDOWS = (2, 4, 8, 16)
B_GW = HALF // len(POOL_WINDOWS)
CONV_W = 31
HEAD_DIM = 64
N_Q_HEADS = HALF // HEAD_DIM
N_KV_HEADS = 2
Q_PER_KV = N_Q_HEADS // N_KV_HEADS
KV_W = N_KV_HEADS * HEAD_DIM
ATT_BLK = 128
WINDOW = 128
AXIS_DIM = HEAD_DIM // 2
ROPE_BASE = 10000.0
N_EXPERTS = 64
N_EXPERT_GROUPS = 8
PER_GROUP = N_EXPERTS // N_EXPERT_GROUPS
TOPK_GROUPS = 4
TOP_K = 8
EXPERT_FF = 512
ROUTED_SCALE = 2.5
EPS = 1e-6

N_LAT = BATCH * SEQ
N_CTX = BATCH * CTX_LEN
N_ALL = N_LAT + N_CTX
CTX_GROUP = BATCH
HALO = 128
CONV_HALO = 16
MOE_BM = 256

MIB = 1024 * 1024


def _cparams(sem, vmem_mib):
    return pltpu.CompilerParams(dimension_semantics=sem, vmem_limit_bytes=vmem_mib * MIB)


def _dot(a, b):
    return jnp.dot(a, b, preferred_element_type=F32)


def _dot_nt(a, b):
    return lax.dot_general(a, b, (((1,), (1,)), ((), ())), preferred_element_type=F32)


def _rms_mod(x, g, sh, sc):
    y = x * lax.rsqrt(jnp.mean(x * x, axis=-1, keepdims=True) + EPS) * g
    return y * (1.0 + sc) + sh


def _layer_norm(x, g, b):
    mu = jnp.mean(x, axis=-1, keepdims=True)
    xc = x - mu
    var = jnp.mean(xc * xc, axis=-1, keepdims=True)
    return xc * lax.rsqrt(var + EPS) * g + b


def _gelu(x):
    return 0.5 * x * (1.0 + lax.erf(x * (2.0 ** -0.5)))


def _silu(x):
    return x * jax.nn.sigmoid(x)


def _split_bf16(x):
    hi = x.astype(BF16)
    lo = (x - hi.astype(F32)).astype(BF16)
    return hi, lo


def _pack_halves(y):
    n = y.shape[1] // 2
    lo = lax.bitcast_convert_type(y[:, :n].astype(BF16).astype(F32), U32) >> 16
    hi = lax.bitcast_convert_type(y[:, n:].astype(BF16).astype(F32), U32) & U32(0xFFFF0000)
    return hi | lo


def _unpack_halves(p):
    lo = lax.bitcast_convert_type(p << 16, F32)
    hi = lax.bitcast_convert_type(p & U32(0xFFFF0000), F32)
    return lo, hi


def _unpack_bf16(p):
    lo, hi = _unpack_halves(p)
    return jnp.concatenate([lo.astype(BF16), hi.astype(BF16)], axis=1)


LANE_BLKS = HALF // 128


def _store_tile_rows(ref, packed):
    tm = packed.shape[0]
    for s in range(LANE_BLKS):
        ref[pl.ds(s, tm, stride=LANE_BLKS), :] = packed[:, s * 128:(s + 1) * 128]


def _load_tile_rows(ref, tm):
    return jnp.concatenate([ref[pl.ds(s, tm, stride=LANE_BLKS), :] for s in range(LANE_BLKS)], axis=1)


def _mod_spec(chunk, group_fn):
    return pl.BlockSpec((None, None, 1, D), lambda i, *_: (group_fn(i), chunk, 0, 0))


def _group_fn(tm):
    return lambda i: jnp.minimum(i // (SEQ // tm), CTX_GROUP)


def _const_spec(*shape):
    return pl.BlockSpec(shape, lambda *_: (0,) * len(shape))


def _ada_kernel(c_ref, w_ref, b_ref, o_ref):
    s = _silu(c_ref[...]).astype(BF16)
    o_ref[...] = _dot(s, w_ref[...].astype(BF16)) + b_ref[...]


def _ada_mod(c, c_ctx, ada_w, ada_b):
    tn = 1024
    cv = jnp.zeros((8, D), F32).at[:BATCH].set(c).at[CTX_GROUP].set(c_ctx)
    out = pl.pallas_call(
        _ada_kernel,
        grid=(DEPTH, 6 * D // tn),
        in_specs=[pl.BlockSpec((8, D), lambda l, j: (0, 0)),
                  pl.BlockSpec((None, D, tn), lambda l, j: (l, 0, j)),
                  pl.BlockSpec((None, 1, tn), lambda l, j: (l, 0, j))],
        out_specs=pl.BlockSpec((None, 8, tn), lambda l, j: (l, 0, j)),
        out_shape=jax.ShapeDtypeStruct((DEPTH, 8, 6 * D), F32),
        compiler_params=_cparams(("arbitrary", "arbitrary"), 40),
        name="ada_mod",
    )(cv, ada_w, ada_b.reshape(DEPTH, 1, 6 * D))
    return out.reshape(DEPTH, 8, 6, 1, D)


def _even_in_kernel(x_ref, g_ref, sh_ref, sc_ref, w_ref, uv_ref, z_ref, h_ref):
    j = pl.program_id(1)

    @pl.when(j == 0)
    def _():
        h_ref[...] = _rms_mod(x_ref[...], g_ref[...], sh_ref[...], sc_ref[...]).astype(BF16)

    acc = _dot(h_ref[...], w_ref[...])

    @pl.when(j < 2)
    def _():
        uv_ref[...] = _gelu(acc).astype(BF16)

    @pl.when(j == 2)
    def _():
        z_ref[...] = acc


def _even_in(x, mod_l, norm_g, w_in, tm):
    n_rows = x.shape[0]
    group_fn = _group_fn(tm)
    return pl.pallas_call(
        _even_in_kernel,
        grid=(n_rows // tm, 3),
        in_specs=[pl.BlockSpec((tm, D), lambda i, j: (i, 0)),
                  _const_spec(1, D),
                  _mod_spec(0, group_fn), _mod_spec(1, group_fn),
                  pl.BlockSpec((D, HALF), lambda i, j: (0, j))],
        out_specs=[pl.BlockSpec((tm, HALF), lambda i, j: (i, jnp.minimum(j, 1))),
                   pl.BlockSpec((tm, HALF), lambda i, j: (i, 0))],
        out_shape=[jax.ShapeDtypeStruct((n_rows, 2 * HALF), BF16),
                   jax.ShapeDtypeStruct((n_rows, HALF), F32)],
        scratch_shapes=[pltpu.VMEM((tm, D), BF16)],
        compiler_params=_cparams(("arbitrary", "arbitrary"), 40),
        name="even_in",
    )(x, norm_g, mod_l, mod_l, w_in)


def _band(d, w):
    inside = lax.bitcast_convert_type(d + w // 2, U32) < U32(w)
    return jnp.where(inside, 1.0, 0.0).astype(BF16)


def _seq_tile(i, tm):
    lat_tiles = N_LAT // tm
    is_lat = i < lat_tiles
    it = i % (SEQ // tm)
    first = jnp.logical_or(jnp.logical_not(is_lat), it == 0)
    last = jnp.logical_or(jnp.logical_not(is_lat), it == SEQ // tm - 1)
    pos0 = jnp.where(is_lat, it * tm, 0)
    seq_len = jnp.where(is_lat, SEQ, CTX_LEN)
    return first, last, pos0, seq_len


def _even_mix_kernel(u_ref, v_ref, z_ref, zp_ref, zn_ref, lng_ref, lnb_ref, ws_ref, bs_ref,
                     wp_ref, ps_ref, y_ref, *, tm):
    first, last, pos0, seq_len = _seq_tile(pl.program_id(0), tm)
    vn = _layer_norm(v_ref[...].astype(F32), lng_ref[...], lnb_ref[...]).astype(BF16)
    for g in range(A_GROUPS):
        cs = slice(g * A_GW, (g + 1) * A_GW)
        for c in range(tm // CHUNK):
            rs = slice(c * CHUNK, (c + 1) * CHUNK)
            mixed = _dot(ws_ref[g], vn[rs, cs]) + bs_ref[g]
            y_ref[rs, cs] = (u_ref[rs, cs].astype(F32) * mixed).astype(BF16)
    z = z_ref[...]
    zp = jnp.where(first, 0.0, zp_ref[...])
    zn = jnp.where(last, 0.0, zn_ref[...])
    z_hi, z_lo = _split_bf16(z)
    zp_hi, zp_lo = _split_bf16(zp)
    zn_hi, zn_lo = _split_bf16(zn)
    d_main = (lax.broadcasted_iota(I32, (tm, tm), 1) - lax.broadcasted_iota(I32, (tm, tm), 0))
    d_halo = (lax.broadcasted_iota(I32, (tm, HALO), 1) - lax.broadcasted_iota(I32, (tm, HALO), 0))
    pos = pos0 + lax.broadcasted_iota(I32, (tm, 1), 0)
    for g, w in enumerate(POOL_WINDOWS):
        cs = slice(g * B_GW, (g + 1) * B_GW)
        bm_ = _band(d_main, w)
        bp = _band(d_halo - HALO, w)
        bn = _band(d_halo + tm, w)
        tot = (_dot(bm_, z_hi[:, cs]) + _dot(bm_, z_lo[:, cs])
               + _dot(bp, zp_hi[:, cs]) + _dot(bp, zp_lo[:, cs])
               + _dot(bn, zn_hi[:, cs]) + _dot(bn, zn_lo[:, cs]))
        cnt = (jnp.minimum(pos + w // 2, seq_len) - jnp.maximum(pos - w // 2, 0)).astype(F32)
        pooled = (tot / cnt - z[:, cs]).astype(BF16)
        y_ref[:, HALF + g * B_GW:HALF + (g + 1) * B_GW] = (
            _dot(pooled, wp_ref[g]) * ps_ref[:, cs]).astype(BF16)


def _even_mix(uv, z, ln_g, ln_b, ws, bs, wp, ps, tm):
    n_rows = z.shape[0]
    hb = tm // HALO
    n_hblk = n_rows // HALO
    return pl.pallas_call(
        functools.partial(_even_mix_kernel, tm=tm),
        grid=(n_rows // tm,),
        in_specs=[pl.BlockSpec((tm, HALF), lambda i: (i, 0)),
                  pl.BlockSpec((tm, HALF), lambda i: (i, 1)),
                  pl.BlockSpec((tm, HALF), lambda i: (i, 0)),
                  pl.BlockSpec((HALO, HALF), lambda i: (jnp.maximum(i * hb - 1, 0), 0)),
                  pl.BlockSpec((HALO, HALF), lambda i: (jnp.minimum((i + 1) * hb, n_hblk - 1), 0)),
                  _const_spec(1, HALF), _const_spec(1, HALF),
                  _const_spec(A_GROUPS, CHUNK, CHUNK), _const_spec(A_GROUPS, CHUNK, 1),
                  _const_spec(len(POOL_WINDOWS), B_GW, B_GW), _const_spec(1, HALF)],
        out_specs=pl.BlockSpec((tm, D), lambda i: (i, 0)),
        out_shape=jax.ShapeDtypeStruct((n_rows, D), BF16),
        compiler_params=_cparams(("arbitrary",), 40),
        name="even_mix",
    )(uv, uv, z, z, z, ln_g, ln_b, ws, bs, wp, ps)


def _mix_out_kernel(ya_ref, yb_ref, w_ref, x_ref, g1_ref, n2_ref, sh_ref, sc_ref, rwh_ref, rwl_ref,
                    xo_ref, hp_ref, lg_ref):
    o = _dot(ya_ref[...], w_ref[0]) + _dot(yb_ref[...], w_ref[1])
    xn = x_ref[...] + g1_ref[...] * o
    xo_ref[...] = xn
    h = _rms_mod(xn, n2_ref[...], sh_ref[...], sc_ref[...])
    _store_tile_rows(hp_ref, _pack_halves(h))
    h_hi, h_lo = _split_bf16(h)
    lg_ref[...] = _dot_nt(rwh_ref[...], h_hi) + _dot_nt(rwl_ref[...], h_hi) + _dot_nt(rwh_ref[...], h_lo)


def _mix_out(ya, ya_col, yb, yb_col, w_out, x, n_rows, mod_l, norm2_g, rw_hi, rw_lo, tm):
    group_fn = _group_fn(tm)
    return pl.pallas_call(
        _mix_out_kernel,
        grid=(n_rows // tm,),
        in_specs=[pl.BlockSpec((tm, HALF), lambda i: (i, ya_col)),
                  pl.BlockSpec((tm, HALF), lambda i: (i, yb_col)),
                  _const_spec(2, HALF, D),
                  pl.BlockSpec((tm, D), lambda i: (i, 0)),
                  _mod_spec(2, group_fn), _const_spec(1, D), _mod_spec(3, group_fn), _mod_spec(4, group_fn),
                  _const_spec(N_EXPERTS, D), _const_spec(N_EXPERTS, D)],
        out_specs=[pl.BlockSpec((tm, D), lambda i: (i, 0)),
                   pl.BlockSpec((tm * LANE_BLKS, 128), lambda i: (i, 0)),
                   pl.BlockSpec((N_EXPERTS, tm), lambda i: (0, i))],
        out_shape=[jax.ShapeDtypeStruct((n_rows, D), F32),
                   jax.ShapeDtypeStruct((n_rows * LANE_BLKS, 128), U32),
                   jax.ShapeDtypeStruct((N_EXPERTS, n_rows), F32)],
        compiler_params=_cparams(("arbitrary",), 48),
        name="mix_out",
    )(ya, yb, w_out, x, mod_l, norm2_g, mod_l, mod_l, rw_hi, rw_lo)


def _rope(x, cos, sin_signed, first_half):
    partner = jnp.where(first_half, pltpu.roll(x, 128 - AXIS_DIM // 2, 1), pltpu.roll(x, AXIS_DIM // 2, 1))
    return x * cos + partner * sin_signed


def _odd_in_kernel(x_ref, g_ref, sh_ref, sc_ref, w_ref, wkv_ref, cos_ref, sin_ref,
                   zc_ref, q_ref, k_ref, v_ref, h_ref, a_ref):
    j = pl.program_id(1)
    tm = x_ref.shape[0]

    @pl.when(j == 0)
    def _():
        h_ref[...] = _rms_mod(x_ref[...], g_ref[...], sh_ref[...], sc_ref[...]).astype(BF16)

    first_half = (lax.broadcasted_iota(I32, (tm, 128), 1) % AXIS_DIM) < (AXIS_DIM // 2)

    @pl.when(j == 0)
    def _():
        a_ref[...] = _dot(h_ref[...], w_ref[...])

    @pl.when(j == 1)
    def _():
        zc_ref[...] = a_ref[...] * jax.nn.sigmoid(_dot(h_ref[...], w_ref[...]))

    @pl.when(j == 2)
    def _():
        q = _dot(h_ref[...], w_ref[...])
        cos = cos_ref[...]
        sin = sin_ref[...]
        for b in range(HALF // 128):
            cs = slice(b * 128, (b + 1) * 128)
            q_ref[:, cs] = (_rope(q[:, cs], cos, sin, first_half) * (HEAD_DIM ** -0.5)).astype(BF16)

    @pl.when(j == 3)
    def _():
        kv = _dot(h_ref[...], wkv_ref[...])
        k_ref[...] = _rope(kv[:, :KV_W], cos_ref[...], sin_ref[...], first_half).astype(BF16)
        v_ref[...] = kv[:, KV_W:].astype(BF16)


def _odd_in(x, mod_l, norm_g, w_main, w_kv, cos_t, sin_t, tm):
    n_rows = x.shape[0]
    group_fn = _group_fn(tm)
    lat_tiles = N_LAT // tm
    pos_blk = lambda i, j: (jnp.where(i < lat_tiles, i % (SEQ // tm), SEQ // tm), 0)
    row = lambda w: pl.BlockSpec((tm, w), lambda i, j: (i, 0))
    return pl.pallas_call(
        _odd_in_kernel,
        grid=(n_rows // tm, 4),
        in_specs=[pl.BlockSpec((tm, D), lambda i, j: (i, 0)),
                  _const_spec(1, D),
                  _mod_spec(0, group_fn), _mod_spec(1, group_fn),
                  pl.BlockSpec((D, HALF), lambda i, j: (0, jnp.minimum(j, 2))),
                  _const_spec(D, 2 * KV_W),
                  pl.BlockSpec((tm, 128), pos_blk), pl.BlockSpec((tm, 128), pos_blk)],
        out_specs=[row(HALF), row(HALF), row(KV_W), row(KV_W)],
        out_shape=[jax.ShapeDtypeStruct((n_rows, HALF), F32),
                   jax.ShapeDtypeStruct((n_rows, HALF), BF16),
                   jax.ShapeDtypeStruct((n_rows, KV_W), BF16),
                   jax.ShapeDtypeStruct((n_rows, KV_W), BF16)],
        scratch_shapes=[pltpu.VMEM((tm, D), BF16), pltpu.VMEM((tm, HALF), F32)],
        compiler_params=_cparams(("arbitrary", "arbitrary"), 40),
        name="odd_in",
    )(x, norm_g, mod_l, mod_l, w_main, w_kv, cos_t, sin_t)


def _conv_kernel(z_ref, zp_ref, zn_ref, w_ref, b_ref, lng_ref, lnb_ref, y_ref, ze_ref, c_ref, *, tm):
    first, last, _, _ = _seq_tile(pl.program_id(0), tm)
    ze_ref[0:CONV_HALO, :] = jnp.where(first, 0.0, zp_ref[...])
    ze_ref[CONV_HALO:CONV_HALO + tm, :] = z_ref[...]
    ze_ref[CONV_HALO + tm:, :] = jnp.where(last, 0.0, zn_ref[...])
    rc = 64
    base = CONV_HALO - CONV_W // 2

    def lane_block(cb, _):
        cs = pl.ds(pl.multiple_of(cb * 128, 128), 128)
        for r in range(tm // rc):
            acc = jnp.zeros((rc, 128), F32)
            for t in range(CONV_W):
                acc = acc + w_ref[t:t + 1, cs] * ze_ref[base + r * rc + t:base + r * rc + t + rc, cs]
            c_ref[r * rc:(r + 1) * rc, cs] = acc
        return 0

    lax.fori_loop(0, HALF // 128, lane_block, 0)
    y = _layer_norm(c_ref[...] + b_ref[...], lng_ref[...], lnb_ref[...])
    y_ref[...] = _silu(y).astype(BF16)


def _conv_module(zc, n_rows, conv_w, conv_b, ln_g, ln_b, tm):
    hb = tm // CONV_HALO
    n_hblk = zc.shape[0] // CONV_HALO
    return pl.pallas_call(
        functools.partial(_conv_kernel, tm=tm),
        grid=(n_rows // tm,),
        in_specs=[pl.BlockSpec((tm, HALF), lambda i: (i, 0)),
                  pl.BlockSpec((CONV_HALO, HALF), lambda i: (jnp.maximum(i * hb - 1, 0), 0)),
                  pl.BlockSpec((CONV_HALO, HALF), lambda i: (jnp.minimum((i + 1) * hb, n_hblk - 1), 0)),
                  _const_spec(CONV_W, HALF), _const_spec(1, HALF), _const_spec(1, HALF), _const_spec(1, HALF)],
        out_specs=pl.BlockSpec((tm, HALF), lambda i: (i, 0)),
        out_shape=jax.ShapeDtypeStruct((n_rows, HALF), BF16),
        scratch_shapes=[pltpu.VMEM((tm + 2 * CONV_HALO, HALF), F32), pltpu.VMEM((tm, HALF), F32)],
        compiler_params=_cparams(("arbitrary",), 32),
        name="conv_module",
    )(zc, zc, zc, conv_w, conv_b, ln_g, ln_b)


def _attn_kernel(sink_ref, q_ref, kp_ref, kc_ref, kn_ref, vp_ref, vc_ref, vn_ref, kx_ref, vx_ref, o_ref):
    i = pl.program_id(1)
    n_keys = 3 * ATT_BLK + CTX_LEN
    kb = jnp.concatenate([kp_ref[...], kc_ref[...], kn_ref[...], kx_ref[...]], axis=0)
    vb = jnp.concatenate([vp_ref[...], vc_ref[...], vn_ref[...], vx_ref[...]], axis=0)
    key_lane = lax.broadcasted_iota(I32, (n_keys, 2 * HEAD_DIM), 1)
    k_head = [jnp.where(key_lane < HEAD_DIM, kb, jnp.zeros_like(kb)),
              jnp.where(key_lane >= HEAD_DIM, kb, jnp.zeros_like(kb))]
    r = lax.broadcasted_iota(I32, (ATT_BLK, n_keys), 0)
    c = lax.broadcasted_iota(I32, (ATT_BLK, n_keys), 1)
    kpos = c + (i - 1) * ATT_BLK
    band_ok = lax.bitcast_convert_type(c - r, U32) <= U32(2 * WINDOW)
    in_seq = lax.bitcast_convert_type(kpos, U32) < U32(SEQ)
    bias = jnp.where(c >= 3 * ATT_BLK, 0.0, jnp.where(band_ok, jnp.where(in_seq, 0.0, -jnp.inf), -jnp.inf))
    out_lane = lax.broadcasted_iota(I32, (ATT_BLK, 2 * HEAD_DIM), 1)
    for g in range(Q_PER_KV):
        qg = q_ref[:, g * 128:(g + 1) * 128]
        outs = []
        for kvh in range(N_KV_HEADS):
            sk = sink_ref[kvh * Q_PER_KV + g]
            s = _dot_nt(qg, k_head[kvh]) + bias
            m = jnp.maximum(jnp.max(s, axis=-1, keepdims=True), sk)
            e = jnp.exp(s - m)
            den = jnp.sum(e, axis=-1, keepdims=True) + jnp.exp(sk - m)
            outs.append(_dot(e.astype(BF16), vb) / den)
        o_ref[:, g * 128:(g + 1) * 128] = jnp.where(out_lane < HEAD_DIM, outs[0], outs[1]).astype(BF16)


def _attention(q, k, v, sink):
    nb = SEQ // ATT_BLK
    ctx0 = N_LAT // CTX_LEN
    blk = lambda w, fn: pl.BlockSpec((ATT_BLK, w), fn)
    prev = lambda b, i, s: (b * nb + jnp.maximum(i - 1, 0), 0)
    cur = lambda b, i, s: (b * nb + i, 0)
    nxt = lambda b, i, s: (b * nb + jnp.minimum(i + 1, nb - 1), 0)
    ctx = pl.BlockSpec((CTX_LEN, KV_W), lambda b, i, s: (ctx0 + b, 0))
    grid_spec = pltpu.PrefetchScalarGridSpec(
        num_scalar_prefetch=1,
        grid=(BATCH, nb),
        in_specs=[blk(HALF, cur), blk(KV_W, prev), blk(KV_W, cur), blk(KV_W, nxt),
                  blk(KV_W, prev), blk(KV_W, cur), blk(KV_W, nxt), ctx, ctx],
        out_specs=blk(HALF, cur),
    )
    return pl.pallas_call(
        _attn_kernel,
        grid_spec=grid_spec,
        out_shape=jax.ShapeDtypeStruct((N_LAT, HALF), BF16),
        compiler_params=_cparams(("arbitrary", "arbitrary"), 32),
        name="attention",
    )(sink, q, k, k, k, v, v, v, k, v)


def _first_argmax(x, iota, n):
    m = jnp.max(x, axis=0, keepdims=True)
    first = jnp.min(jnp.where(x == m, iota, n), axis=0, keepdims=True)
    return m, first


def _route_kernel(lg_ref, bias_ref, idx_ref, gate_ref, rank_ref, cnt_ref, carry_ref):
    tm = lg_ref.shape[1]

    @pl.when(pl.program_id(0) == 0)
    def _():
        carry_ref[...] = jnp.zeros_like(carry_ref)

    scores = jax.nn.sigmoid(lg_ref[...])
    biased = scores + bias_ref[...]
    sub = lax.broadcasted_iota(I32, (PER_GROUP, tm), 0)
    blocks = [biased[g * PER_GROUP:(g + 1) * PER_GROUP, :] for g in range(N_EXPERT_GROUPS)]
    gs = []
    for blk in blocks:
        m1, f1 = _first_argmax(blk, sub, PER_GROUP)
        m2 = jnp.max(jnp.where(sub == f1, -jnp.inf, blk), axis=0, keepdims=True)
        gs.append(m1 + m2)
    gs = jnp.concatenate(gs, axis=0)
    giota = lax.broadcasted_iota(I32, (N_EXPERT_GROUPS, tm), 0)
    keep = jnp.zeros((N_EXPERT_GROUPS, tm), F32)
    for _ in range(TOPK_GROUPS):
        _, f = _first_argmax(gs, giota, N_EXPERT_GROUPS)
        hit = giota == f
        keep = jnp.where(hit, 1.0, keep)
        gs = jnp.where(hit, -jnp.inf, gs)
    cur = jnp.concatenate([jnp.where(keep[g:g + 1, :] > 0.0, blocks[g], -jnp.inf)
                           for g in range(N_EXPERT_GROUPS)], axis=0)
    eiota = lax.broadcasted_iota(I32, (N_EXPERTS, tm), 0)
    chosen = jnp.zeros((N_EXPERTS, tm), F32)
    idx, sel = [], []
    for _ in range(TOP_K):
        _, f = _first_argmax(cur, eiota, N_EXPERTS)
        hit = eiota == f
        idx.append(f)
        sel.append(jnp.sum(jnp.where(hit, scores, 0.0), axis=0, keepdims=True))
        cur = jnp.where(hit, -jnp.inf, cur)
        chosen = jnp.where(hit, 1.0, chosen)
    sel = jnp.concatenate(sel, axis=0)
    idx = jnp.concatenate(idx, axis=0)
    gate_ref[...] = sel / jnp.sum(sel, axis=0, keepdims=True) * ROUTED_SCALE
    idx_ref[...] = idx
    before = jnp.where(lax.broadcasted_iota(I32, (tm, tm), 0) < lax.broadcasted_iota(I32, (tm, tm), 1), 1.0, 0.0)
    rank = _dot(chosen.astype(BF16), before.astype(BF16)) + carry_ref[:, 0:1]
    rank_ref[...] = jnp.concatenate(
        [jnp.sum(jnp.where(eiota == idx[k:k + 1, :], rank, 0.0), axis=0, keepdims=True) for k in range(TOP_K)],
        axis=0).astype(I32)
    carry_ref[...] = carry_ref[...] + jnp.sum(chosen, axis=1, keepdims=True)
    cnt_ref[...] = carry_ref[...]


def _route(logits_t, router_bias, tm):
    n_tok = logits_t.shape[1]
    tok = lambda rows: pl.BlockSpec((rows, tm), lambda i: (0, i))
    return pl.pallas_call(
        _route_kernel,
        grid=(n_tok // tm,),
        in_specs=[tok(N_EXPERTS), _const_spec(N_EXPERTS, 1)],
        out_specs=[tok(TOP_K), tok(TOP_K), tok(TOP_K), _const_spec(N_EXPERTS, 128)],
        out_shape=[jax.ShapeDtypeStruct((TOP_K, n_tok), I32),
                   jax.ShapeDtypeStruct((TOP_K, n_tok), F32),
                   jax.ShapeDtypeStruct((TOP_K, n_tok), I32),
                   jax.ShapeDtypeStruct((N_EXPERTS, 128), F32)],
        scratch_shapes=[pltpu.VMEM((N_EXPERTS, 128), F32)],
        compiler_params=_cparams(("arbitrary",), 32),
        name="route",
    )(logits_t, router_bias.astype(F32).reshape(N_EXPERTS, 1))


def _moe_kernel(be_ref, na_ref, src_ref, srcn_ref, dst_ref, h_hbm, wg_ref, wu_ref, wd_ref, y_hbm,
                xb, yb, wgb, wub, wdb, sem_in, sem_out):
    i = pl.program_id(0)
    n_steps = pl.num_programs(0)
    n_act = na_ref[0]
    cur = i % 2
    oth = 1 - cur
    bm = xb.shape[2]

    def tile(hbm, row):
        return hbm.at[pl.ds(pl.multiple_of(row * LANE_BLKS, LANE_BLKS), LANE_BLKS), :]

    def gather(idx_ref, s):
        for r in range(bm):
            pltpu.make_async_copy(tile(h_hbm, idx_ref[0, 0, r]), xb.at[s, :, r, :], sem_in.at[s]).start()

    def scatter(idx_ref, s):
        for r in range(bm):
            pltpu.make_async_copy(yb.at[s, :, r, :], tile(y_hbm, idx_ref[0, 0, r]), sem_out.at[s]).start()

    def wait_gather(s):
        for b in range(LANE_BLKS):
            pltpu.make_async_copy(h_hbm.at[pl.ds(0, bm), :], xb.at[s, b], sem_in.at[s]).wait()

    def wait_scatter(s):
        for b in range(LANE_BLKS):
            pltpu.make_async_copy(yb.at[s, b], y_hbm.at[pl.ds(0, bm), :], sem_out.at[s]).wait()

    @pl.when(i == 0)
    def _():
        gather(src_ref, 0)

    @pl.when(i < n_act)
    def _():
        wait_gather(cur)

        @pl.when(i >= 2)
        def _():
            wait_scatter(cur)

        @pl.when(jnp.logical_or(i == 0, be_ref[i] != be_ref[jnp.maximum(i - 1, 0)]))
        def _():
            wgb[...] = wg_ref[...].astype(BF16)
            wub[...] = wu_ref[...].astype(BF16)
            wdb[...] = wd_ref[...].astype(BF16)

        gather(srcn_ref, oth)
        xs = xb[cur]
        x = _unpack_bf16(jnp.concatenate([xs[b] for b in range(LANE_BLKS)], axis=1))
        hid = (_silu(_dot(x, wgb[...])) * _dot(x, wub[...])).astype(BF16)
        packed = _pack_halves(_dot(hid, wdb[...]))
        for b in range(LANE_BLKS):
            yb[cur, b] = packed[:, b * 128:(b + 1) * 128]
        scatter(dst_ref, cur)

    @pl.when(i == n_steps - 1)
    def _():
        wait_gather(n_act % 2)
        wait_scatter((n_act + 1) % 2)

        @pl.when(n_act >= 2)
        def _():
            wait_scatter(n_act % 2)


def _moe_experts(layer, h2p, blk_exp, n_act, src, dst, w_gate, w_up, w_down):
    n_blocks = blk_exp.shape[0]
    bm = MOE_BM
    smem_blk = lambda fn: pl.BlockSpec((1, 1, bm), fn, memory_space=pltpu.SMEM)
    wspec = lambda a, b: pl.BlockSpec((None, None, a, b), lambda i, be, na: (layer, be[i], 0, 0))
    grid_spec = pltpu.PrefetchScalarGridSpec(
        num_scalar_prefetch=2,
        grid=(n_blocks,),
        in_specs=[smem_blk(lambda i, be, na: (i, 0, 0)),
                  smem_blk(lambda i, be, na: (jnp.minimum(i + 1, n_blocks - 1), 0, 0)),
                  smem_blk(lambda i, be, na: (i, 0, 0)),
                  pl.BlockSpec(memory_space=pl.ANY),
                  wspec(D, EXPERT_FF), wspec(D, EXPERT_FF), wspec(EXPERT_FF, D)],
        out_specs=pl.BlockSpec(memory_space=pl.ANY),
        scratch_shapes=[pltpu.VMEM((2, LANE_BLKS, bm, 128), U32), pltpu.VMEM((2, LANE_BLKS, bm, 128), U32),
                        pltpu.VMEM((D, EXPERT_FF), BF16), pltpu.VMEM((D, EXPERT_FF), BF16),
                        pltpu.VMEM((EXPERT_FF, D), BF16),
                        pltpu.SemaphoreType.DMA((2,)), pltpu.SemaphoreType.DMA((2,))],
    )
    src3 = src.reshape(n_blocks, 1, bm)
    return pl.pallas_call(
        _moe_kernel,
        grid_spec=grid_spec,
        out_shape=jax.ShapeDtypeStruct((n_blocks * bm * LANE_BLKS, 128), U32),
        compiler_params=_cparams(("arbitrary",), 48),
        name="moe_experts",
    )(blk_exp, n_act, src3, src3, dst.reshape(n_blocks, 1, bm), h2p, w_gate, w_up, w_down)


def _shared_kernel(h_ref, wg_ref, wu_ref, wd_ref, o_ref):
    x = _unpack_bf16(_load_tile_rows(h_ref, o_ref.shape[0]))
    hid = (_silu(_dot(x, wg_ref[...])) * _dot(x, wu_ref[...])).astype(BF16)
    o_ref[...] = _dot(hid, wd_ref[...])


def _shared_expert(h2p, wg, wu, wd, tm):
    n_rows = h2p.shape[0] // LANE_BLKS
    return pl.pallas_call(
        _shared_kernel,
        grid=(n_rows // tm,),
        in_specs=[pl.BlockSpec((tm * LANE_BLKS, 128), lambda i: (i, 0)),
                  _const_spec(D, EXPERT_FF), _const_spec(D, EXPERT_FF), _const_spec(EXPERT_FF, D)],
        out_specs=pl.BlockSpec((tm, D), lambda i: (i, 0)),
        out_shape=jax.ShapeDtypeStruct((n_rows, D), F32),
        compiler_params=_cparams(("arbitrary",), 40),
        name="shared_expert",
    )(h2p, wg, wu, wd)


def _combine_kernel(*refs, final):
    x_ref, sh_ref, gt_ref, g2_ref = refs[:4]
    y_refs = refs[4:4 + TOP_K]
    fg_ref = refs[4 + TOP_K] if final else None
    o_ref = refs[-1]
    tm = x_ref.shape[0]
    gt = gt_ref[...]
    for b in range(LANE_BLKS):
        c_lo = slice(b * 128, (b + 1) * 128)
        c_hi = slice(HALF + b * 128, HALF + (b + 1) * 128)
        acc_lo = sh_ref[:, c_lo]
        acc_hi = sh_ref[:, c_hi]
        for k in range(TOP_K):
            lo, hi = _unpack_halves(y_refs[k][pl.ds(b, tm, stride=LANE_BLKS), :])
            acc_lo = acc_lo + lo * gt[:, k:k + 1]
            acc_hi = acc_hi + hi * gt[:, k:k + 1]
        o_ref[:, c_lo] = x_ref[:, c_lo] + g2_ref[:, c_lo] * acc_lo
        o_ref[:, c_hi] = x_ref[:, c_hi] + g2_ref[:, c_hi] * acc_hi
    if final:
        xn = o_ref[...]
        o_ref[...] = xn * lax.rsqrt(jnp.mean(xn * xn, axis=-1, keepdims=True) + EPS) * fg_ref[...]


def _combine(x, shared, gates, mod_l, y_tk, n_tok, final_g, tm):
    nt = n_tok // tm
    group_fn = _group_fn(tm)
    row = pl.BlockSpec((tm, D), lambda i: (i, 0))
    in_specs = [row, row, pl.BlockSpec((tm, TOP_K), lambda i: (i, 0)), _mod_spec(5, group_fn)]
    in_specs += [pl.BlockSpec((tm * LANE_BLKS, 128), functools.partial(lambda i, k: (k * nt + i, 0), k=k))
                 for k in range(TOP_K)]
    args = [x, shared, gates, mod_l] + [y_tk] * TOP_K
    final = final_g is not None
    if final:
        in_specs.append(_const_spec(1, D))
        args.append(final_g)
    return pl.pallas_call(
        functools.partial(_combine_kernel, final=final),
        grid=(nt,),
        in_specs=in_specs,
        out_specs=row,
        out_shape=jax.ShapeDtypeStruct((n_tok, D), F32),
        compiler_params=_cparams(("arbitrary",), 40),
        name="moe_combine",
    )(*args)


def _dispatch_plan(idx_t, rank_t, cnt, n_tok):
    bm = MOE_BM
    n_assign = n_tok * TOP_K
    n_blocks = n_assign // bm + N_EXPERTS
    n_rows = n_blocks * bm
    sizes = cnt[:, 0].astype(I32)
    padded = (sizes + bm - 1) // bm * bm
    pad_end = jnp.cumsum(padded)
    pad_start = pad_end - padded
    blk_first = jnp.arange(n_blocks, dtype=I32) * bm
    blk_exp = jnp.minimum(jnp.sum((pad_end[None, :] <= blk_first[:, None]).astype(I32), axis=1), N_EXPERTS - 1)
    n_act = (pad_end[-1] // bm).astype(I32).reshape(1)
    experts = jnp.arange(N_EXPERTS, dtype=I32)
    slot = jnp.sum(jnp.where(idx_t[:, :, None] == experts, pad_start, 0), axis=-1) + rank_t
    inv = jnp.full((n_rows,), -1, I32).at[slot.reshape(-1)].set(jnp.arange(n_assign, dtype=I32))
    valid = inv >= 0
    spare = n_assign + jnp.repeat(blk_exp, bm) * bm + jnp.arange(n_rows, dtype=I32) % bm
    src = jnp.where(valid, inv % n_tok, 0).astype(I32)
    dst = jnp.where(valid, inv, spare).astype(I32)
    return blk_exp, n_act, src, dst


def _moe_layer(layer, x, h2p, logits_t, router_bias, mod_l, exp_w, shared_w, final_g):
    n_tok = h2p.shape[0] // LANE_BLKS
    idx_t, gates_t, rank_t, cnt = _route(logits_t, router_bias, 512)
    blk_exp, n_act, src, dst = _dispatch_plan(idx_t, rank_t, cnt, n_tok)
    y_tk = _moe_experts(layer, h2p, blk_exp, n_act, src, dst, *exp_w)
    shared = _shared_expert(h2p, *[w[layer].astype(BF16) for w in shared_w], 512)
    return _combine(x, shared, gates_t.T, mod_l, y_tk, n_tok, final_g, 128)


def _rope_tables():
    rows = SEQ // GRID_W
    row = jnp.broadcast_to(jnp.arange(rows)[:, None], (rows, GRID_W)).reshape(-1).astype(F32)
    col = jnp.broadcast_to(jnp.arange(GRID_W)[None, :], (rows, GRID_W)).reshape(-1).astype(F32)
    inv = ROPE_BASE ** (-jnp.arange(0, AXIS_DIM, 2, dtype=F32) / AXIS_DIM)
    ang_r = row[:, None] * inv
    ang_c = col[:, None] * inv
    ang = jnp.concatenate([ang_r, ang_r, ang_c, ang_c], axis=-1)
    cos, sin = jnp.cos(ang), jnp.sin(ang)
    sign = jnp.where((jnp.arange(HEAD_DIM) % AXIS_DIM) < AXIS_DIM // 2, -1.0, 1.0).astype(F32)
    ident = 512
    cos_t = jnp.concatenate([jnp.tile(cos, (1, 2)), jnp.ones((ident, 128), F32)], axis=0)
    sin_t = jnp.concatenate([jnp.tile(sin * sign, (1, 2)), jnp.zeros((ident, 128), F32)], axis=0)
    return cos_t, sin_t


def _head_perm():
    g = jnp.arange(Q_PER_KV)[:, None, None]
    kvh = jnp.arange(N_KV_HEADS)[None, :, None]
    d = jnp.arange(HEAD_DIM)[None, None, :]
    return ((kvh * Q_PER_KV + g) * HEAD_DIM + d).reshape(-1)


def kernel(x, c, ctx, c_ctx, ada_w, ada_b, norm1_g, norm2_g, even_w_in, gmlp_ln_g, gmlp_ln_b, gmlp_ws, gmlp_bs, pool_w, pool_scale, even_w_out, odd_w_in, conv_w, conv_b, conv_ln_g, conv_ln_b, attn_sink, odd_w_out, router_w, router_bias, exp_w_gate, exp_w_up, exp_w_down, shared_w_gate, shared_w_up, shared_w_down, final_g):
    mod = _ada_mod(c, c_ctx, ada_w, ada_b)
    row = lambda a: a.reshape(1, -1)
    exp_w = (exp_w_gate, exp_w_up, exp_w_down)
    shared_w = (shared_w_gate, shared_w_up, shared_w_down)

    def router_split(i):
        w = router_w[i].T
        hi = w.astype(BF16)
        return hi, (w - hi.astype(F32)).astype(BF16)

    x0 = jnp.concatenate([x.reshape(N_LAT, D), ctx.reshape(N_CTX, D)], axis=0)

    uv, z = _even_in(x0, mod[0], row(norm1_g[0]), even_w_in[0].astype(BF16), 512)
    y = _even_mix(uv, z, row(gmlp_ln_g[0]), row(gmlp_ln_b[0]), gmlp_ws[0].astype(BF16),
                  gmlp_bs[0].reshape(A_GROUPS, CHUNK, 1), pool_w[0].astype(BF16), row(pool_scale[0]), 256)
    x1, h2p, lg = _mix_out(y, 0, y, 1, even_w_out[0].astype(BF16).reshape(2, HALF, D), x0, N_ALL, mod[0],
                           row(norm2_g[0]), *router_split(0), 256)
    x1 = _moe_layer(0, x1, h2p, lg, router_bias[0], mod[0], exp_w, shared_w, None)

    perm = _head_perm()
    w_in1 = odd_w_in[0]
    w_main = jnp.concatenate([w_in1[:, :2 * HALF], w_in1[:, 2 * HALF:3 * HALF][:, perm]], axis=1).astype(BF16)
    w_kv = w_in1[:, 3 * HALF:].astype(BF16)
    w_out1 = odd_w_out[0]
    w_out1 = jnp.stack([w_out1[:HALF], w_out1[HALF:][perm]], axis=0).astype(BF16)
    cos_t, sin_t = _rope_tables()
    zc, q, k, v = _odd_in(x1, mod[1], row(norm1_g[1]), w_main, w_kv, cos_t, sin_t, 512)
    y_conv = _conv_module(zc, N_LAT, conv_w[0], row(conv_b[0]), row(conv_ln_g[0]), row(conv_ln_b[0]), 256)
    y_attn = _attention(q, k, v, attn_sink[0].astype(F32))
    x2, h2p, lg = _mix_out(y_conv, 0, y_attn, 0, w_out1, x1, N_LAT, mod[1], row(norm2_g[1]),
                           *router_split(1), 256)
    out = _moe_layer(1, x2, h2p, lg, router_bias[1], mod[1], exp_w, shared_w, row(final_g))
    return out.reshape(BATCH, SEQ, D)
```

```python
import functools

import jax
import jax.numpy as jnp
from jax import lax
from jax.experimental import pallas as pl
from jax.experimental.pallas import tpu as pltpu

F32 = jnp.float32
BF16 = jnp.bfloat16
U32 = jnp.uint32
I32 = jnp.int32

D = 2048
BATCH = 4
SEQ = 4096
DEPTH = 2
GRID_W = 64
CTX_LEN = 256
HALF = D // 2
CHUNK = 128
A_GROUPS = 4
A_GW = HALF // A_GROUPS
POOL_WINDOWS = (2, 4, 8, 16)
B_GW = HALF // len(POOL_WINDOWS)
CONV_W = 31
HEAD_DIM = 64
N_Q_HEADS = HALF // HEAD_DIM
N_KV_HEADS = 2
Q_PER_KV = N_Q_HEADS // N_KV_HEADS
KV_W = N_KV_HEADS * HEAD_DIM
ATT_BLK = 128
WINDOW = 128
AXIS_DIM = HEAD_DIM // 2
ROPE_BASE = 10000.0
N_EXPERTS = 64
N_EXPERT_GROUPS = 8
PER_GROUP = N_EXPERTS // N_EXPERT_GROUPS
TOPK_GROUPS = 4
TOP_K = 8
EXPERT_FF = 512
ROUTED_SCALE = 2.5
EPS = 1e-6

N_LAT = BATCH * SEQ
N_CTX = BATCH * CTX_LEN
N_ALL = N_LAT + N_CTX
CTX_GROUP = BATCH
HALO = 128
CONV_HALO = 16
MOE_BM = 256

MIB = 1024 * 1024


def _cparams(sem, vmem_mib):
    return pltpu.CompilerParams(dimension_semantics=sem, vmem_limit_bytes=vmem_mib * MIB)


def _dot(a, b):
    return jnp.dot(a, b, preferred_element_type=F32)


def _dot_nt(a, b):
    return lax.dot_general(a, b, (((1,), (1,)), ((), ())), preferred_element_type=F32)


def _rms_mod(x, g, sh, sc):
    y = x * lax.rsqrt(jnp.mean(x * x, axis=-1, keepdims=True) + EPS) * g
    return y * (1.0 + sc) + sh


def _layer_norm(x, g, b):
    mu = jnp.mean(x, axis=-1, keepdims=True)
    xc = x - mu
    var = jnp.mean(xc * xc, axis=-1, keepdims=True)
    return xc * lax.rsqrt(var + EPS) * g + b


def _gelu(x):
    return 0.5 * x * (1.0 + lax.erf(x * (2.0 ** -0.5)))


def _silu(x):
    return x * jax.nn.sigmoid(x)


def _split_bf16(x):
    hi = x.astype(BF16)
    lo = (x - hi.astype(F32)).astype(BF16)
    return hi, lo


def _pack_halves(y):
    n = y.shape[1] // 2
    lo = lax.bitcast_convert_type(y[:, :n].astype(BF16).astype(F32), U32) >> 16
    hi = lax.bitcast_convert_type(y[:, n:].astype(BF16).astype(F32), U32) & U32(0xFFFF0000)
    return hi | lo


def _unpack_halves(p):
    lo = lax.bitcast_convert_type(p << 16, F32)
    hi = lax.bitcast_convert_type(p & U32(0xFFFF0000), F32)
    return lo, hi


def _unpack_bf16(p):
    lo, hi = _unpack_halves(p)
    return jnp.concatenate([lo.astype(BF16), hi.astype(BF16)], axis=1)


LANE_BLKS = HALF // 128


def _store_tile_rows(ref, packed):
    tm = packed.shape[0]
    for s in range(LANE_BLKS):
        ref[pl.ds(s, tm, stride=LANE_BLKS), :] = packed[:, s * 128:(s + 1) * 128]


def _load_tile_rows(ref, tm):
    return jnp.concatenate([ref[pl.ds(s, tm, stride=LANE_BLKS), :] for s in range(LANE_BLKS)], axis=1)


def _mod_spec(chunk, group_fn):
    return pl.BlockSpec((None, None, 1, D), lambda i, *_: (group_fn(i), chunk, 0, 0))


def _group_fn(tm):
    return lambda i: jnp.minimum(i // (SEQ // tm), CTX_GROUP)


def _const_spec(*shape):
    return pl.BlockSpec(shape, lambda *_: (0,) * len(shape))


def _ada_kernel(c_ref, w_ref, b_ref, o_ref):
    s = _silu(c_ref[...]).astype(BF16)
    o_ref[...] = _dot(s, w_ref[...].astype(BF16)) + b_ref[...]


def _ada_mod(c, c_ctx, ada_w, ada_b):
    tn = 1024
    cv = jnp.zeros((8, D), F32).at[:BATCH].set(c).at[CTX_GROUP].set(c_ctx)
    out = pl.pallas_call(
        _ada_kernel,
        grid=(DEPTH, 6 * D // tn),
        in_specs=[pl.BlockSpec((8, D), lambda l, j: (0, 0)),
                  pl.BlockSpec((None, D, tn), lambda l, j: (l, 0, j)),
                  pl.BlockSpec((None, 1, tn), lambda l, j: (l, 0, j))],
        out_specs=pl.BlockSpec((None, 8, tn), lambda l, j: (l, 0, j)),
        out_shape=jax.ShapeDtypeStruct((DEPTH, 8, 6 * D), F32),
        compiler_params=_cparams(("arbitrary", "arbitrary"), 40),
        name="ada_mod",
    )(cv, ada_w, ada_b.reshape(DEPTH, 1, 6 * D))
    return out.reshape(DEPTH, 8, 6, 1, D)


def _even_in_kernel(x_ref, g_ref, sh_ref, sc_ref, w_ref, uv_ref, z_ref, h_ref):
    j = pl.program_id(1)

    @pl.when(j == 0)
    def _():
        h_ref[...] = _rms_mod(x_ref[...], g_ref[...], sh_ref[...], sc_ref[...]).astype(BF16)

    acc = _dot(h_ref[...], w_ref[...])

    @pl.when(j < 2)
    def _():
        uv_ref[...] = _gelu(acc).astype(BF16)

    @pl.when(j == 2)
    def _():
        z_ref[...] = acc


def _even_in(x, mod_l, norm_g, w_in, tm):
    n_rows = x.shape[0]
    group_fn = _group_fn(tm)
    return pl.pallas_call(
        _even_in_kernel,
        grid=(n_rows // tm, 3),
        in_specs=[pl.BlockSpec((tm, D), lambda i, j: (i, 0)),
                  _const_spec(1, D),
                  _mod_spec(0, group_fn), _mod_spec(1, group_fn),
                  pl.BlockSpec((D, HALF), lambda i, j: (0, j))],
        out_specs=[pl.BlockSpec((tm, HALF), lambda i, j: (i, jnp.minimum(j, 1))),
                   pl.BlockSpec((tm, HALF), lambda i, j: (i, 0))],
        out_shape=[jax.ShapeDtypeStruct((n_rows, 2 * HALF), BF16),
                   jax.ShapeDtypeStruct((n_rows, HALF), F32)],
        scratch_shapes=[pltpu.VMEM((tm, D), BF16)],
        compiler_params=_cparams(("arbitrary", "arbitrary"), 40),
        name="even_in",
    )(x, norm_g, mod_l, mod_l, w_in)


def _band(d, w):
    inside = lax.bitcast_convert_type(d + w // 2, U32) < U32(w)
    return jnp.where(inside, 1.0, 0.0).astype(BF16)


def _seq_tile(i, tm):
    lat_tiles = N_LAT // tm
    is_lat = i < lat_tiles
    it = i % (SEQ // tm)
    first = jnp.logical_or(jnp.logical_not(is_lat), it == 0)
    last = jnp.logical_or(jnp.logical_not(is_lat), it == SEQ // tm - 1)
    pos0 = jnp.where(is_lat, it * tm, 0)
    seq_len = jnp.where(is_lat, SEQ, CTX_LEN)
    return first, last, pos0, seq_len


def _even_mix_kernel(u_ref, v_ref, z_ref, zp_ref, zn_ref, lng_ref, lnb_ref, ws_ref, bs_ref,
                     wp_ref, ps_ref, y_ref, *, tm):
    first, last, pos0, seq_len = _seq_tile(pl.program_id(0), tm)
    vn = _layer_norm(v_ref[...].astype(F32), lng_ref[...], lnb_ref[...]).astype(BF16)
    for g in range(A_GROUPS):
        cs = slice(g * A_GW, (g + 1) * A_GW)
        for c in range(tm // CHUNK):
            rs = slice(c * CHUNK, (c + 1) * CHUNK)
            mixed = _dot(ws_ref[g], vn[rs, cs]) + bs_ref[g]
            y_ref[rs, cs] = (u_ref[rs, cs].astype(F32) * mixed).astype(BF16)
    z = z_ref[...]
    zp = jnp.where(first, 0.0, zp_ref[...])
    zn = jnp.where(last, 0.0, zn_ref[...])
    z_hi, z_lo = _split_bf16(z)
    zp_hi, zp_lo = _split_bf16(zp)
    zn_hi, zn_lo = _split_bf16(zn)
    d_main = (lax.broadcasted_iota(I32, (tm, tm), 1) - lax.broadcasted_iota(I32, (tm, tm), 0))
    d_halo = (lax.broadcasted_iota(I32, (tm, HALO), 1) - lax.broadcasted_iota(I32, (tm, HALO), 0))
    pos = pos0 + lax.broadcasted_iota(I32, (tm, 1), 0)
    for g, w in enumerate(POOL_WINDOWS):
        cs = slice(g * B_GW, (g + 1) * B_GW)
        bm_ = _band(d_main, w)
        bp = _band(d_halo - HALO, w)
        bn = _band(d_halo + tm, w)
        tot = (_dot(bm_, z_hi[:, cs]) + _dot(bm_, z_lo[:, cs])
               + _dot(bp, zp_hi[:, cs]) + _dot(bp, zp_lo[:, cs])
               + _dot(bn, zn_hi[:, cs]) + _dot(bn, zn_lo[:, cs]))
        cnt = (jnp.minimum(pos + w // 2, seq_len) - jnp.maximum(pos - w // 2, 0)).astype(F32)
        pooled = (tot / cnt - z[:, cs]).astype(BF16)
        y_ref[:, HALF + g * B_GW:HALF + (g + 1) * B_GW] = (
            _dot(pooled, wp_ref[g]) * ps_ref[:, cs]).astype(BF16)


def _even_mix(uv, z, ln_g, ln_b, ws, bs, wp, ps, tm):
    n_rows = z.shape[0]
    hb = tm // HALO
    n_hblk = n_rows // HALO
    return pl.pallas_call(
        functools.partial(_even_mix_kernel, tm=tm),
        grid=(n_rows // tm,),
        in_specs=[pl.BlockSpec((tm, HALF), lambda i: (i, 0)),
                  pl.BlockSpec((tm, HALF), lambda i: (i, 1)),
                  pl.BlockSpec((tm, HALF), lambda i: (i, 0)),
                  pl.BlockSpec((HALO, HALF), lambda i: (jnp.maximum(i * hb - 1, 0), 0)),
                  pl.BlockSpec((HALO, HALF), lambda i: (jnp.minimum((i + 1) * hb, n_hblk - 1), 0)),
                  _const_spec(1, HALF), _const_spec(1, HALF),
                  _const_spec(A_GROUPS, CHUNK, CHUNK), _const_spec(A_GROUPS, CHUNK, 1),
                  _const_spec(len(POOL_WINDOWS), B_GW, B_GW), _const_spec(1, HALF)],
        out_specs=pl.BlockSpec((tm, D), lambda i: (i, 0)),
        out_shape=jax.ShapeDtypeStruct((n_rows, D), BF16),
        compiler_params=_cparams(("arbitrary",), 40),
        name="even_mix",
    )(uv, uv, z, z, z, ln_g, ln_b, ws, bs, wp, ps)


def _mix_out_kernel(ya_ref, yb_ref, w_ref, x_ref, g1_ref, n2_ref, sh_ref, sc_ref, rwh_ref, rwl_ref,
                    xo_ref, hp_ref, lg_ref):
    o = _dot(ya_ref[...], w_ref[0]) + _dot(yb_ref[...], w_ref[1])
    xn = x_ref[...] + g1_ref[...] * o
    xo_ref[...] = xn
    h = _rms_mod(xn, n2_ref[...], sh_ref[...], sc_ref[...])
    _store_tile_rows(hp_ref, _pack_halves(h))
    h_hi, h_lo = _split_bf16(h)
    lg_ref[...] = _dot_nt(rwh_ref[...], h_hi) + _dot_nt(rwl_ref[...], h_hi) + _dot_nt(rwh_ref[...], h_lo)


def _mix_out(ya, ya_col, yb, yb_col, w_out, x, n_rows, mod_l, norm2_g, rw_hi, rw_lo, tm):
    group_fn = _group_fn(tm)
    return pl.pallas_call(
        _mix_out_kernel,
        grid=(n_rows // tm,),
        in_specs=[pl.BlockSpec((tm, HALF), lambda i: (i, ya_col)),
                  pl.BlockSpec((tm, HALF), lambda i: (i, yb_col)),
                  _const_spec(2, HALF, D),
                  pl.BlockSpec((tm, D), lambda i: (i, 0)),
                  _mod_spec(2, group_fn), _const_spec(1, D), _mod_spec(3, group_fn), _mod_spec(4, group_fn),
                  _const_spec(N_EXPERTS, D), _const_spec(N_EXPERTS, D)],
        out_specs=[pl.BlockSpec((tm, D), lambda i: (i, 0)),
                   pl.BlockSpec((tm * LANE_BLKS, 128), lambda i: (i, 0)),
                   pl.BlockSpec((N_EXPERTS, tm), lambda i: (0, i))],
        out_shape=[jax.ShapeDtypeStruct((n_rows, D), F32),
                   jax.ShapeDtypeStruct((n_rows * LANE_BLKS, 128), U32),
                   jax.ShapeDtypeStruct((N_EXPERTS, n_rows), F32)],
        compiler_params=_cparams(("arbitrary",), 48),
        name="mix_out",
    )(ya, yb, w_out, x, mod_l, norm2_g, mod_l, mod_l, rw_hi, rw_lo)


def _rope(x, cos, sin_signed, first_half):
    partner = jnp.where(first_half, pltpu.roll(x, 128 - AXIS_DIM // 2, 1), pltpu.roll(x, AXIS_DIM // 2, 1))
    return x * cos + partner * sin_signed


def _odd_in_kernel(x_ref, g_ref, sh_ref, sc_ref, w_ref, wkv_ref, cos_ref, sin_ref,
                   zc_ref, q_ref, k_ref, v_ref, h_ref, a_ref):
    j = pl.program_id(1)
    tm = x_ref.shape[0]

    @pl.when(j == 0)
    def _():
        h_ref[...] = _rms_mod(x_ref[...], g_ref[...], sh_ref[...], sc_ref[...]).astype(BF16)

    first_half = (lax.broadcasted_iota(I32, (tm, 128), 1) % AXIS_DIM) < (AXIS_DIM // 2)

    @pl.when(j == 0)
    def _():
        a_ref[...] = _dot(h_ref[...], w_ref[...])

    @pl.when(j == 1)
    def _():
        zc_ref[...] = a_ref[...] * jax.nn.sigmoid(_dot(h_ref[...], w_ref[...]))

    @pl.when(j == 2)
    def _():
        q = _dot(h_ref[...], w_ref[...])
        cos = cos_ref[...]
        sin = sin_ref[...]
        for b in range(HALF // 128):
            cs = slice(b * 128, (b + 1) * 128)
            q_ref[:, cs] = (_rope(q[:, cs], cos, sin, first_half) * (HEAD_DIM ** -0.5)).astype(BF16)

    @pl.when(j == 3)
    def _():
        kv = _dot(h_ref[...], wkv_ref[...])
        k_ref[...] = _rope(kv[:, :KV_W], cos_ref[...], sin_ref[...], first_half).astype(BF16)
        v_ref[...] = kv[:, KV_W:].astype(BF16)


def _odd_in(x, mod_l, norm_g, w_main, w_kv, cos_t, sin_t, tm):
    n_rows = x.shape[0]
    group_fn = _group_fn(tm)
    lat_tiles = N_LAT // tm
    pos_blk = lambda i, j: (jnp.where(i < lat_tiles, i % (SEQ // tm), SEQ // tm), 0)
    row = lambda w: pl.BlockSpec((tm, w), lambda i, j: (i, 0))
    return pl.pallas_call(
        _odd_in_kernel,
        grid=(n_rows // tm, 4),
        in_specs=[pl.BlockSpec((tm, D), lambda i, j: (i, 0)),
                  _const_spec(1, D),
                  _mod_spec(0, group_fn), _mod_spec(1, group_fn),
                  pl.BlockSpec((D, HALF), lambda i, j: (0, jnp.minimum(j, 2))),
                  _const_spec(D, 2 * KV_W),
                  pl.BlockSpec((tm, 128), pos_blk), pl.BlockSpec((tm, 128), pos_blk)],
        out_specs=[row(HALF), row(HALF), row(KV_W), row(KV_W)],
        out_shape=[jax.ShapeDtypeStruct((n_rows, HALF), F32),
                   jax.ShapeDtypeStruct((n_rows, HALF), BF16),
                   jax.ShapeDtypeStruct((n_rows, KV_W), BF16),
                   jax.ShapeDtypeStruct((n_rows, KV_W), BF16)],
        scratch_shapes=[pltpu.VMEM((tm, D), BF16), pltpu.VMEM((tm, HALF), F32)],
        compiler_params=_cparams(("arbitrary", "arbitrary"), 40),
        name="odd_in",
    )(x, norm_g, mod_l, mod_l, w_main, w_kv, cos_t, sin_t)


def _conv_kernel(z_ref, zp_ref, zn_ref, w_ref, b_ref, lng_ref, lnb_ref, y_ref, ze_ref, c_ref, *, tm):
    first, last, _, _ = _seq_tile(pl.program_id(0), tm)
    ze_ref[0:CONV_HALO, :] = jnp.where(first, 0.0, zp_ref[...])
    ze_ref[CONV_HALO:CONV_HALO + tm, :] = z_ref[...]
    ze_ref[CONV_HALO + tm:, :] = jnp.where(last, 0.0, zn_ref[...])
    rc = 64
    base = CONV_HALO - CONV_W // 2

    def lane_block(cb, _):
        cs = pl.ds(pl.multiple_of(cb * 128, 128), 128)
        for r in range(tm // rc):
            acc = jnp.zeros((rc, 128), F32)
            for t in range(CONV_W):
                acc = acc + w_ref[t:t + 1, cs] * ze_ref[base + r * rc + t:base + r * rc + t + rc, cs]
            c_ref[r * rc:(r + 1) * rc, cs] = acc
        return 0

    lax.fori_loop(0, HALF // 128, lane_block, 0)
    y = _layer_norm(c_ref[...] + b_ref[...], lng_ref[...], lnb_ref[...])
    y_ref[...] = _silu(y).astype(BF16)


def _conv_module(zc, n_rows, conv_w, conv_b, ln_g, ln_b, tm):
    hb = tm // CONV_HALO
    n_hblk = zc.shape[0] // CONV_HALO
    return pl.pallas_call(
        functools.partial(_conv_kernel, tm=tm),
        grid=(n_rows // tm,),
        in_specs=[pl.BlockSpec((tm, HALF), lambda i: (i, 0)),
                  pl.BlockSpec((CONV_HALO, HALF), lambda i: (jnp.maximum(i * hb - 1, 0), 0)),
                  pl.BlockSpec((CONV_HALO, HALF), lambda i: (jnp.minimum((i + 1) * hb, n_hblk - 1), 0)),
                  _const_spec(CONV_W, HALF), _const_spec(1, HALF), _const_spec(1, HALF), _const_spec(1, HALF)],
        out_specs=pl.BlockSpec((tm, HALF), lambda i: (i, 0)),
        out_shape=jax.ShapeDtypeStruct((n_rows, HALF), BF16),
        scratch_shapes=[pltpu.VMEM((tm + 2 * CONV_HALO, HALF), F32), pltpu.VMEM((tm, HALF), F32)],
        compiler_params=_cparams(("arbitrary",), 32),
        name="conv_module",
    )(zc, zc, zc, conv_w, conv_b, ln_g, ln_b)


def _attn_kernel(sink_ref, q_ref, kp_ref, kc_ref, kn_ref, vp_ref, vc_ref, vn_ref, kx_ref, vx_ref, o_ref):
    i = pl.program_id(1)
    n_keys = 3 * ATT_BLK + CTX_LEN
    kb = jnp.concatenate([kp_ref[...], kc_ref[...], kn_ref[...], kx_ref[...]], axis=0)
    vb = jnp.concatenate([vp_ref[...], vc_ref[...], vn_ref[...], vx_ref[...]], axis=0)
    key_lane = lax.broadcasted_iota(I32, (n_keys, 2 * HEAD_DIM), 1)
    k_head = [jnp.where(key_lane < HEAD_DIM, kb, jnp.zeros_like(kb)),
              jnp.where(key_lane >= HEAD_DIM, kb, jnp.zeros_like(kb))]
    r = lax.broadcasted_iota(I32, (ATT_BLK, n_keys), 0)
    c = lax.broadcasted_iota(I32, (ATT_BLK, n_keys), 1)
    kpos = c + (i - 1) * ATT_BLK
    band_ok = lax.bitcast_convert_type(c - r, U32) <= U32(2 * WINDOW)
    in_seq = lax.bitcast_convert_type(kpos, U32) < U32(SEQ)
    bias = jnp.where(c >= 3 * ATT_BLK, 0.0, jnp.where(band_ok, jnp.where(in_seq, 0.0, -jnp.inf), -jnp.inf))
    out_lane = lax.broadcasted_iota(I32, (ATT_BLK, 2 * HEAD_DIM), 1)
    for g in range(Q_PER_KV):
        qg = q_ref[:, g * 128:(g + 1) * 128]
        outs = []
        for kvh in range(N_KV_HEADS):
            sk = sink_ref[kvh * Q_PER_KV + g]
            s = _dot_nt(qg, k_head[kvh]) + bias
            m = jnp.maximum(jnp.max(s, axis=-1, keepdims=True), sk)
            e = jnp.exp(s - m)
            den = jnp.sum(e, axis=-1, keepdims=True) + jnp.exp(sk - m)
            outs.append(_dot(e.astype(BF16), vb) / den)
        o_ref[:, g * 128:(g + 1) * 128] = jnp.where(out_lane < HEAD_DIM, outs[0], outs[1]).astype(BF16)


def _attention(q, k, v, sink):
    nb = SEQ // ATT_BLK
    ctx0 = N_LAT // CTX_LEN
    blk = lambda w, fn: pl.BlockSpec((ATT_BLK, w), fn)
    prev = lambda b, i, s: (b * nb + jnp.maximum(i - 1, 0), 0)
    cur = lambda b, i, s: (b * nb + i, 0)
    nxt = lambda b, i, s: (b * nb + jnp.minimum(i + 1, nb - 1), 0)
    ctx = pl.BlockSpec((CTX_LEN, KV_W), lambda b, i, s: (ctx0 + b, 0))
    grid_spec = pltpu.PrefetchScalarGridSpec(
        num_scalar_prefetch=1,
        grid=(BATCH, nb),
        in_specs=[blk(HALF, cur), blk(KV_W, prev), blk(KV_W, cur), blk(KV_W, nxt),
                  blk(KV_W, prev), blk(KV_W, cur), blk(KV_W, nxt), ctx, ctx],
        out_specs=blk(HALF, cur),
    )
    return pl.pallas_call(
        _attn_kernel,
        grid_spec=grid_spec,
        out_shape=jax.ShapeDtypeStruct((N_LAT, HALF), BF16),
        compiler_params=_cparams(("arbitrary", "arbitrary"), 32),
        name="attention",
    )(sink, q, k, k, k, v, v, v, k, v)


def _first_argmax(x, iota, n):
    m = jnp.max(x, axis=0, keepdims=True)
    first = jnp.min(jnp.where(x == m, iota, n), axis=0, keepdims=True)
    return m, first


def _route_kernel(lg_ref, bias_ref, idx_ref, gate_ref, rank_ref, cnt_ref, carry_ref):
    tm = lg_ref.shape[1]

    @pl.when(pl.program_id(0) == 0)
    def _():
        carry_ref[...] = jnp.zeros_like(carry_ref)

    scores = jax.nn.sigmoid(lg_ref[...])
    biased = scores + bias_ref[...]
    sub = lax.broadcasted_iota(I32, (PER_GROUP, tm), 0)
    blocks = [biased[g * PER_GROUP:(g + 1) * PER_GROUP, :] for g in range(N_EXPERT_GROUPS)]
    gs = []
    for blk in blocks:
        m1, f1 = _first_argmax(blk, sub, PER_GROUP)
        m2 = jnp.max(jnp.where(sub == f1, -jnp.inf, blk), axis=0, keepdims=True)
        gs.append(m1 + m2)
    gs = jnp.concatenate(gs, axis=0)
    giota = lax.broadcasted_iota(I32, (N_EXPERT_GROUPS, tm), 0)
    keep = jnp.zeros((N_EXPERT_GROUPS, tm), F32)
    for _ in range(TOPK_GROUPS):
        _, f = _first_argmax(gs, giota, N_EXPERT_GROUPS)
        hit = giota == f
        keep = jnp.where(hit, 1.0, keep)
        gs = jnp.where(hit, -jnp.inf, gs)
    cur = jnp.concatenate([jnp.where(keep[g:g + 1, :] > 0.0, blocks[g], -jnp.inf)
                           for g in range(N_EXPERT_GROUPS)], axis=0)
    eiota = lax.broadcasted_iota(I32, (N_EXPERTS, tm), 0)
    chosen = jnp.zeros((N_EXPERTS, tm), F32)
    idx, sel = [], []
    for _ in range(TOP_K):
        _, f = _first_argmax(cur, eiota, N_EXPERTS)
        hit = eiota == f
        idx.append(f)
        sel.append(jnp.sum(jnp.where(hit, scores, 0.0), axis=0, keepdims=True))
        cur = jnp.where(hit, -jnp.inf, cur)
        chosen = jnp.where(hit, 1.0, chosen)
    sel = jnp.concatenate(sel, axis=0)
    idx = jnp.concatenate(idx, axis=0)
    gate_ref[...] = sel / jnp.sum(sel, axis=0, keepdims=True) * ROUTED_SCALE
    idx_ref[...] = idx
    before = jnp.where(lax.broadcasted_iota(I32, (tm, tm), 0) < lax.broadcasted_iota(I32, (tm, tm), 1), 1.0, 0.0)
    rank = _dot(chosen.astype(BF16), before.astype(BF16)) + carry_ref[:, 0:1]
    rank_ref[...] = jnp.concatenate(
        [jnp.sum(jnp.where(eiota == idx[k:k + 1, :], rank, 0.0), axis=0, keepdims=True) for k in range(TOP_K)],
        axis=0).astype(I32)
    carry_ref[...] = carry_ref[...] + jnp.sum(chosen, axis=1, keepdims=True)
    cnt_ref[...] = carry_ref[...]


def _route(logits_t, router_bias, tm):
    n_tok = logits_t.shape[1]
    tok = lambda rows: pl.BlockSpec((rows, tm), lambda i: (0, i))
    return pl.pallas_call(
        _route_kernel,
        grid=(n_tok // tm,),
        in_specs=[tok(N_EXPERTS), _const_spec(N_EXPERTS, 1)],
        out_specs=[tok(TOP_K), tok(TOP_K), tok(TOP_K), _const_spec(N_EXPERTS, 128)],
        out_shape=[jax.ShapeDtypeStruct((TOP_K, n_tok), I32),
                   jax.ShapeDtypeStruct((TOP_K, n_tok), F32),
                   jax.ShapeDtypeStruct((TOP_K, n_tok), I32),
                   jax.ShapeDtypeStruct((N_EXPERTS, 128), F32)],
        scratch_shapes=[pltpu.VMEM((N_EXPERTS, 128), F32)],
        compiler_params=_cparams(("arbitrary",), 32),
        name="route",
    )(logits_t, router_bias.astype(F32).reshape(N_EXPERTS, 1))


def _moe_kernel(be_ref, na_ref, src_ref, srcn_ref, dst_ref, h_hbm, wg_ref, wu_ref, wd_ref, y_hbm,
                xb, yb, wgb, wub, wdb, sem_in, sem_out):
    i = pl.program_id(0)
    n_steps = pl.num_programs(0)
    n_act = na_ref[0]
    cur = i % 2
    oth = 1 - cur
    bm = xb.shape[2]

    def tile(hbm, row):
        return hbm.at[pl.ds(pl.multiple_of(row * LANE_BLKS, LANE_BLKS), LANE_BLKS), :]

    def gather(idx_ref, s):
        for r in range(bm):
            pltpu.make_async_copy(tile(h_hbm, idx_ref[0, 0, r]), xb.at[s, :, r, :],
                                  sem_in.at[s]).start(priority=r % 2)

    def scatter(idx_ref, s):
        for r in range(bm):
            pltpu.make_async_copy(yb.at[s, :, r, :], tile(y_hbm, idx_ref[0, 0, r]),
                                  sem_out.at[s]).start(priority=r % 2)

    def wait_gather(s):
        for b in range(LANE_BLKS):
            pltpu.make_async_copy(h_hbm.at[pl.ds(0, bm), :], xb.at[s, b], sem_in.at[s]).wait()

    def wait_scatter(s):
        for b in range(LANE_BLKS):
            pltpu.make_async_copy(yb.at[s, b], y_hbm.at[pl.ds(0, bm), :], sem_out.at[s]).wait()

    @pl.when(i == 0)
    def _():
        gather(src_ref, 0)

    @pl.when(i < n_act)
    def _():
        wait_gather(cur)

        @pl.when(i >= 2)
        def _():
            wait_scatter(cur)

        @pl.when(jnp.logical_or(i == 0, be_ref[i] != be_ref[jnp.maximum(i - 1, 0)]))
        def _():
            wgb[...] = wg_ref[...].astype(BF16)
            wub[...] = wu_ref[...].astype(BF16)
            wdb[...] = wd_ref[...].astype(BF16)

        gather(srcn_ref, oth)
        xs = xb[cur]
        x = _unpack_bf16(jnp.concatenate([xs[b] for b in range(LANE_BLKS)], axis=1))
        hid = (_silu(_dot(x, wgb[...])) * _dot(x, wub[...])).astype(BF16)
        packed = _pack_halves(_dot(hid, wdb[...]))
        for b in range(LANE_BLKS):
            yb[cur, b] = packed[:, b * 128:(b + 1) * 128]
        scatter(dst_ref, cur)

    @pl.when(i == n_steps - 1)
    def _():
        wait_gather(n_act % 2)
        wait_scatter((n_act + 1) % 2)

        @pl.when(n_act >= 2)
        def _():
            wait_scatter(n_act % 2)


def _moe_experts(layer, h2p, blk_exp, n_act, src, dst, w_gate, w_up, w_down):
    n_blocks = blk_exp.shape[0]
    bm = MOE_BM
    smem_blk = lambda fn: pl.BlockSpec((1, 1, bm), fn, memory_space=pltpu.SMEM)
    wspec = lambda a, b: pl.BlockSpec((None, None, a, b), lambda i, be, na: (layer, be[i], 0, 0))
    grid_spec = pltpu.PrefetchScalarGridSpec(
        num_scalar_prefetch=2,
        grid=(n_blocks,),
        in_specs=[smem_blk(lambda i, be, na: (i, 0, 0)),
                  smem_blk(lambda i, be, na: (jnp.minimum(i + 1, n_blocks - 1), 0, 0)),
                  smem_blk(lambda i, be, na: (i, 0, 0)),
                  pl.BlockSpec(memory_space=pl.ANY),
                  wspec(D, EXPERT_FF), wspec(D, EXPERT_FF), wspec(EXPERT_FF, D)],
        out_specs=pl.BlockSpec(memory_space=pl.ANY),
        scratch_shapes=[pltpu.VMEM((2, LANE_BLKS, bm, 128), U32), pltpu.VMEM((2, LANE_BLKS, bm, 128), U32),
                        pltpu.VMEM((D, EXPERT_FF), BF16), pltpu.VMEM((D, EXPERT_FF), BF16),
                        pltpu.VMEM((EXPERT_FF, D), BF16),
                        pltpu.SemaphoreType.DMA((2,)), pltpu.SemaphoreType.DMA((2,))],
    )
    src3 = src.reshape(n_blocks, 1, bm)
    return pl.pallas_call(
        _moe_kernel,
        grid_spec=grid_spec,
        out_shape=jax.ShapeDtypeStruct((n_blocks * bm * LANE_BLKS, 128), U32),
        compiler_params=_cparams(("arbitrary",), 48),
        name="moe_experts",
    )(blk_exp, n_act, src3, src3, dst.reshape(n_blocks, 1, bm), h2p, w_gate, w_up, w_down)


def _shared_kernel(h_ref, wg_ref, wu_ref, wd_ref, o_ref):
    x = _unpack_bf16(_load_tile_rows(h_ref, o_ref.shape[0]))
    hid = (_silu(_dot(x, wg_ref[...])) * _dot(x, wu_ref[...])).astype(BF16)
    o_ref[...] = _dot(hid, wd_ref[...])


def _shared_expert(h2p, wg, wu, wd, tm):
    n_rows = h2p.shape[0] // LANE_BLKS
    return pl.pallas_call(
        _shared_kernel,
        grid=(n_rows // tm,),
        in_specs=[pl.BlockSpec((tm * LANE_BLKS, 128), lambda i: (i, 0)),
                  _const_spec(D, EXPERT_FF), _const_spec(D, EXPERT_FF), _const_spec(EXPERT_FF, D)],
        out_specs=pl.BlockSpec((tm, D), lambda i: (i, 0)),
        out_shape=jax.ShapeDtypeStruct((n_rows, D), F32),
        compiler_params=_cparams(("arbitrary",), 40),
        name="shared_expert",
    )(h2p, wg, wu, wd)


def _combine_kernel(*refs, final):
    x_ref, sh_ref, gt_ref, g2_ref = refs[:4]
    y_refs = refs[4:4 + TOP_K]
    fg_ref = refs[4 + TOP_K] if final else None
    o_ref = refs[-1]
    tm = x_ref.shape[0]
    gt = gt_ref[...]
    for b in range(LANE_BLKS):
        c_lo = slice(b * 128, (b + 1) * 128)
        c_hi = slice(HALF + b * 128, HALF + (b + 1) * 128)
        acc_lo = sh_ref[:, c_lo]
        acc_hi = sh_ref[:, c_hi]
        for k in range(TOP_K):
            lo, hi = _unpack_halves(y_refs[k][pl.ds(b, tm, stride=LANE_BLKS), :])
            acc_lo = acc_lo + lo * gt[:, k:k + 1]
            acc_hi = acc_hi + hi * gt[:, k:k + 1]
        o_ref[:, c_lo] = x_ref[:, c_lo] + g2_ref[:, c_lo] * acc_lo
        o_ref[:, c_hi] = x_ref[:, c_hi] + g2_ref[:, c_hi] * acc_hi
    if final:
        xn = o_ref[...]
        o_ref[...] = xn * lax.rsqrt(jnp.mean(xn * xn, axis=-1, keepdims=True) + EPS) * fg_ref[...]


def _combine(x, shared, gates, mod_l, y_tk, n_tok, final_g, tm):
    nt = n_tok // tm
    group_fn = _group_fn(tm)
    row = pl.BlockSpec((tm, D), lambda i: (i, 0))
    in_specs = [row, row, pl.BlockSpec((tm, TOP_K), lambda i: (i, 0)), _mod_spec(5, group_fn)]
    in_specs += [pl.BlockSpec((tm * LANE_BLKS, 128), functools.partial(lambda i, k: (k * nt + i, 0), k=k))
                 for k in range(TOP_K)]
    args = [x, shared, gates, mod_l] + [y_tk] * TOP_K
    final = final_g is not None
    if final:
        in_specs.append(_const_spec(1, D))
        args.append(final_g)
    return pl.pallas_call(
        functools.partial(_combine_kernel, final=final),
        grid=(nt,),
        in_specs=in_specs,
        out_specs=row,
        out_shape=jax.ShapeDtypeStruct((n_tok, D), F32),
        compiler_params=_cparams(("arbitrary",), 40),
        name="moe_combine",
    )(*args)


def _dispatch_plan(idx_t, rank_t, cnt, n_tok):
    bm = MOE_BM
    n_assign = n_tok * TOP_K
    n_blocks = n_assign // bm + N_EXPERTS
    n_rows = n_blocks * bm
    sizes = cnt[:, 0].astype(I32)
    padded = (sizes + bm - 1) // bm * bm
    pad_end = jnp.cumsum(padded)
    pad_start = pad_end - padded
    blk_first = jnp.arange(n_blocks, dtype=I32) * bm
    blk_exp = jnp.minimum(jnp.sum((pad_end[None, :] <= blk_first[:, None]).astype(I32), axis=1), N_EXPERTS - 1)
    n_act = (pad_end[-1] // bm).astype(I32).reshape(1)
    experts = jnp.arange(N_EXPERTS, dtype=I32)
    slot = jnp.sum(jnp.where(idx_t[:, :, None] == experts, pad_start, 0), axis=-1) + rank_t
    inv = jnp.zeros((n_rows,), I32).at[slot.reshape(-1)].add(jnp.arange(1, n_assign + 1, dtype=I32)) - 1
    valid = inv >= 0
    spare = n_assign + jnp.repeat(blk_exp, bm) * bm + jnp.arange(n_rows, dtype=I32) % bm
    src = jnp.where(valid, inv % n_tok, 0).astype(I32)
    dst = jnp.where(valid, inv, spare).astype(I32)
    return blk_exp, n_act, src, dst


def _moe_layer(layer, x, h2p, logits_t, router_bias, mod_l, exp_w, shared_w, final_g):
    n_tok = h2p.shape[0] // LANE_BLKS
    idx_t, gates_t, rank_t, cnt = _route(logits_t, router_bias, 512)
    blk_exp, n_act, src, dst = _dispatch_plan(idx_t, rank_t, cnt, n_tok)
    y_tk = _moe_experts(layer, h2p, blk_exp, n_act, src, dst, *exp_w)
    shared = _shared_expert(h2p, *[w[layer].astype(BF16) for w in shared_w], 512)
    return _combine(x, shared, gates_t.T, mod_l, y_tk, n_tok, final_g, 128)


def _rope_tables():
    rows = SEQ // GRID_W
    row = jnp.broadcast_to(jnp.arange(rows)[:, None], (rows, GRID_W)).reshape(-1).astype(F32)
    col = jnp.broadcast_to(jnp.arange(GRID_W)[None, :], (rows, GRID_W)).reshape(-1).astype(F32)
    inv = ROPE_BASE ** (-jnp.arange(0, AXIS_DIM, 2, dtype=F32) / AXIS_DIM)
    ang_r = row[:, None] * inv
    ang_c = col[:, None] * inv
    ang = jnp.concatenate([ang_r, ang_r, ang_c, ang_c], axis=-1)
    cos, sin = jnp.cos(ang), jnp.sin(ang)
    sign = jnp.where((jnp.arange(HEAD_DIM) % AXIS_DIM) < AXIS_DIM // 2, -1.0, 1.0).astype(F32)
    ident = 512
    cos_t = jnp.concatenate([jnp.tile(cos, (1, 2)), jnp.ones((ident, 128), F32)], axis=0)
    sin_t = jnp.concatenate([jnp.tile(sin * sign, (1, 2)), jnp.zeros((ident, 128), F32)], axis=0)
    return cos_t, sin_t


def _head_perm():
    g = jnp.arange(Q_PER_KV)[:, None, None]
    kvh = jnp.arange(N_KV_HEADS)[None, :, None]
    d = jnp.arange(HEAD_DIM)[None, None, :]
    return ((kvh * Q_PER_KV + g) * HEAD_DIM + d).reshape(-1)


def kernel(x, c, ctx, c_ctx, ada_w, ada_b, norm1_g, norm2_g, even_w_in, gmlp_ln_g, gmlp_ln_b, gmlp_ws, gmlp_bs, pool_w, pool_scale, even_w_out, odd_w_in, conv_w, conv_b, conv_ln_g, conv_ln_b, attn_sink, odd_w_out, router_w, router_bias, exp_w_gate, exp_w_up, exp_w_down, shared_w_gate, shared_w_up, shared_w_down, final_g):
    mod = _ada_mod(c, c_ctx, ada_w, ada_b)
    row = lambda a: a.reshape(1, -1)
    exp_w = (exp_w_gate, exp_w_up, exp_w_down)
    shared_w = (shared_w_gate, shared_w_up, shared_w_down)

    def router_split(i):
        w = router_w[i].T
        hi = w.astype(BF16)
        return hi, (w - hi.astype(F32)).astype(BF16)

    x0 = jnp.concatenate([x.reshape(N_LAT, D), ctx.reshape(N_CTX, D)], axis=0)

    uv, z = _even_in(x0, mod[0], row(norm1_g[0]), even_w_in[0].astype(BF16), 512)
    y = _even_mix(uv, z, row(gmlp_ln_g[0]), row(gmlp_ln_b[0]), gmlp_ws[0].astype(BF16),
                  gmlp_bs[0].reshape(A_GROUPS, CHUNK, 1), pool_w[0].astype(BF16), row(pool_scale[0]), 256)
    x1, h2p, lg = _mix_out(y, 0, y, 1, even_w_out[0].astype(BF16).reshape(2, HALF, D), x0, N_ALL, mod[0],
                           row(norm2_g[0]), *router_split(0), 256)
    x1 = _moe_layer(0, x1, h2p, lg, router_bias[0], mod[0], exp_w, shared_w, None)

    perm = _head_perm()
    w_in1 = odd_w_in[0]
    w_main = jnp.concatenate([w_in1[:, :2 * HALF], w_in1[:, 2 * HALF:3 * HALF][:, perm]], axis=1).astype(BF16)
    w_kv = w_in1[:, 3 * HALF:].astype(BF16)
    w_out1 = odd_w_out[0]
    w_out1 = jnp.stack([w_out1[:HALF], w_out1[HALF:][perm]], axis=0).astype(BF16)
    cos_t, sin_t = _rope_tables()
    zc, q, k, v = _odd_in(x1, mod[1], row(norm1_g[1]), w_main, w_kv, cos_t, sin_t, 512)
    y_conv = _conv_module(zc, N_LAT, conv_w[0], row(conv_b[0]), row(conv_ln_g[0]), row(conv_ln_b[0]), 256)
    y_attn = _attention(q, k, v, attn_sink[0].astype(F32))
    x2, h2p, lg = _mix_out(y_conv, 0, y_attn, 0, w_out1, x1, N_LAT, mod[1], row(norm2_g[1]),
                           *router_split(1), 256)
    out = _moe_layer(1, x2, h2p, lg, router_bias[1], mod[1], exp_w, shared_w, row(final_g))
    return out.reshape(BATCH, SEQ, D)
```

```python
import functools

import jax
import jax.numpy as jnp
from jax import lax
from jax.experimental import pallas as pl
from jax.experimental.pallas import tpu as pltpu

F32 = jnp.float32
BF16 = jnp.bfloat16
U32 = jnp.uint32
I32 = jnp.int32

D = 2048
BATCH = 4
SEQ = 4096
DEPTH = 2
GRID_W = 64
CTX_LEN = 256
HALF = D // 2
CHUNK = 128
A_GROUPS = 4
A_GW = HALF // A_GROUPS
POOL_WINDOWS = (2, 4, 8, 16)
B_GW = HALF // len(POOL_WINDOWS)
CONV_W = 31
HEAD_DIM = 64
N_Q_HEADS = HALF // HEAD_DIM
N_KV_HEADS = 2
Q_PER_KV = N_Q_HEADS // N_KV_HEADS
KV_W = N_KV_HEADS * HEAD_DIM
ATT_BLK = 128
WINDOW = 128
AXIS_DIM = HEAD_DIM // 2
ROPE_BASE = 10000.0
N_EXPERTS = 64
N_EXPERT_GROUPS = 8
PER_GROUP = N_EXPERTS // N_EXPERT_GROUPS
TOPK_GROUPS = 4
TOP_K = 8
EXPERT_FF = 512
ROUTED_SCALE = 2.5
EPS = 1e-6

N_LAT = BATCH * SEQ
N_CTX = BATCH * CTX_LEN
N_ALL = N_LAT + N_CTX
CTX_GROUP = BATCH
HALO = 128
CONV_HALO = 16
MOE_BM = 256

MIB = 1024 * 1024


def _cparams(sem, vmem_mib):
    return pltpu.CompilerParams(dimension_semantics=sem, vmem_limit_bytes=vmem_mib * MIB)


def _dot(a, b):
    return jnp.dot(a, b, preferred_element_type=F32)


def _dot_nt(a, b):
    return lax.dot_general(a, b, (((1,), (1,)), ((), ())), preferred_element_type=F32)


def _rms_mod(x, g, sh, sc):
    y = x * lax.rsqrt(jnp.mean(x * x, axis=-1, keepdims=True) + EPS) * g
    return y * (1.0 + sc) + sh


def _layer_norm(x, g, b):
    mu = jnp.mean(x, axis=-1, keepdims=True)
    xc = x - mu
    var = jnp.mean(xc * xc, axis=-1, keepdims=True)
    return xc * lax.rsqrt(var + EPS) * g + b


def _gelu(x):
    return 0.5 * x * (1.0 + lax.erf(x * (2.0 ** -0.5)))


def _silu(x):
    return x * jax.nn.sigmoid(x)


def _split_bf16(x):
    hi = x.astype(BF16)
    lo = (x - hi.astype(F32)).astype(BF16)
    return hi, lo


def _pack_halves(y):
    n = y.shape[1] // 2
    lo = lax.bitcast_convert_type(y[:, :n].astype(BF16).astype(F32), U32) >> 16
    hi = lax.bitcast_convert_type(y[:, n:].astype(BF16).astype(F32), U32) & U32(0xFFFF0000)
    return hi | lo


def _unpack_halves(p):
    lo = lax.bitcast_convert_type(p << 16, F32)
    hi = lax.bitcast_convert_type(p & U32(0xFFFF0000), F32)
    return lo, hi


def _unpack_bf16(p):
    lo, hi = _unpack_halves(p)
    return jnp.concatenate([lo.astype(BF16), hi.astype(BF16)], axis=1)


LANE_BLKS = HALF // 128


def _store_tile_rows(ref, packed):
    tm = packed.shape[0]
    for s in range(LANE_BLKS):
        ref[pl.ds(s, tm, stride=LANE_BLKS), :] = packed[:, s * 128:(s + 1) * 128]


def _load_tile_rows(ref, tm):
    return jnp.concatenate([ref[pl.ds(s, tm, stride=LANE_BLKS), :] for s in range(LANE_BLKS)], axis=1)


def _mod_spec(chunk, group_fn):
    return pl.BlockSpec((None, None, 1, D), lambda i, *_: (group_fn(i), chunk, 0, 0))


def _group_fn(tm):
    return lambda i: jnp.minimum(i // (SEQ // tm), CTX_GROUP)


def _const_spec(*shape):
    return pl.BlockSpec(shape, lambda *_: (0,) * len(shape))


def _ada_kernel(c_ref, w_ref, b_ref, o_ref):
    s = _silu(c_ref[...]).astype(BF16)
    o_ref[...] = _dot(s, w_ref[...].astype(BF16)) + b_ref[...]


def _ada_mod(c, c_ctx, ada_w, ada_b):
    tn = 1024
    cv = jnp.zeros((8, D), F32).at[:BATCH].set(c).at[CTX_GROUP].set(c_ctx)
    out = pl.pallas_call(
        _ada_kernel,
        grid=(DEPTH, 6 * D // tn),
        in_specs=[pl.BlockSpec((8, D), lambda l, j: (0, 0)),
                  pl.BlockSpec((None, D, tn), lambda l, j: (l, 0, j)),
                  pl.BlockSpec((None, 1, tn), lambda l, j: (l, 0, j))],
        out_specs=pl.BlockSpec((None, 8, tn), lambda l, j: (l, 0, j)),
        out_shape=jax.ShapeDtypeStruct((DEPTH, 8, 6 * D), F32),
        compiler_params=_cparams(("arbitrary", "arbitrary"), 40),
        name="ada_mod",
    )(cv, ada_w, ada_b.reshape(DEPTH, 1, 6 * D))
    return out.reshape(DEPTH, 8, 6, 1, D)


def _even_in_kernel(x_ref, g_ref, sh_ref, sc_ref, w_ref, uv_ref, z_ref, h_ref):
    h_ref[...] = _rms_mod(x_ref[...], g_ref[...], sh_ref[...], sc_ref[...]).astype(BF16)
    for j in range(2):
        cs = slice(j * HALF, (j + 1) * HALF)
        uv_ref[:, cs] = _gelu(_dot(h_ref[...], w_ref[:, cs])).astype(BF16)
    z_ref[...] = _dot(h_ref[...], w_ref[:, 2 * HALF:])


def _resident_spec(*shape):
    return pl.BlockSpec(shape, lambda *_: (0,) * len(shape), pipeline_mode=pl.Buffered(1))


def _even_in(x, mod_l, norm_g, w_in, tm):
    n_rows = x.shape[0]
    group_fn = _group_fn(tm)
    return pl.pallas_call(
        _even_in_kernel,
        grid=(n_rows // tm,),
        in_specs=[pl.BlockSpec((tm, D), lambda i: (i, 0)),
                  _const_spec(1, D),
                  _mod_spec(0, group_fn), _mod_spec(1, group_fn),
                  _resident_spec(D, 3 * HALF)],
        out_specs=[pl.BlockSpec((tm, 2 * HALF), lambda i: (i, 0)),
                   pl.BlockSpec((tm, HALF), lambda i: (i, 0))],
        out_shape=[jax.ShapeDtypeStruct((n_rows, 2 * HALF), BF16),
                   jax.ShapeDtypeStruct((n_rows, HALF), F32)],
        scratch_shapes=[pltpu.VMEM((tm, D), BF16)],
        compiler_params=_cparams(("arbitrary",), 48),
        name="even_in",
    )(x, norm_g, mod_l, mod_l, w_in)


def _band(d, w):
    inside = lax.bitcast_convert_type(d + w // 2, U32) < U32(w)
    return jnp.where(inside, 1.0, 0.0).astype(BF16)


def _seq_tile(i, tm):
    lat_tiles = N_LAT // tm
    is_lat = i < lat_tiles
    it = i % (SEQ // tm)
    first = jnp.logical_or(jnp.logical_not(is_lat), it == 0)
    last = jnp.logical_or(jnp.logical_not(is_lat), it == SEQ // tm - 1)
    pos0 = jnp.where(is_lat, it * tm, 0)
    seq_len = jnp.where(is_lat, SEQ, CTX_LEN)
    return first, last, pos0, seq_len


def _even_mix_kernel(u_ref, v_ref, z_ref, zp_ref, zn_ref, lng_ref, lnb_ref, ws_ref, bs_ref,
                     wp_ref, ps_ref, y_ref, *, tm):
    first, last, pos0, seq_len = _seq_tile(pl.program_id(0), tm)
    vn = _layer_norm(v_ref[...].astype(F32), lng_ref[...], lnb_ref[...]).astype(BF16)
    for g in range(A_GROUPS):
        cs = slice(g * A_GW, (g + 1) * A_GW)
        for c in range(tm // CHUNK):
            rs = slice(c * CHUNK, (c + 1) * CHUNK)
            mixed = _dot(ws_ref[g], vn[rs, cs]) + bs_ref[g]
            y_ref[rs, cs] = (u_ref[rs, cs].astype(F32) * mixed).astype(BF16)
    z = z_ref[...]
    zp = jnp.where(first, 0.0, zp_ref[...])
    zn = jnp.where(last, 0.0, zn_ref[...])
    z_hi, z_lo = _split_bf16(z)
    zp_hi, zp_lo = _split_bf16(zp)
    zn_hi, zn_lo = _split_bf16(zn)
    d_main = (lax.broadcasted_iota(I32, (tm, tm), 1) - lax.broadcasted_iota(I32, (tm, tm), 0))
    d_halo = (lax.broadcasted_iota(I32, (tm, HALO), 1) - lax.broadcasted_iota(I32, (tm, HALO), 0))
    pos = pos0 + lax.broadcasted_iota(I32, (tm, 1), 0)
    for g, w in enumerate(POOL_WINDOWS):
        cs = slice(g * B_GW, (g + 1) * B_GW)
        bm_ = _band(d_main, w)
        bp = _band(d_halo - HALO, w)
        bn = _band(d_halo + tm, w)
        tot = (_dot(bm_, z_hi[:, cs]) + _dot(bm_, z_lo[:, cs])
               + _dot(bp, zp_hi[:, cs]) + _dot(bp, zp_lo[:, cs])
               + _dot(bn, zn_hi[:, cs]) + _dot(bn, zn_lo[:, cs]))
        cnt = (jnp.minimum(pos + w // 2, seq_len) - jnp.maximum(pos - w // 2, 0)).astype(F32)
        pooled = (tot / cnt - z[:, cs]).astype(BF16)
        y_ref[:, HALF + g * B_GW:HALF + (g + 1) * B_GW] = (
            _dot(pooled, wp_ref[g]) * ps_ref[:, cs]).astype(BF16)


def _even_mix(uv, z, ln_g, ln_b, ws, bs, wp, ps, tm):
    n_rows = z.shape[0]
    hb = tm // HALO
    n_hblk = n_rows // HALO
    return pl.pallas_call(
        functools.partial(_even_mix_kernel, tm=tm),
        grid=(n_rows // tm,),
        in_specs=[pl.BlockSpec((tm, HALF), lambda i: (i, 0)),
                  pl.BlockSpec((tm, HALF), lambda i: (i, 1)),
                  pl.BlockSpec((tm, HALF), lambda i: (i, 0)),
                  pl.BlockSpec((HALO, HALF), lambda i: (jnp.maximum(i * hb - 1, 0), 0)),
                  pl.BlockSpec((HALO, HALF), lambda i: (jnp.minimum((i + 1) * hb, n_hblk - 1), 0)),
                  _const_spec(1, HALF), _const_spec(1, HALF),
                  _const_spec(A_GROUPS, CHUNK, CHUNK), _const_spec(A_GROUPS, CHUNK, 1),
                  _const_spec(len(POOL_WINDOWS), B_GW, B_GW), _const_spec(1, HALF)],
        out_specs=pl.BlockSpec((tm, D), lambda i: (i, 0)),
        out_shape=jax.ShapeDtypeStruct((n_rows, D), BF16),
        compiler_params=_cparams(("arbitrary",), 40),
        name="even_mix",
    )(uv, uv, z, z, z, ln_g, ln_b, ws, bs, wp, ps)


def _mix_out_kernel(ya_ref, yb_ref, w_ref, x_ref, g1_ref, n2_ref, sh_ref, sc_ref, rw_ref,
                    xo_ref, hp_ref, lg_ref):
    o = _dot(ya_ref[...], w_ref[0]) + _dot(yb_ref[...], w_ref[1])
    xn = x_ref[...] + g1_ref[...] * o
    xo_ref[...] = xn
    h = _rms_mod(xn, n2_ref[...], sh_ref[...], sc_ref[...])
    _store_tile_rows(hp_ref, _pack_halves(h))
    lg_ref[...] = _dot_nt(rw_ref[...], h.astype(BF16))


def _mix_out(ya, ya_col, yb, yb_col, w_out, x, n_rows, mod_l, norm2_g, rw, tm):
    group_fn = _group_fn(tm)
    return pl.pallas_call(
        _mix_out_kernel,
        grid=(n_rows // tm,),
        in_specs=[pl.BlockSpec((tm, HALF), lambda i: (i, ya_col)),
                  pl.BlockSpec((tm, HALF), lambda i: (i, yb_col)),
                  _resident_spec(2, HALF, D),
                  pl.BlockSpec((tm, D), lambda i: (i, 0)),
                  _mod_spec(2, group_fn), _const_spec(1, D), _mod_spec(3, group_fn), _mod_spec(4, group_fn),
                  _const_spec(N_EXPERTS, D)],
        out_specs=[pl.BlockSpec((tm, D), lambda i: (i, 0)),
                   pl.BlockSpec((tm * LANE_BLKS, 128), lambda i: (i, 0)),
                   pl.BlockSpec((N_EXPERTS, tm), lambda i: (0, i))],
        out_shape=[jax.ShapeDtypeStruct((n_rows, D), F32),
                   jax.ShapeDtypeStruct((n_rows * LANE_BLKS, 128), U32),
                   jax.ShapeDtypeStruct((N_EXPERTS, n_rows), F32)],
        compiler_params=_cparams(("arbitrary",), 48),
        name="mix_out",
    )(ya, yb, w_out, x, mod_l, norm2_g, mod_l, mod_l, rw)


def _rope(x, cos, sin_signed, first_half):
    partner = jnp.where(first_half, pltpu.roll(x, 128 - AXIS_DIM // 2, 1), pltpu.roll(x, AXIS_DIM // 2, 1))
    return x * cos + partner * sin_signed


def _odd_in_kernel(x_ref, g_ref, sh_ref, sc_ref, w_ref, wkv_ref, cos_ref, sin_ref,
                   zc_ref, q_ref, k_ref, v_ref, h_ref):
    tm = x_ref.shape[0]
    h_ref[...] = _rms_mod(x_ref[...], g_ref[...], sh_ref[...], sc_ref[...]).astype(BF16)
    first_half = (lax.broadcasted_iota(I32, (tm, 128), 1) % AXIS_DIM) < (AXIS_DIM // 2)
    cos = cos_ref[...]
    sin = sin_ref[...]
    for b in range(HALF // 256):
        a = _dot(h_ref[...], w_ref[:, b * 256:(b + 1) * 256])
        gate = _dot(h_ref[...], w_ref[:, HALF + b * 256:HALF + (b + 1) * 256])
        zc_ref[:, b * 256:(b + 1) * 256] = a * jax.nn.sigmoid(gate)
    for b in range(HALF // 256):
        q = _dot(h_ref[...], w_ref[:, 2 * HALF + b * 256:2 * HALF + (b + 1) * 256])
        for s in range(2):
            cs = slice(b * 256 + s * 128, b * 256 + (s + 1) * 128)
            q_ref[:, cs] = (_rope(q[:, s * 128:(s + 1) * 128], cos, sin, first_half)
                            * (HEAD_DIM ** -0.5)).astype(BF16)
    kv = _dot(h_ref[...], wkv_ref[...])
    k_ref[...] = _rope(kv[:, :KV_W], cos, sin, first_half).astype(BF16)
    v_ref[...] = kv[:, KV_W:].astype(BF16)


def _odd_in(x, mod_l, norm_g, w_main, w_kv, cos_t, sin_t, tm):
    n_rows = x.shape[0]
    group_fn = _group_fn(tm)
    lat_tiles = N_LAT // tm
    pos_blk = lambda i: (jnp.where(i < lat_tiles, i % (SEQ // tm), SEQ // tm), 0)
    row = lambda w: pl.BlockSpec((tm, w), lambda i: (i, 0))
    return pl.pallas_call(
        _odd_in_kernel,
        grid=(n_rows // tm,),
        in_specs=[pl.BlockSpec((tm, D), lambda i: (i, 0)),
                  _const_spec(1, D),
                  _mod_spec(0, group_fn), _mod_spec(1, group_fn),
                  _resident_spec(D, 3 * HALF),
                  _resident_spec(D, 2 * KV_W),
                  pl.BlockSpec((tm, 128), pos_blk), pl.BlockSpec((tm, 128), pos_blk)],
        out_specs=[row(HALF), row(HALF), row(KV_W), row(KV_W)],
        out_shape=[jax.ShapeDtypeStruct((n_rows, HALF), F32),
                   jax.ShapeDtypeStruct((n_rows, HALF), BF16),
                   jax.ShapeDtypeStruct((n_rows, KV_W), BF16),
                   jax.ShapeDtypeStruct((n_rows, KV_W), BF16)],
        scratch_shapes=[pltpu.VMEM((tm, D), BF16)],
        compiler_params=_cparams(("arbitrary",), 48),
        name="odd_in",
    )(x, norm_g, mod_l, mod_l, w_main, w_kv, cos_t, sin_t)


def _conv_kernel(z_ref, zp_ref, zn_ref, w_ref, b_ref, lng_ref, lnb_ref, y_ref, ze_ref, zs_ref, c_ref, *, tm):
    first, last, _, _ = _seq_tile(pl.program_id(0), tm)
    ze_ref[0:CONV_HALO, :] = jnp.where(first, 0.0, zp_ref[...])
    ze_ref[CONV_HALO:CONV_HALO + tm, :] = z_ref[...]
    ze_ref[CONV_HALO + tm:, :] = jnp.where(last, 0.0, zn_ref[...])
    rc = 64
    base = CONV_HALO - CONV_W // 2
    n_sh = zs_ref.shape[1]
    for b in range(8):
        zs_ref[b] = ze_ref[b:b + n_sh, :]

    def lane_block(cb, _):
        cs = pl.ds(pl.multiple_of(cb * 128, 128), 128)
        for r in range(tm // rc):
            acc = jnp.zeros((rc, 128), F32)
            for t in range(CONV_W):
                off = base + t
                start = r * rc + (off // 8) * 8
                acc = acc + w_ref[t:t + 1, cs] * zs_ref[off % 8, start:start + rc, cs]
            c_ref[r * rc:(r + 1) * rc, cs] = acc
        return 0

    lax.fori_loop(0, HALF // 128, lane_block, 0)
    y = _layer_norm(c_ref[...] + b_ref[...], lng_ref[...], lnb_ref[...])
    y_ref[...] = _silu(y).astype(BF16)


def _conv_module(zc, n_rows, conv_w, conv_b, ln_g, ln_b, tm):
    hb = tm // CONV_HALO
    n_hblk = zc.shape[0] // CONV_HALO
    return pl.pallas_call(
        functools.partial(_conv_kernel, tm=tm),
        grid=(n_rows // tm,),
        in_specs=[pl.BlockSpec((tm, HALF), lambda i: (i, 0)),
                  pl.BlockSpec((CONV_HALO, HALF), lambda i: (jnp.maximum(i * hb - 1, 0), 0)),
                  pl.BlockSpec((CONV_HALO, HALF), lambda i: (jnp.minimum((i + 1) * hb, n_hblk - 1), 0)),
                  _const_spec(CONV_W, HALF), _const_spec(1, HALF), _const_spec(1, HALF), _const_spec(1, HALF)],
        out_specs=pl.BlockSpec((tm, HALF), lambda i: (i, 0)),
        out_shape=jax.ShapeDtypeStruct((n_rows, HALF), BF16),
        scratch_shapes=[pltpu.VMEM((tm + 2 * CONV_HALO, HALF), F32),
                        pltpu.VMEM((8, tm + 2 * CONV_HALO - 8, HALF), F32),
                        pltpu.VMEM((tm, HALF), F32)],
        compiler_params=_cparams(("arbitrary",), 40),
        name="conv_module",
    )(zc, zc, zc, conv_w, conv_b, ln_g, ln_b)


def _attn_kernel(sink_ref, q_ref, kp_ref, kc_ref, kn_ref, vp_ref, vc_ref, vn_ref, kx_ref, vx_ref, o_ref):
    i = pl.program_id(1)
    n_keys = 3 * ATT_BLK + CTX_LEN
    kb = jnp.concatenate([kp_ref[...], kc_ref[...], kn_ref[...], kx_ref[...]], axis=0)
    vb = jnp.concatenate([vp_ref[...], vc_ref[...], vn_ref[...], vx_ref[...]], axis=0)
    key_lane = lax.broadcasted_iota(I32, (n_keys, 2 * HEAD_DIM), 1)
    k_head = [jnp.where(key_lane < HEAD_DIM, kb, jnp.zeros_like(kb)),
              jnp.where(key_lane >= HEAD_DIM, kb, jnp.zeros_like(kb))]
    r = lax.broadcasted_iota(I32, (ATT_BLK, n_keys), 0)
    c = lax.broadcasted_iota(I32, (ATT_BLK, n_keys), 1)
    kpos = c + (i - 1) * ATT_BLK
    band_ok = lax.bitcast_convert_type(c - r, U32) <= U32(2 * WINDOW)
    in_seq = lax.bitcast_convert_type(kpos, U32) < U32(SEQ)
    bias = jnp.where(c >= 3 * ATT_BLK, 0.0, jnp.where(band_ok, jnp.where(in_seq, 0.0, -jnp.inf), -jnp.inf))
    gs = 4
    rows = gs * ATT_BLK
    out_lane = lax.broadcasted_iota(I32, (rows, 2 * HEAD_DIM), 1)
    bias_s = jnp.concatenate([bias] * gs, axis=0)
    for g0 in range(0, Q_PER_KV, gs):
        qs = jnp.concatenate([q_ref[:, g * 128:(g + 1) * 128] for g in range(g0, g0 + gs)], axis=0)
        outs = []
        for kvh in range(N_KV_HEADS):
            sk = jnp.concatenate([jnp.full((ATT_BLK, 1), sink_ref[kvh * Q_PER_KV + g], F32)
                                  for g in range(g0, g0 + gs)], axis=0)
            s = _dot_nt(qs, k_head[kvh]) + bias_s
            m = jnp.maximum(jnp.max(s, axis=-1, keepdims=True), sk)
            e = jnp.exp(s - m)
            den = jnp.sum(e, axis=-1, keepdims=True) + jnp.exp(sk - m)
            outs.append(_dot(e.astype(BF16), vb) / den)
        o = jnp.where(out_lane < HEAD_DIM, outs[0], outs[1]).astype(BF16)
        for j in range(gs):
            o_ref[:, (g0 + j) * 128:(g0 + j + 1) * 128] = o[j * ATT_BLK:(j + 1) * ATT_BLK, :]


def _attention(q, k, v, sink):
    nb = SEQ // ATT_BLK
    ctx0 = N_LAT // CTX_LEN
    blk = lambda w, fn: pl.BlockSpec((ATT_BLK, w), fn)
    prev = lambda b, i, s: (b * nb + jnp.maximum(i - 1, 0), 0)
    cur = lambda b, i, s: (b * nb + i, 0)
    nxt = lambda b, i, s: (b * nb + jnp.minimum(i + 1, nb - 1), 0)
    ctx = pl.BlockSpec((CTX_LEN, KV_W), lambda b, i, s: (ctx0 + b, 0))
    grid_spec = pltpu.PrefetchScalarGridSpec(
        num_scalar_prefetch=1,
        grid=(BATCH, nb),
        in_specs=[blk(HALF, cur), blk(KV_W, prev), blk(KV_W, cur), blk(KV_W, nxt),
                  blk(KV_W, prev), blk(KV_W, cur), blk(KV_W, nxt), ctx, ctx],
        out_specs=blk(HALF, cur),
    )
    return pl.pallas_call(
        _attn_kernel,
        grid_spec=grid_spec,
        out_shape=jax.ShapeDtypeStruct((N_LAT, HALF), BF16),
        compiler_params=_cparams(("arbitrary", "arbitrary"), 32),
        name="attention",
    )(sink, q, k, k, k, v, v, v, k, v)


def _first_argmax(x, iota, n):
    m = jnp.max(x, axis=0, keepdims=True)
    first = jnp.min(jnp.where(x == m, iota, n), axis=0, keepdims=True)
    return m, first


def _route_kernel(lg_ref, bias_ref, idx_ref, gate_ref, rank_ref, cnt_ref, carry_ref):
    tm = lg_ref.shape[1]

    @pl.when(pl.program_id(0) == 0)
    def _():
        carry_ref[...] = jnp.zeros_like(carry_ref)

    scores = jax.nn.sigmoid(lg_ref[...])
    biased = scores + bias_ref[...]
    sub = lax.broadcasted_iota(I32, (PER_GROUP, tm), 0)
    blocks = [biased[g * PER_GROUP:(g + 1) * PER_GROUP, :] for g in range(N_EXPERT_GROUPS)]
    gs = []
    for blk in blocks:
        m1, f1 = _first_argmax(blk, sub, PER_GROUP)
        m2 = jnp.max(jnp.where(sub == f1, -jnp.inf, blk), axis=0, keepdims=True)
        gs.append(m1 + m2)
    gs = jnp.concatenate(gs, axis=0)
    giota = lax.broadcasted_iota(I32, (N_EXPERT_GROUPS, tm), 0)
    keep = jnp.zeros((N_EXPERT_GROUPS, tm), F32)
    for _ in range(TOPK_GROUPS):
        _, f = _first_argmax(gs, giota, N_EXPERT_GROUPS)
        hit = giota == f
        keep = jnp.where(hit, 1.0, keep)
        gs = jnp.where(hit, -jnp.inf, gs)
    cur = jnp.concatenate([jnp.where(keep[g:g + 1, :] > 0.0, blocks[g], -jnp.inf)
                           for g in range(N_EXPERT_GROUPS)], axis=0)
    eiota = lax.broadcasted_iota(I32, (N_EXPERTS, tm), 0)
    chosen = jnp.zeros((N_EXPERTS, tm), F32)
    idx, sel = [], []
    for _ in range(TOP_K):
        _, f = _first_argmax(cur, eiota, N_EXPERTS)
        hit = eiota == f
        idx.append(f)
        sel.append(jnp.sum(jnp.where(hit, scores, 0.0), axis=0, keepdims=True))
        cur = jnp.where(hit, -jnp.inf, cur)
        chosen = jnp.where(hit, 1.0, chosen)
    sel = jnp.concatenate(sel, axis=0)
    idx = jnp.concatenate(idx, axis=0)
    gate_ref[...] = sel / jnp.sum(sel, axis=0, keepdims=True) * ROUTED_SCALE
    idx_ref[...] = idx
    before = jnp.where(lax.broadcasted_iota(I32, (tm, tm), 0) < lax.broadcasted_iota(I32, (tm, tm), 1), 1.0, 0.0)
    rank = _dot(chosen.astype(BF16), before.astype(BF16)) + carry_ref[:, 0:1]
    rank_ref[...] = jnp.concatenate(
        [jnp.sum(jnp.where(eiota == idx[k:k + 1, :], rank, 0.0), axis=0, keepdims=True) for k in range(TOP_K)],
        axis=0).astype(I32)
    carry_ref[...] = carry_ref[...] + jnp.sum(chosen, axis=1, keepdims=True)
    cnt_ref[...] = carry_ref[...]


def _route(logits_t, router_bias, tm):
    n_tok = logits_t.shape[1]
    tok = lambda rows: pl.BlockSpec((rows, tm), lambda i: (0, i))
    return pl.pallas_call(
        _route_kernel,
        grid=(n_tok // tm,),
        in_specs=[tok(N_EXPERTS), _const_spec(N_EXPERTS, 1)],
        out_specs=[tok(TOP_K), tok(TOP_K), tok(TOP_K), _const_spec(N_EXPERTS, 128)],
        out_shape=[jax.ShapeDtypeStruct((TOP_K, n_tok), I32),
                   jax.ShapeDtypeStruct((TOP_K, n_tok), F32),
                   jax.ShapeDtypeStruct((TOP_K, n_tok), I32),
                   jax.ShapeDtypeStruct((N_EXPERTS, 128), F32)],
        scratch_shapes=[pltpu.VMEM((N_EXPERTS, 128), F32)],
        compiler_params=_cparams(("arbitrary",), 32),
        name="route",
    )(logits_t, router_bias.astype(F32).reshape(N_EXPERTS, 1))


def _moe_kernel(be_ref, na_ref, src_ref, srcn_ref, dst_ref, h_hbm, wg_ref, wu_ref, wd_ref, y_hbm,
                xb, yb, wgb, wub, wdb, sem_in, sem_out):
    i = pl.program_id(0)
    n_steps = pl.num_programs(0)
    n_act = na_ref[0]
    cur = i % 2
    oth = 1 - cur
    bm = xb.shape[2]

    def tile(hbm, row):
        return hbm.at[pl.ds(pl.multiple_of(row * LANE_BLKS, LANE_BLKS), LANE_BLKS), :]

    def gather(idx_ref, s):
        for r in range(bm):
            pltpu.make_async_copy(tile(h_hbm, idx_ref[0, 0, r]), xb.at[s, :, r, :],
                                  sem_in.at[s]).start(priority=r % 2)

    def scatter(idx_ref, s):
        for r in range(bm):
            pltpu.make_async_copy(yb.at[s, :, r, :], tile(y_hbm, idx_ref[0, 0, r]),
                                  sem_out.at[s]).start(priority=r % 2)

    def wait_gather(s):
        for b in range(LANE_BLKS):
            pltpu.make_async_copy(h_hbm.at[pl.ds(0, bm), :], xb.at[s, b], sem_in.at[s]).wait()

    def wait_scatter(s):
        for b in range(LANE_BLKS):
            pltpu.make_async_copy(yb.at[s, b], y_hbm.at[pl.ds(0, bm), :], sem_out.at[s]).wait()

    @pl.when(i == 0)
    def _():
        gather(src_ref, 0)

    @pl.when(i < n_act)
    def _():
        wait_gather(cur)

        @pl.when(i >= 2)
        def _():
            wait_scatter(cur)

        @pl.when(jnp.logical_or(i == 0, be_ref[i] != be_ref[jnp.maximum(i - 1, 0)]))
        def _():
            wgb[...] = wg_ref[...].astype(BF16)
            wub[...] = wu_ref[...].astype(BF16)
            wdb[...] = wd_ref[...].astype(BF16)

        gather(srcn_ref, oth)
        xs = xb[cur]
        x = _unpack_bf16(jnp.concatenate([xs[b] for b in range(LANE_BLKS)], axis=1))
        hid = (_silu(_dot(x, wgb[...])) * _dot(x, wub[...])).astype(BF16)
        packed = _pack_halves(_dot(hid, wdb[...]))
        for b in range(LANE_BLKS):
            yb[cur, b] = packed[:, b * 128:(b + 1) * 128]
        scatter(dst_ref, cur)

    @pl.when(i == n_steps - 1)
    def _():
        wait_gather(n_act % 2)
        wait_scatter((n_act + 1) % 2)

        @pl.when(n_act >= 2)
        def _():
            wait_scatter(n_act % 2)


def _moe_experts(layer, h2p, blk_exp, n_act, src, dst, w_gate, w_up, w_down):
    n_blocks = blk_exp.shape[0]
    bm = MOE_BM
    smem_blk = lambda fn: pl.BlockSpec((1, 1, bm), fn, memory_space=pltpu.SMEM)
    wspec = lambda a, b: pl.BlockSpec((None, None, a, b), lambda i, be, na: (layer, be[i], 0, 0))
    grid_spec = pltpu.PrefetchScalarGridSpec(
        num_scalar_prefetch=2,
        grid=(n_blocks,),
        in_specs=[smem_blk(lambda i, be, na: (i, 0, 0)),
                  smem_blk(lambda i, be, na: (jnp.minimum(i + 1, n_blocks - 1), 0, 0)),
                  smem_blk(lambda i, be, na: (i, 0, 0)),
                  pl.BlockSpec(memory_space=pl.ANY),
                  wspec(D, EXPERT_FF), wspec(D, EXPERT_FF), wspec(EXPERT_FF, D)],
        out_specs=pl.BlockSpec(memory_space=pl.ANY),
        scratch_shapes=[pltpu.VMEM((2, LANE_BLKS, bm, 128), U32), pltpu.VMEM((2, LANE_BLKS, bm, 128), U32),
                        pltpu.VMEM((D, EXPERT_FF), BF16), pltpu.VMEM((D, EXPERT_FF), BF16),
                        pltpu.VMEM((EXPERT_FF, D), BF16),
                        pltpu.SemaphoreType.DMA((2,)), pltpu.SemaphoreType.DMA((2,))],
    )
    src3 = src.reshape(n_blocks, 1, bm)
    return pl.pallas_call(
        _moe_kernel,
        grid_spec=grid_spec,
        out_shape=jax.ShapeDtypeStruct((n_blocks * bm * LANE_BLKS, 128), U32),
        compiler_params=_cparams(("arbitrary",), 48),
        name="moe_experts",
    )(blk_exp, n_act, src3, src3, dst.reshape(n_blocks, 1, bm), h2p, w_gate, w_up, w_down)


def _shared_kernel(h_ref, wg_ref, wu_ref, wd_ref, o_ref):
    x = _unpack_bf16(_load_tile_rows(h_ref, o_ref.shape[0]))
    hid = (_silu(_dot(x, wg_ref[...])) * _dot(x, wu_ref[...])).astype(BF16)
    o_ref[...] = _dot(hid, wd_ref[...])


def _shared_expert(h2p, wg, wu, wd, tm):
    n_rows = h2p.shape[0] // LANE_BLKS
    return pl.pallas_call(
        _shared_kernel,
        grid=(n_rows // tm,),
        in_specs=[pl.BlockSpec((tm * LANE_BLKS, 128), lambda i: (i, 0)),
                  _const_spec(D, EXPERT_FF), _const_spec(D, EXPERT_FF), _const_spec(EXPERT_FF, D)],
        out_specs=pl.BlockSpec((tm, D), lambda i: (i, 0)),
        out_shape=jax.ShapeDtypeStruct((n_rows, D), F32),
        compiler_params=_cparams(("arbitrary",), 40),
        name="shared_expert",
    )(h2p, wg, wu, wd)


def _combine_kernel(*refs, final):
    x_ref, sh_ref, gt_ref, g2_ref = refs[:4]
    y_refs = refs[4:4 + TOP_K]
    fg_ref = refs[4 + TOP_K] if final else None
    o_ref = refs[-1]
    tm = x_ref.shape[0]
    gt = gt_ref[...]
    for b in range(LANE_BLKS):
        c_lo = slice(b * 128, (b + 1) * 128)
        c_hi = slice(HALF + b * 128, HALF + (b + 1) * 128)
        acc_lo = sh_ref[:, c_lo]
        acc_hi = sh_ref[:, c_hi]
        for k in range(TOP_K):
            lo, hi = _unpack_halves(y_refs[k][pl.ds(b, tm, stride=LANE_BLKS), :])
            acc_lo = acc_lo + lo * gt[:, k:k + 1]
            acc_hi = acc_hi + hi * gt[:, k:k + 1]
        o_ref[:, c_lo] = x_ref[:, c_lo] + g2_ref[:, c_lo] * acc_lo
        o_ref[:, c_hi] = x_ref[:, c_hi] + g2_ref[:, c_hi] * acc_hi
    if final:
        xn = o_ref[...]
        o_ref[...] = xn * lax.rsqrt(jnp.mean(xn * xn, axis=-1, keepdims=True) + EPS) * fg_ref[...]


def _combine(x, shared, gates, mod_l, y_tk, n_tok, final_g, tm):
    nt = n_tok // tm
    group_fn = _group_fn(tm)
    row = pl.BlockSpec((tm, D), lambda i: (i, 0))
    in_specs = [row, row, pl.BlockSpec((tm, TOP_K), lambda i: (i, 0)), _mod_spec(5, group_fn)]
    in_specs += [pl.BlockSpec((tm * LANE_BLKS, 128), functools.partial(lambda i, k: (k * nt + i, 0), k=k))
                 for k in range(TOP_K)]
    args = [x, shared, gates, mod_l] + [y_tk] * TOP_K
    final = final_g is not None
    if final:
        in_specs.append(_const_spec(1, D))
        args.append(final_g)
    return pl.pallas_call(
        functools.partial(_combine_kernel, final=final),
        grid=(nt,),
        in_specs=in_specs,
        out_specs=row,
        out_shape=jax.ShapeDtypeStruct((n_tok, D), F32),
        compiler_params=_cparams(("arbitrary",), 40),
        name="moe_combine",
    )(*args)


def _dispatch_plan(idx_t, rank_t, cnt, n_tok):
    bm = MOE_BM
    n_assign = n_tok * TOP_K
    n_blocks = n_assign // bm + N_EXPERTS
    n_rows = n_blocks * bm
    sizes = cnt[:, 0].astype(I32)
    padded = (sizes + bm - 1) // bm * bm
    pad_end = jnp.cumsum(padded)
    pad_start = pad_end - padded
    blk_first = jnp.arange(n_blocks, dtype=I32) * bm
    blk_exp = jnp.minimum(jnp.sum((pad_end[None, :] <= blk_first[:, None]).astype(I32), axis=1), N_EXPERTS - 1)
    n_act = (pad_end[-1] // bm).astype(I32).reshape(1)
    experts = jnp.arange(N_EXPERTS, dtype=I32)
    slot = jnp.sum(jnp.where(idx_t[:, :, None] == experts, pad_start, 0), axis=-1) + rank_t
    inv = jnp.zeros((n_rows,), I32).at[slot.reshape(-1)].add(jnp.arange(1, n_assign + 1, dtype=I32)) - 1
    valid = inv >= 0
    spare = n_assign + jnp.repeat(blk_exp, bm) * bm + jnp.arange(n_rows, dtype=I32) % bm
    src = jnp.where(valid, inv % n_tok, 0).astype(I32)
    dst = jnp.where(valid, inv, spare).astype(I32)
    return blk_exp, n_act, src, dst


def _moe_layer(layer, x, h2p, logits_t, router_bias, mod_l, exp_w, shared_w, final_g):
    n_tok = h2p.shape[0] // LANE_BLKS
    idx_t, gates_t, rank_t, cnt = _route(logits_t, router_bias, 512)
    blk_exp, n_act, src, dst = _dispatch_plan(idx_t, rank_t, cnt, n_tok)
    y_tk = _moe_experts(layer, h2p, blk_exp, n_act, src, dst, *exp_w)
    shared = _shared_expert(h2p, *[w[layer].astype(BF16) for w in shared_w], 512)
    return _combine(x, shared, gates_t.T, mod_l, y_tk, n_tok, final_g, 128)


def _rope_tables():
    rows = SEQ // GRID_W
    row = jnp.broadcast_to(jnp.arange(rows)[:, None], (rows, GRID_W)).reshape(-1).astype(F32)
    col = jnp.broadcast_to(jnp.arange(GRID_W)[None, :], (rows, GRID_W)).reshape(-1).astype(F32)
    inv = ROPE_BASE ** (-jnp.arange(0, AXIS_DIM, 2, dtype=F32) / AXIS_DIM)
    ang_r = row[:, None] * inv
    ang_c = col[:, None] * inv
    ang = jnp.concatenate([ang_r, ang_r, ang_c, ang_c], axis=-1)
    cos, sin = jnp.cos(ang), jnp.sin(ang)
    sign = jnp.where((jnp.arange(HEAD_DIM) % AXIS_DIM) < AXIS_DIM // 2, -1.0, 1.0).astype(F32)
    ident = 512
    cos_t = jnp.concatenate([jnp.tile(cos, (1, 2)), jnp.ones((ident, 128), F32)], axis=0)
    sin_t = jnp.concatenate([jnp.tile(sin * sign, (1, 2)), jnp.zeros((ident, 128), F32)], axis=0)
    return cos_t, sin_t


def _head_perm():
    g = jnp.arange(Q_PER_KV)[:, None, None]
    kvh = jnp.arange(N_KV_HEADS)[None, :, None]
    d = jnp.arange(HEAD_DIM)[None, None, :]
    return ((kvh * Q_PER_KV + g) * HEAD_DIM + d).reshape(-1)


def kernel(x, c, ctx, c_ctx, ada_w, ada_b, norm1_g, norm2_g, even_w_in, gmlp_ln_g, gmlp_ln_b, gmlp_ws, gmlp_bs, pool_w, pool_scale, even_w_out, odd_w_in, conv_w, conv_b, conv_ln_g, conv_ln_b, attn_sink, odd_w_out, router_w, router_bias, exp_w_gate, exp_w_up, exp_w_down, shared_w_gate, shared_w_up, shared_w_down, final_g):
    mod = _ada_mod(c, c_ctx, ada_w, ada_b)
    row = lambda a: a.reshape(1, -1)
    exp_w = (exp_w_gate, exp_w_up, exp_w_down)
    shared_w = (shared_w_gate, shared_w_up, shared_w_down)

    def router_t(i):
        return router_w[i].T.astype(BF16)

    x0 = jnp.concatenate([x.reshape(N_LAT, D), ctx.reshape(N_CTX, D)], axis=0)

    uv, z = _even_in(x0, mod[0], row(norm1_g[0]), even_w_in[0].astype(BF16), 512)
    y = _even_mix(uv, z, row(gmlp_ln_g[0]), row(gmlp_ln_b[0]), gmlp_ws[0].astype(BF16),
                  gmlp_bs[0].reshape(A_GROUPS, CHUNK, 1), pool_w[0].astype(BF16), row(pool_scale[0]), 256)
    x1, h2p, lg = _mix_out(y, 0, y, 1, even_w_out[0].astype(BF16).reshape(2, HALF, D), x0, N_ALL, mod[0],
                           row(norm2_g[0]), router_t(0), 512)
    x1 = _moe_layer(0, x1, h2p, lg, router_bias[0], mod[0], exp_w, shared_w, None)

    perm = _head_perm()
    w_in1 = odd_w_in[0]
    w_main = jnp.concatenate([w_in1[:, :2 * HALF], w_in1[:, 2 * HALF:3 * HALF][:, perm]], axis=1).astype(BF16)
    w_kv = w_in1[:, 3 * HALF:].astype(BF16)
    w_out1 = odd_w_out[0]
    w_out1 = jnp.stack([w_out1[:HALF], w_out1[HALF:][perm]], axis=0).astype(BF16)
    cos_t, sin_t = _rope_tables()
    zc, q, k, v = _odd_in(x1, mod[1], row(norm1_g[1]), w_main, w_kv, cos_t, sin_t, 512)
    y_conv = _conv_module(zc, N_LAT, conv_w[0], row(conv_b[0]), row(conv_ln_g[0]), row(conv_ln_b[0]), 256)
    y_attn = _attention(q, k, v, attn_sink[0].astype(F32))
    x2, h2p, lg = _mix_out(y_conv, 0, y_attn, 0, w_out1, x1, N_LAT, mod[1], row(norm2_g[1]),
                           router_t(1), 512)
    out = _moe_layer(1, x2, h2p, lg, router_bias[1], mod[1], exp_w, shared_w, row(final_g))
    return out.reshape(BATCH, SEQ, D)
```

```python
import functools

import jax
import jax.numpy as jnp
from jax import lax
from jax.experimental import pallas as pl
from jax.experimental.pallas import tpu as pltpu

F32 = jnp.float32
BF16 = jnp.bfloat16
U32 = jnp.uint32
I32 = jnp.int32

D = 2048
BATCH = 4
SEQ = 4096
DEPTH = 2
GRID_W = 64
CTX_LEN = 256
HALF = D // 2
CHUNK = 128
A_GROUPS = 4
A_GW = HALF // A_GROUPS
POOL_WINDOWS = (2, 4, 8, 16)
B_GW = HALF // len(POOL_WINDOWS)
CONV_W = 31
HEAD_DIM = 64
N_Q_HEADS = HALF // HEAD_DIM
N_KV_HEADS = 2
Q_PER_KV = N_Q_HEADS // N_KV_HEADS
KV_W = N_KV_HEADS * HEAD_DIM
ATT_BLK = 128
WINDOW = 128
AXIS_DIM = HEAD_DIM // 2
ROPE_BASE = 10000.0
N_EXPERTS = 64
N_EXPERT_GROUPS = 8
PER_GROUP = N_EXPERTS // N_EXPERT_GROUPS
TOPK_GROUPS = 4
TOP_K = 8
EXPERT_FF = 512
ROUTED_SCALE = 2.5
EPS = 1e-6

N_LAT = BATCH * SEQ
N_CTX = BATCH * CTX_LEN
N_ALL = N_LAT + N_CTX
CTX_GROUP = BATCH
HALO = 128
CONV_HALO = 16
MOE_BM = 256

MIB = 1024 * 1024


def _cparams(sem, vmem_mib):
    return pltpu.CompilerParams(dimension_semantics=sem, vmem_limit_bytes=vmem_mib * MIB)


def _dot(a, b):
    return jnp.dot(a, b, preferred_element_type=F32)


def _dot_nt(a, b):
    return lax.dot_general(a, b, (((1,), (1,)), ((), ())), preferred_element_type=F32)


def _rms_mod(x, g, sh, sc):
    y = x * lax.rsqrt(jnp.mean(x * x, axis=-1, keepdims=True) + EPS) * g
    return y * (1.0 + sc) + sh


def _layer_norm(x, g, b):
    mu = jnp.mean(x, axis=-1, keepdims=True)
    xc = x - mu
    var = jnp.mean(xc * xc, axis=-1, keepdims=True)
    return xc * lax.rsqrt(var + EPS) * g + b


def _gelu(x):
    return 0.5 * x * (1.0 + lax.erf(x * (2.0 ** -0.5)))


def _silu(x):
    return x * jax.nn.sigmoid(x)


def _split_bf16(x):
    hi = x.astype(BF16)
    lo = (x - hi.astype(F32)).astype(BF16)
    return hi, lo


def _pack_halves(y):
    n = y.shape[1] // 2
    lo = lax.bitcast_convert_type(y[:, :n].astype(BF16).astype(F32), U32) >> 16
    hi = lax.bitcast_convert_type(y[:, n:].astype(BF16).astype(F32), U32) & U32(0xFFFF0000)
    return hi | lo


def _unpack_halves(p):
    lo = lax.bitcast_convert_type(p << 16, F32)
    hi = lax.bitcast_convert_type(p & U32(0xFFFF0000), F32)
    return lo, hi


def _unpack_bf16(p):
    lo, hi = _unpack_halves(p)
    return jnp.concatenate([lo.astype(BF16), hi.astype(BF16)], axis=1)


LANE_BLKS = HALF // 128


def _store_tile_rows(ref, packed):
    tm = packed.shape[0]
    for s in range(LANE_BLKS):
        ref[pl.ds(s, tm, stride=LANE_BLKS), :] = packed[:, s * 128:(s + 1) * 128]


def _load_tile_rows(ref, tm):
    return jnp.concatenate([ref[pl.ds(s, tm, stride=LANE_BLKS), :] for s in range(LANE_BLKS)], axis=1)


def _mod_spec(chunk, group_fn):
    return pl.BlockSpec((None, None, 1, D), lambda i, *_: (group_fn(i), chunk, 0, 0))


def _group_fn(tm):
    return lambda i: jnp.minimum(i // (SEQ // tm), CTX_GROUP)


def _const_spec(*shape):
    return pl.BlockSpec(shape, lambda *_: (0,) * len(shape))


def _ada_kernel(c_ref, w_ref, b_ref, o_ref):
    s = _silu(c_ref[...]).astype(BF16)
    o_ref[...] = _dot(s, w_ref[...].astype(BF16)) + b_ref[...]


def _ada_mod(c, c_ctx, ada_w, ada_b):
    tn = 1024
    cv = jnp.zeros((8, D), F32).at[:BATCH].set(c).at[CTX_GROUP].set(c_ctx)
    out = pl.pallas_call(
        _ada_kernel,
        grid=(DEPTH, 6 * D // tn),
        in_specs=[pl.BlockSpec((8, D), lambda l, j: (0, 0)),
                  pl.BlockSpec((None, D, tn), lambda l, j: (l, 0, j)),
                  pl.BlockSpec((None, 1, tn), lambda l, j: (l, 0, j))],
        out_specs=pl.BlockSpec((None, 8, tn), lambda l, j: (l, 0, j)),
        out_shape=jax.ShapeDtypeStruct((DEPTH, 8, 6 * D), F32),
        compiler_params=_cparams(("arbitrary", "arbitrary"), 40),
        name="ada_mod",
    )(cv, ada_w, ada_b.reshape(DEPTH, 1, 6 * D))
    return out.reshape(DEPTH, 8, 6, 1, D)


def _two_stream_specs(tm):
    lat_tiles = N_LAT // tm
    return [pl.BlockSpec((tm, D), lambda i: (jnp.minimum(i, lat_tiles - 1), 0)),
            pl.BlockSpec((tm, D), lambda i: (jnp.maximum(i - lat_tiles, 0), 0))]


def _two_stream_rows(xl_ref, xc_ref):
    tm = xl_ref.shape[0]
    return jnp.where(pl.program_id(0) < N_LAT // tm, xl_ref[...], xc_ref[...])


def _even_in_kernel(xl_ref, xc_ref, g_ref, sh_ref, sc_ref, w_ref, uv_ref, z_ref, h_ref):
    h_ref[...] = _rms_mod(_two_stream_rows(xl_ref, xc_ref), g_ref[...], sh_ref[...], sc_ref[...]).astype(BF16)
    for j in range(2):
        cs = slice(j * HALF, (j + 1) * HALF)
        uv_ref[:, cs] = _gelu(_dot(h_ref[...], w_ref[:, cs])).astype(BF16)
    z_ref[...] = _dot(h_ref[...], w_ref[:, 2 * HALF:])


def _resident_spec(*shape):
    return pl.BlockSpec(shape, lambda *_: (0,) * len(shape), pipeline_mode=pl.Buffered(1))


def _even_in(xl, xc, mod_l, norm_g, w_in, tm):
    n_rows = N_ALL
    group_fn = _group_fn(tm)
    return pl.pallas_call(
        _even_in_kernel,
        grid=(n_rows // tm,),
        in_specs=_two_stream_specs(tm) + [
                  _const_spec(1, D),
                  _mod_spec(0, group_fn), _mod_spec(1, group_fn),
                  _resident_spec(D, 3 * HALF)],
        out_specs=[pl.BlockSpec((tm, 2 * HALF), lambda i: (i, 0)),
                   pl.BlockSpec((tm, HALF), lambda i: (i, 0))],
        out_shape=[jax.ShapeDtypeStruct((n_rows, 2 * HALF), BF16),
                   jax.ShapeDtypeStruct((n_rows, HALF), F32)],
        scratch_shapes=[pltpu.VMEM((tm, D), BF16)],
        compiler_params=_cparams(("arbitrary",), 48),
        name="even_in",
    )(xl, xc, norm_g, mod_l, mod_l, w_in)


def _band(d, w):
    inside = lax.bitcast_convert_type(d + w // 2, U32) < U32(w)
    return jnp.where(inside, 1.0, 0.0).astype(BF16)


def _seq_tile(i, tm):
    lat_tiles = N_LAT // tm
    is_lat = i < lat_tiles
    it = i % (SEQ // tm)
    first = jnp.logical_or(jnp.logical_not(is_lat), it == 0)
    last = jnp.logical_or(jnp.logical_not(is_lat), it == SEQ // tm - 1)
    pos0 = jnp.where(is_lat, it * tm, 0)
    seq_len = jnp.where(is_lat, SEQ, CTX_LEN)
    return first, last, pos0, seq_len


def _even_mix_kernel(u_ref, v_ref, z_ref, zp_ref, zn_ref, lng_ref, lnb_ref, ws_ref, bs_ref,
                     wp_ref, ps_ref, y_ref, *, tm):
    first, last, pos0, seq_len = _seq_tile(pl.program_id(0), tm)
    vn = _layer_norm(v_ref[...].astype(F32), lng_ref[...], lnb_ref[...]).astype(BF16)
    for g in range(A_GROUPS):
        cs = slice(g * A_GW, (g + 1) * A_GW)
        for c in range(tm // CHUNK):
            rs = slice(c * CHUNK, (c + 1) * CHUNK)
            mixed = _dot(ws_ref[g], vn[rs, cs]) + bs_ref[g]
            y_ref[rs, cs] = (u_ref[rs, cs].astype(F32) * mixed).astype(BF16)
    z = z_ref[...]
    zp = jnp.where(first, 0.0, zp_ref[...])
    zn = jnp.where(last, 0.0, zn_ref[...])
    z_hi, z_lo = _split_bf16(z)
    zp_hi, zp_lo = _split_bf16(zp)
    zn_hi, zn_lo = _split_bf16(zn)
    d_main = (lax.broadcasted_iota(I32, (tm, tm), 1) - lax.broadcasted_iota(I32, (tm, tm), 0))
    d_halo = (lax.broadcasted_iota(I32, (tm, HALO), 1) - lax.broadcasted_iota(I32, (tm, HALO), 0))
    pos = pos0 + lax.broadcasted_iota(I32, (tm, 1), 0)
    for g, w in enumerate(POOL_WINDOWS):
        cs = slice(g * B_GW, (g + 1) * B_GW)
        bm_ = _band(d_main, w)
        bp = _band(d_halo - HALO, w)
        bn = _band(d_halo + tm, w)
        tot = (_dot(bm_, z_hi[:, cs]) + _dot(bm_, z_lo[:, cs])
               + _dot(bp, zp_hi[:, cs]) + _dot(bp, zp_lo[:, cs])
               + _dot(bn, zn_hi[:, cs]) + _dot(bn, zn_lo[:, cs]))
        cnt = (jnp.minimum(pos + w // 2, seq_len) - jnp.maximum(pos - w // 2, 0)).astype(F32)
        pooled = (tot / cnt - z[:, cs]).astype(BF16)
        y_ref[:, HALF + g * B_GW:HALF + (g + 1) * B_GW] = (
            _dot(pooled, wp_ref[g]) * ps_ref[:, cs]).astype(BF16)


def _even_mix(uv, z, ln_g, ln_b, ws, bs, wp, ps, tm):
    n_rows = z.shape[0]
    hb = tm // HALO
    n_hblk = n_rows // HALO
    return pl.pallas_call(
        functools.partial(_even_mix_kernel, tm=tm),
        grid=(n_rows // tm,),
        in_specs=[pl.BlockSpec((tm, HALF), lambda i: (i, 0)),
                  pl.BlockSpec((tm, HALF), lambda i: (i, 1)),
                  pl.BlockSpec((tm, HALF), lambda i: (i, 0)),
                  pl.BlockSpec((HALO, HALF), lambda i: (jnp.maximum(i * hb - 1, 0), 0)),
                  pl.BlockSpec((HALO, HALF), lambda i: (jnp.minimum((i + 1) * hb, n_hblk - 1), 0)),
                  _const_spec(1, HALF), _const_spec(1, HALF),
                  _const_spec(A_GROUPS, CHUNK, CHUNK), _const_spec(A_GROUPS, CHUNK, 1),
                  _const_spec(len(POOL_WINDOWS), B_GW, B_GW), _const_spec(1, HALF)],
        out_specs=pl.BlockSpec((tm, D), lambda i: (i, 0)),
        out_shape=jax.ShapeDtypeStruct((n_rows, D), BF16),
        compiler_params=_cparams(("arbitrary",), 40),
        name="even_mix",
    )(uv, uv, z, z, z, ln_g, ln_b, ws, bs, wp, ps)


def _mix_out_kernel(ya_ref, yb_ref, w_ref, xl_ref, xc_ref, g1_ref, n2_ref, sh_ref, sc_ref, rw_ref,
                    xo_ref, hp_ref, lg_ref):
    o = _dot(ya_ref[...], w_ref[0]) + _dot(yb_ref[...], w_ref[1])
    xn = _two_stream_rows(xl_ref, xc_ref) + g1_ref[...] * o
    xo_ref[...] = xn
    h = _rms_mod(xn, n2_ref[...], sh_ref[...], sc_ref[...])
    _store_tile_rows(hp_ref, _pack_halves(h))
    lg_ref[...] = _dot_nt(rw_ref[...], h.astype(BF16))


def _mix_out(ya, ya_col, yb, yb_col, w_out, xl, xc, n_rows, mod_l, norm2_g, rw, tm):
    group_fn = _group_fn(tm)
    return pl.pallas_call(
        _mix_out_kernel,
        grid=(n_rows // tm,),
        in_specs=[pl.BlockSpec((tm, HALF), lambda i: (i, ya_col)),
                  pl.BlockSpec((tm, HALF), lambda i: (i, yb_col)),
                  _resident_spec(2, HALF, D)] + _two_stream_specs(tm) + [
                  _mod_spec(2, group_fn), _const_spec(1, D), _mod_spec(3, group_fn), _mod_spec(4, group_fn),
                  _const_spec(N_EXPERTS, D)],
        out_specs=[pl.BlockSpec((tm, D), lambda i: (i, 0)),
                   pl.BlockSpec((tm * LANE_BLKS, 128), lambda i: (i, 0)),
                   pl.BlockSpec((N_EXPERTS, tm), lambda i: (0, i))],
        out_shape=[jax.ShapeDtypeStruct((n_rows, D), F32),
                   jax.ShapeDtypeStruct((n_rows * LANE_BLKS, 128), U32),
                   jax.ShapeDtypeStruct((N_EXPERTS, n_rows), F32)],
        compiler_params=_cparams(("arbitrary",), 56),
        name="mix_out",
    )(ya, yb, w_out, xl, xc, mod_l, norm2_g, mod_l, mod_l, rw)


def _rope(x, cos, sin_signed, first_half):
    partner = jnp.where(first_half, pltpu.roll(x, 128 - AXIS_DIM // 2, 1), pltpu.roll(x, AXIS_DIM // 2, 1))
    return x * cos + partner * sin_signed


def _odd_in_kernel(x_ref, g_ref, sh_ref, sc_ref, w_ref, wkv_ref, cos_ref, sin_ref,
                   zc_ref, q_ref, k_ref, v_ref, h_ref):
    tm = x_ref.shape[0]
    h_ref[...] = _rms_mod(x_ref[...], g_ref[...], sh_ref[...], sc_ref[...]).astype(BF16)
    first_half = (lax.broadcasted_iota(I32, (tm, 128), 1) % AXIS_DIM) < (AXIS_DIM // 2)
    cos = cos_ref[...]
    sin = sin_ref[...]
    for b in range(HALF // 256):
        a = _dot(h_ref[...], w_ref[:, b * 256:(b + 1) * 256])
        gate = _dot(h_ref[...], w_ref[:, HALF + b * 256:HALF + (b + 1) * 256])
        zc_ref[:, b * 256:(b + 1) * 256] = a * jax.nn.sigmoid(gate)
    for b in range(HALF // 256):
        q = _dot(h_ref[...], w_ref[:, 2 * HALF + b * 256:2 * HALF + (b + 1) * 256])
        for s in range(2):
            cs = slice(b * 256 + s * 128, b * 256 + (s + 1) * 128)
            q_ref[:, cs] = (_rope(q[:, s * 128:(s + 1) * 128], cos, sin, first_half)
                            * (HEAD_DIM ** -0.5)).astype(BF16)
    kv = _dot(h_ref[...], wkv_ref[...])
    k_ref[...] = _rope(kv[:, :KV_W], cos, sin, first_half).astype(BF16)
    v_ref[...] = kv[:, KV_W:].astype(BF16)


def _odd_in(x, mod_l, norm_g, w_main, w_kv, cos_t, sin_t, tm):
    n_rows = x.shape[0]
    group_fn = _group_fn(tm)
    lat_tiles = N_LAT // tm
    pos_blk = lambda i: (jnp.where(i < lat_tiles, i % (SEQ // tm), SEQ // tm), 0)
    row = lambda w: pl.BlockSpec((tm, w), lambda i: (i, 0))
    return pl.pallas_call(
        _odd_in_kernel,
        grid=(n_rows // tm,),
        in_specs=[pl.BlockSpec((tm, D), lambda i: (i, 0)),
                  _const_spec(1, D),
                  _mod_spec(0, group_fn), _mod_spec(1, group_fn),
                  _resident_spec(D, 3 * HALF),
                  _resident_spec(D, 2 * KV_W),
                  pl.BlockSpec((tm, 128), pos_blk), pl.BlockSpec((tm, 128), pos_blk)],
        out_specs=[row(HALF), row(HALF), row(KV_W), row(KV_W)],
        out_shape=[jax.ShapeDtypeStruct((n_rows, HALF), F32),
                   jax.ShapeDtypeStruct((n_rows, HALF), BF16),
                   jax.ShapeDtypeStruct((n_rows, KV_W), BF16),
                   jax.ShapeDtypeStruct((n_rows, KV_W), BF16)],
        scratch_shapes=[pltpu.VMEM((tm, D), BF16)],
        compiler_params=_cparams(("arbitrary",), 48),
        name="odd_in",
    )(x, norm_g, mod_l, mod_l, w_main, w_kv, cos_t, sin_t)


def _conv_kernel(z_ref, zp_ref, zn_ref, w_ref, b_ref, lng_ref, lnb_ref, y_ref, ze_ref, zs_ref, c_ref, *, tm):
    first, last, _, _ = _seq_tile(pl.program_id(0), tm)
    ze_ref[0:CONV_HALO, :] = jnp.where(first, 0.0, zp_ref[...])
    ze_ref[CONV_HALO:CONV_HALO + tm, :] = z_ref[...]
    ze_ref[CONV_HALO + tm:, :] = jnp.where(last, 0.0, zn_ref[...])
    rc = 64
    base = CONV_HALO - CONV_W // 2
    n_sh = zs_ref.shape[1]
    for b in range(8):
        zs_ref[b] = ze_ref[b:b + n_sh, :]

    def lane_block(cb, _):
        cs = pl.ds(pl.multiple_of(cb * 128, 128), 128)
        for r in range(tm // rc):
            acc = jnp.zeros((rc, 128), F32)
            for t in range(CONV_W):
                off = base + t
                start = r * rc + (off // 8) * 8
                acc = acc + w_ref[t:t + 1, cs] * zs_ref[off % 8, start:start + rc, cs]
            c_ref[r * rc:(r + 1) * rc, cs] = acc
        return 0

    lax.fori_loop(0, HALF // 128, lane_block, 0)
    y = _layer_norm(c_ref[...] + b_ref[...], lng_ref[...], lnb_ref[...])
    y_ref[...] = _silu(y).astype(BF16)


def _conv_module(zc, n_rows, conv_w, conv_b, ln_g, ln_b, tm):
    hb = tm // CONV_HALO
    n_hblk = zc.shape[0] // CONV_HALO
    return pl.pallas_call(
        functools.partial(_conv_kernel, tm=tm),
        grid=(n_rows // tm,),
        in_specs=[pl.BlockSpec((tm, HALF), lambda i: (i, 0)),
                  pl.BlockSpec((CONV_HALO, HALF), lambda i: (jnp.maximum(i * hb - 1, 0), 0)),
                  pl.BlockSpec((CONV_HALO, HALF), lambda i: (jnp.minimum((i + 1) * hb, n_hblk - 1), 0)),
                  _const_spec(CONV_W, HALF), _const_spec(1, HALF), _const_spec(1, HALF), _const_spec(1, HALF)],
        out_specs=pl.BlockSpec((tm, HALF), lambda i: (i, 0)),
        out_shape=jax.ShapeDtypeStruct((n_rows, HALF), BF16),
        scratch_shapes=[pltpu.VMEM((tm + 2 * CONV_HALO, HALF), F32),
                        pltpu.VMEM((8, tm + 2 * CONV_HALO - 8, HALF), F32),
                        pltpu.VMEM((tm, HALF), F32)],
        compiler_params=_cparams(("arbitrary",), 40),
        name="conv_module",
    )(zc, zc, zc, conv_w, conv_b, ln_g, ln_b)


def _attn_kernel(sink_ref, q_ref, kp_ref, kc_ref, kn_ref, vp_ref, vc_ref, vn_ref, kx_ref, vx_ref, o_ref):
    i = pl.program_id(1)
    n_keys = 3 * ATT_BLK + CTX_LEN
    kb = jnp.concatenate([kp_ref[...], kc_ref[...], kn_ref[...], kx_ref[...]], axis=0)
    vb = jnp.concatenate([vp_ref[...], vc_ref[...], vn_ref[...], vx_ref[...]], axis=0)
    key_lane = lax.broadcasted_iota(I32, (n_keys, 2 * HEAD_DIM), 1)
    k_head = [jnp.where(key_lane < HEAD_DIM, kb, jnp.zeros_like(kb)),
              jnp.where(key_lane >= HEAD_DIM, kb, jnp.zeros_like(kb))]
    r = lax.broadcasted_iota(I32, (ATT_BLK, n_keys), 0)
    c = lax.broadcasted_iota(I32, (ATT_BLK, n_keys), 1)
    kpos = c + (i - 1) * ATT_BLK
    band_ok = lax.bitcast_convert_type(c - r, U32) <= U32(2 * WINDOW)
    in_seq = lax.bitcast_convert_type(kpos, U32) < U32(SEQ)
    bias = jnp.where(c >= 3 * ATT_BLK, 0.0, jnp.where(band_ok, jnp.where(in_seq, 0.0, -jnp.inf), -jnp.inf))
    gs = 4
    rows = gs * ATT_BLK
    out_lane = lax.broadcasted_iota(I32, (rows, 2 * HEAD_DIM), 1)
    bias_s = jnp.concatenate([bias] * gs, axis=0)
    for g0 in range(0, Q_PER_KV, gs):
        qs = jnp.concatenate([q_ref[:, g * 128:(g + 1) * 128] for g in range(g0, g0 + gs)], axis=0)
        outs = []
        for kvh in range(N_KV_HEADS):
            sk = jnp.concatenate([jnp.full((ATT_BLK, 1), sink_ref[kvh * Q_PER_KV + g], F32)
                                  for g in range(g0, g0 + gs)], axis=0)
            s = _dot_nt(qs, k_head[kvh]) + bias_s
            m = jnp.maximum(jnp.max(s, axis=-1, keepdims=True), sk)
            e = jnp.exp(s - m)
            den = jnp.sum(e, axis=-1, keepdims=True) + jnp.exp(sk - m)
            outs.append(_dot(e.astype(BF16), vb) / den)
        o = jnp.where(out_lane < HEAD_DIM, outs[0], outs[1]).astype(BF16)
        for j in range(gs):
            o_ref[:, (g0 + j) * 128:(g0 + j + 1) * 128] = o[j * ATT_BLK:(j + 1) * ATT_BLK, :]


def _attention(q, k, v, sink):
    nb = SEQ // ATT_BLK
    ctx0 = N_LAT // CTX_LEN
    blk = lambda w, fn: pl.BlockSpec((ATT_BLK, w), fn)
    prev = lambda b, i, s: (b * nb + jnp.maximum(i - 1, 0), 0)
    cur = lambda b, i, s: (b * nb + i, 0)
    nxt = lambda b, i, s: (b * nb + jnp.minimum(i + 1, nb - 1), 0)
    ctx = pl.BlockSpec((CTX_LEN, KV_W), lambda b, i, s: (ctx0 + b, 0))
    grid_spec = pltpu.PrefetchScalarGridSpec(
        num_scalar_prefetch=1,
        grid=(BATCH, nb),
        in_specs=[blk(HALF, cur), blk(KV_W, prev), blk(KV_W, cur), blk(KV_W, nxt),
                  blk(KV_W, prev), blk(KV_W, cur), blk(KV_W, nxt), ctx, ctx],
        out_specs=blk(HALF, cur),
    )
    return pl.pallas_call(
        _attn_kernel,
        grid_spec=grid_spec,
        out_shape=jax.ShapeDtypeStruct((N_LAT, HALF), BF16),
        compiler_params=_cparams(("arbitrary", "arbitrary"), 32),
        name="attention",
    )(sink, q, k, k, k, v, v, v, k, v)


def _first_argmax(x, iota, n):
    m = jnp.max(x, axis=0, keepdims=True)
    first = jnp.min(jnp.where(x == m, iota, n), axis=0, keepdims=True)
    return m, first


def _route_kernel(lg_ref, bias_ref, idx_ref, gate_ref, rank_ref, cnt_ref, carry_ref):
    tm = lg_ref.shape[1]

    @pl.when(pl.program_id(0) == 0)
    def _():
        carry_ref[...] = jnp.zeros_like(carry_ref)

    scores = jax.nn.sigmoid(lg_ref[...])
    biased = scores + bias_ref[...]
    sub = lax.broadcasted_iota(I32, (PER_GROUP, tm), 0)
    blocks = [biased[g * PER_GROUP:(g + 1) * PER_GROUP, :] for g in range(N_EXPERT_GROUPS)]
    gs = []
    for blk in blocks:
        m1, f1 = _first_argmax(blk, sub, PER_GROUP)
        m2 = jnp.max(jnp.where(sub == f1, -jnp.inf, blk), axis=0, keepdims=True)
        gs.append(m1 + m2)
    gs = jnp.concatenate(gs, axis=0)
    giota = lax.broadcasted_iota(I32, (N_EXPERT_GROUPS, tm), 0)
    keep = jnp.zeros((N_EXPERT_GROUPS, tm), F32)
    for _ in range(TOPK_GROUPS):
        _, f = _first_argmax(gs, giota, N_EXPERT_GROUPS)
        hit = giota == f
        keep = jnp.where(hit, 1.0, keep)
        gs = jnp.where(hit, -jnp.inf, gs)
    cur = jnp.concatenate([jnp.where(keep[g:g + 1, :] > 0.0, blocks[g], -jnp.inf)
                           for g in range(N_EXPERT_GROUPS)], axis=0)
    eiota = lax.broadcasted_iota(I32, (N_EXPERTS, tm), 0)
    chosen = jnp.zeros((N_EXPERTS, tm), F32)
    idx, sel = [], []
    for _ in range(TOP_K):
        _, f = _first_argmax(cur, eiota, N_EXPERTS)
        hit = eiota == f
        idx.append(f)
        sel.append(jnp.sum(jnp.where(hit, scores, 0.0), axis=0, keepdims=True))
        cur = jnp.where(hit, -jnp.inf, cur)
        chosen = jnp.where(hit, 1.0, chosen)
    sel = jnp.concatenate(sel, axis=0)
    idx = jnp.concatenate(idx, axis=0)
    gate_ref[...] = sel / jnp.sum(sel, axis=0, keepdims=True) * ROUTED_SCALE
    idx_ref[...] = idx
    before = jnp.where(lax.broadcasted_iota(I32, (tm, tm), 0) < lax.broadcasted_iota(I32, (tm, tm), 1), 1.0, 0.0)
    rank = _dot(chosen.astype(BF16), before.astype(BF16)) + carry_ref[:, 0:1]
    rank_ref[...] = jnp.concatenate(
        [jnp.sum(jnp.where(eiota == idx[k:k + 1, :], rank, 0.0), axis=0, keepdims=True) for k in range(TOP_K)],
        axis=0).astype(I32)
    carry_ref[...] = carry_ref[...] + jnp.sum(chosen, axis=1, keepdims=True)
    cnt_ref[...] = carry_ref[...]


def _route(logits_t, router_bias, tm):
    n_tok = logits_t.shape[1]
    tok = lambda rows: pl.BlockSpec((rows, tm), lambda i: (0, i))
    return pl.pallas_call(
        _route_kernel,
        grid=(n_tok // tm,),
        in_specs=[tok(N_EXPERTS), _const_spec(N_EXPERTS, 1)],
        out_specs=[tok(TOP_K), tok(TOP_K), tok(TOP_K), _const_spec(N_EXPERTS, 128)],
        out_shape=[jax.ShapeDtypeStruct((TOP_K, n_tok), I32),
                   jax.ShapeDtypeStruct((TOP_K, n_tok), F32),
                   jax.ShapeDtypeStruct((TOP_K, n_tok), I32),
                   jax.ShapeDtypeStruct((N_EXPERTS, 128), F32)],
        scratch_shapes=[pltpu.VMEM((N_EXPERTS, 128), F32)],
        compiler_params=_cparams(("arbitrary",), 32),
        name="route",
    )(logits_t, router_bias.astype(F32).reshape(N_EXPERTS, 1))


def _moe_kernel(be_ref, na_ref, src_ref, srcn_ref, dst_ref, h_hbm, wg_ref, wu_ref, wd_ref, y_hbm,
                xb, yb, wgb, wub, wdb, sem_in, sem_out):
    i = pl.program_id(0)
    n_steps = pl.num_programs(0)
    n_act = na_ref[0]
    cur = i % 2
    oth = 1 - cur
    bm = xb.shape[1] // LANE_BLKS

    def tile(ref, row):
        start = row * LANE_BLKS
        if not isinstance(row, int):
            start = pl.multiple_of(start, LANE_BLKS)
        return ref.at[pl.ds(start, LANE_BLKS), :]

    def gather(idx_ref, s):
        for r in range(bm):
            pltpu.make_async_copy(tile(h_hbm, idx_ref[0, 0, r]), tile(xb.at[s], r),
                                  sem_in.at[s]).start(priority=r % 2)

    def scatter(idx_ref, s):
        for r in range(bm):
            pltpu.make_async_copy(tile(yb.at[s], r), tile(y_hbm, idx_ref[0, 0, r]),
                                  sem_out.at[s]).start(priority=r % 2)

    def wait_gather(s):
        pltpu.make_async_copy(h_hbm.at[pl.ds(0, bm * LANE_BLKS), :], xb.at[s], sem_in.at[s]).wait()

    def wait_scatter(s):
        pltpu.make_async_copy(yb.at[s], y_hbm.at[pl.ds(0, bm * LANE_BLKS), :], sem_out.at[s]).wait()

    @pl.when(i == 0)
    def _():
        gather(src_ref, 0)

    @pl.when(i < n_act)
    def _():
        wait_gather(cur)

        @pl.when(i >= 2)
        def _():
            wait_scatter(cur)

        @pl.when(jnp.logical_or(i == 0, be_ref[i] != be_ref[jnp.maximum(i - 1, 0)]))
        def _():
            wgb[...] = wg_ref[...].astype(BF16)
            wub[...] = wu_ref[...].astype(BF16)
            wdb[...] = wd_ref[...].astype(BF16)

        gather(srcn_ref, oth)
        x = _unpack_bf16(_load_tile_rows(xb.at[cur], bm))
        hid = (_silu(_dot(x, wgb[...])) * _dot(x, wub[...])).astype(BF16)
        _store_tile_rows(yb.at[cur], _pack_halves(_dot(hid, wdb[...])))
        scatter(dst_ref, cur)

    @pl.when(i == n_steps - 1)
    def _():
        wait_gather(n_act % 2)
        wait_scatter((n_act + 1) % 2)

        @pl.when(n_act >= 2)
        def _():
            wait_scatter(n_act % 2)


def _moe_experts(layer, h2p, blk_exp, n_act, src, dst, w_gate, w_up, w_down):
    n_blocks = blk_exp.shape[0]
    bm = MOE_BM
    smem_blk = lambda fn: pl.BlockSpec((1, 1, bm), fn, memory_space=pltpu.SMEM)
    wspec = lambda a, b: pl.BlockSpec((None, None, a, b), lambda i, be, na: (layer, be[i], 0, 0))
    grid_spec = pltpu.PrefetchScalarGridSpec(
        num_scalar_prefetch=2,
        grid=(n_blocks,),
        in_specs=[smem_blk(lambda i, be, na: (i, 0, 0)),
                  smem_blk(lambda i, be, na: (jnp.minimum(i + 1, n_blocks - 1), 0, 0)),
                  smem_blk(lambda i, be, na: (i, 0, 0)),
                  pl.BlockSpec(memory_space=pl.ANY),
                  wspec(D, EXPERT_FF), wspec(D, EXPERT_FF), wspec(EXPERT_FF, D)],
        out_specs=pl.BlockSpec(memory_space=pl.ANY),
        scratch_shapes=[pltpu.VMEM((2, bm * LANE_BLKS, 128), U32), pltpu.VMEM((2, bm * LANE_BLKS, 128), U32),
                        pltpu.VMEM((D, EXPERT_FF), BF16), pltpu.VMEM((D, EXPERT_FF), BF16),
                        pltpu.VMEM((EXPERT_FF, D), BF16),
                        pltpu.SemaphoreType.DMA((2,)), pltpu.SemaphoreType.DMA((2,))],
    )
    src3 = src.reshape(n_blocks, 1, bm)
    return pl.pallas_call(
        _moe_kernel,
        grid_spec=grid_spec,
        out_shape=jax.ShapeDtypeStruct((n_blocks * bm * LANE_BLKS, 128), U32),
        compiler_params=_cparams(("arbitrary",), 48),
        name="moe_experts",
    )(blk_exp, n_act, src3, src3, dst.reshape(n_blocks, 1, bm), h2p, w_gate, w_up, w_down)


def _shared_kernel(h_ref, wg_ref, wu_ref, wd_ref, o_ref):
    x = _unpack_bf16(_load_tile_rows(h_ref, o_ref.shape[0]))
    hid = (_silu(_dot(x, wg_ref[...])) * _dot(x, wu_ref[...])).astype(BF16)
    o_ref[...] = _dot(hid, wd_ref[...])


def _shared_expert(h2p, wg, wu, wd, tm):
    n_rows = h2p.shape[0] // LANE_BLKS
    return pl.pallas_call(
        _shared_kernel,
        grid=(n_rows // tm,),
        in_specs=[pl.BlockSpec((tm * LANE_BLKS, 128), lambda i: (i, 0)),
                  _const_spec(D, EXPERT_FF), _const_spec(D, EXPERT_FF), _const_spec(EXPERT_FF, D)],
        out_specs=pl.BlockSpec((tm, D), lambda i: (i, 0)),
        out_shape=jax.ShapeDtypeStruct((n_rows, D), F32),
        compiler_params=_cparams(("arbitrary",), 40),
        name="shared_expert",
    )(h2p, wg, wu, wd)


def _combine_kernel(*refs, final):
    x_ref, sh_ref, gt_ref, g2_ref = refs[:4]
    y_refs = refs[4:4 + TOP_K]
    fg_ref = refs[4 + TOP_K] if final else None
    o_ref = refs[-1]
    tm = x_ref.shape[0]
    gt = gt_ref[...]
    for b in range(LANE_BLKS):
        c_lo = slice(b * 128, (b + 1) * 128)
        c_hi = slice(HALF + b * 128, HALF + (b + 1) * 128)
        acc_lo = sh_ref[:, c_lo]
        acc_hi = sh_ref[:, c_hi]
        for k in range(TOP_K):
            lo, hi = _unpack_halves(y_refs[k][pl.ds(b, tm, stride=LANE_BLKS), :])
            acc_lo = acc_lo + lo * gt[:, k:k + 1]
            acc_hi = acc_hi + hi * gt[:, k:k + 1]
        o_ref[:, c_lo] = x_ref[:, c_lo] + g2_ref[:, c_lo] * acc_lo
        o_ref[:, c_hi] = x_ref[:, c_hi] + g2_ref[:, c_hi] * acc_hi
    if final:
        xn = o_ref[...]
        o_ref[...] = xn * lax.rsqrt(jnp.mean(xn * xn, axis=-1, keepdims=True) + EPS) * fg_ref[...]


def _combine(x, shared, gates, mod_l, y_tk, n_tok, final_g, tm):
    nt = n_tok // tm
    group_fn = _group_fn(tm)
    row = pl.BlockSpec((tm, D), lambda i: (i, 0))
    in_specs = [row, row, pl.BlockSpec((tm, TOP_K), lambda i: (i, 0)), _mod_spec(5, group_fn)]
    in_specs += [pl.BlockSpec((tm * LANE_BLKS, 128), functools.partial(lambda i, k: (k * nt + i, 0), k=k))
                 for k in range(TOP_K)]
    args = [x, shared, gates, mod_l] + [y_tk] * TOP_K
    final = final_g is not None
    if final:
        in_specs.append(_const_spec(1, D))
        args.append(final_g)
    return pl.pallas_call(
        functools.partial(_combine_kernel, final=final),
        grid=(nt,),
        in_specs=in_specs,
        out_specs=row,
        out_shape=jax.ShapeDtypeStruct((n_tok, D), F32),
        compiler_params=_cparams(("arbitrary",), 40),
        name="moe_combine",
    )(*args)


def _dispatch_plan(idx_t, rank_t, cnt, n_tok):
    bm = MOE_BM
    n_assign = n_tok * TOP_K
    n_blocks = n_assign // bm + N_EXPERTS
    n_rows = n_blocks * bm
    sizes = cnt[:, 0].astype(I32)
    padded = (sizes + bm - 1) // bm * bm
    pad_end = jnp.cumsum(padded)
    pad_start = pad_end - padded
    blk_first = jnp.arange(n_blocks, dtype=I32) * bm
    blk_exp = jnp.minimum(jnp.sum((pad_end[None, :] <= blk_first[:, None]).astype(I32), axis=1), N_EXPERTS - 1)
    n_act = (pad_end[-1] // bm).astype(I32).reshape(1)
    experts = jnp.arange(N_EXPERTS, dtype=I32)
    slot = jnp.sum(jnp.where(idx_t[:, :, None] == experts, pad_start, 0), axis=-1) + rank_t
    inv = jnp.zeros((n_rows,), I32).at[slot.reshape(-1)].add(jnp.arange(1, n_assign + 1, dtype=I32)) - 1
    valid = inv >= 0
    spare = n_assign + jnp.repeat(blk_exp, bm) * bm + jnp.arange(n_rows, dtype=I32) % bm
    src = jnp.where(valid, inv % n_tok, 0).astype(I32)
    dst = jnp.where(valid, inv, spare).astype(I32)
    return blk_exp, n_act, src, dst


def _moe_layer(layer, x, h2p, logits_t, router_bias, mod_l, exp_w, shared_w, final_g):
    n_tok = h2p.shape[0] // LANE_BLKS
    idx_t, gates_t, rank_t, cnt = _route(logits_t, router_bias, 512)
    blk_exp, n_act, src, dst = _dispatch_plan(idx_t, rank_t, cnt, n_tok)
    y_tk = _moe_experts(layer, h2p, blk_exp, n_act, src, dst, *exp_w)
    shared = _shared_expert(h2p, *[w[layer].astype(BF16) for w in shared_w], 512)
    return _combine(x, shared, gates_t.T, mod_l, y_tk, n_tok, final_g, 128)


def _rope_tables():
    rows = SEQ // GRID_W
    row = jnp.broadcast_to(jnp.arange(rows)[:, None], (rows, GRID_W)).reshape(-1).astype(F32)
    col = jnp.broadcast_to(jnp.arange(GRID_W)[None, :], (rows, GRID_W)).reshape(-1).astype(F32)
    inv = ROPE_BASE ** (-jnp.arange(0, AXIS_DIM, 2, dtype=F32) / AXIS_DIM)
    ang_r = row[:, None] * inv
    ang_c = col[:, None] * inv
    ang = jnp.concatenate([ang_r, ang_r, ang_c, ang_c], axis=-1)
    cos, sin = jnp.cos(ang), jnp.sin(ang)
    sign = jnp.where((jnp.arange(HEAD_DIM) % AXIS_DIM) < AXIS_DIM // 2, -1.0, 1.0).astype(F32)
    ident = 512
    cos_t = jnp.concatenate([jnp.tile(cos, (1, 2)), jnp.ones((ident, 128), F32)], axis=0)
    sin_t = jnp.concatenate([jnp.tile(sin * sign, (1, 2)), jnp.zeros((ident, 128), F32)], axis=0)
    return cos_t, sin_t


def _head_perm():
    g = jnp.arange(Q_PER_KV)[:, None, None]
    kvh = jnp.arange(N_KV_HEADS)[None, :, None]
    d = jnp.arange(HEAD_DIM)[None, None, :]
    return ((kvh * Q_PER_KV + g) * HEAD_DIM + d).reshape(-1)


def kernel(x, c, ctx, c_ctx, ada_w, ada_b, norm1_g, norm2_g, even_w_in, gmlp_ln_g, gmlp_ln_b, gmlp_ws, gmlp_bs, pool_w, pool_scale, even_w_out, odd_w_in, conv_w, conv_b, conv_ln_g, conv_ln_b, attn_sink, odd_w_out, router_w, router_bias, exp_w_gate, exp_w_up, exp_w_down, shared_w_gate, shared_w_up, shared_w_down, final_g):
    mod = _ada_mod(c, c_ctx, ada_w, ada_b)
    row = lambda a: a.reshape(1, -1)
    exp_w = (exp_w_gate, exp_w_up, exp_w_down)
    shared_w = (shared_w_gate, shared_w_up, shared_w_down)

    def router_t(i):
        return router_w[i].T.astype(BF16)

    xl = x.reshape(N_LAT, D)
    xc = ctx.reshape(N_CTX, D)

    uv, z = _even_in(xl, xc, mod[0], row(norm1_g[0]), even_w_in[0].astype(BF16), 512)
    y = _even_mix(uv, z, row(gmlp_ln_g[0]), row(gmlp_ln_b[0]), gmlp_ws[0].astype(BF16),
                  gmlp_bs[0].reshape(A_GROUPS, CHUNK, 1), pool_w[0].astype(BF16), row(pool_scale[0]), 256)
    x1, h2p, lg = _mix_out(y, 0, y, 1, even_w_out[0].astype(BF16).reshape(2, HALF, D), xl, xc, N_ALL, mod[0],
                           row(norm2_g[0]), router_t(0), 512)
    x1 = _moe_layer(0, x1, h2p, lg, router_bias[0], mod[0], exp_w, shared_w, None)

    perm = _head_perm()
    w_in1 = odd_w_in[0]
    w_main = jnp.concatenate([w_in1[:, :2 * HALF], w_in1[:, 2 * HALF:3 * HALF][:, perm]], axis=1).astype(BF16)
    w_kv = w_in1[:, 3 * HALF:].astype(BF16)
    w_out1 = odd_w_out[0]
    w_out1 = jnp.stack([w_out1[:HALF], w_out1[HALF:][perm]], axis=0).astype(BF16)
    cos_t, sin_t = _rope_tables()
    zc, q, k, v = _odd_in(x1, mod[1], row(norm1_g[1]), w_main, w_kv, cos_t, sin_t, 512)
    y_conv = _conv_module(zc, N_LAT, conv_w[0], row(conv_b[0]), row(conv_ln_g[0]), row(conv_ln_b[0]), 256)
    y_attn = _attention(q, k, v, attn_sink[0].astype(F32))
    x2, h2p, lg = _mix_out(y_conv, 0, y_attn, 0, w_out1, x1, x1, N_LAT, mod[1], row(norm2_g[1]),
                           router_t(1), 512)
    out = _moe_layer(1, x2, h2p, lg, router_bias[1], mod[1], exp_w, shared_w, row(final_g))
    return out.reshape(BATCH, SEQ, D)
```

```python
import functools

import jax
import jax.numpy as jnp
from jax import lax
from jax.experimental import pallas as pl
from jax.experimental.pallas import tpu as pltpu

F32 = jnp.float32
BF16 = jnp.bfloat16
U32 = jnp.uint32
I32 = jnp.int32

D = 2048
BATCH = 4
SEQ = 4096
DEPTH = 2
GRID_W = 64
CTX_LEN = 256
HALF = D // 2
CHUNK = 128
A_GROUPS = 4
A_GW = HALF // A_GROUPS
POOL_WINDOWS = (2, 4, 8, 16)
B_GW = HALF // len(POOL_WINDOWS)
CONV_W = 31
HEAD_DIM = 64
N_Q_HEADS = HALF // HEAD_DIM
N_KV_HEADS = 2
Q_PER_KV = N_Q_HEADS // N_KV_HEADS
KV_W = N_KV_HEADS * HEAD_DIM
ATT_BLK = 128
WINDOW = 128
AXIS_DIM = HEAD_DIM // 2
ROPE_BASE = 10000.0
N_EXPERTS = 64
N_EXPERT_GROUPS = 8
PER_GROUP = N_EXPERTS // N_EXPERT_GROUPS
TOPK_GROUPS = 4
TOP_K = 8
EXPERT_FF = 512
ROUTED_SCALE = 2.5
EPS = 1e-6

N_LAT = BATCH * SEQ
N_CTX = BATCH * CTX_LEN
N_ALL = N_LAT + N_CTX
CTX_GROUP = BATCH
HALO = 128
CONV_HALO = 16
MOE_BM = 256

MIB = 1024 * 1024


def _cparams(sem, vmem_mib):
    return pltpu.CompilerParams(dimension_semantics=sem, vmem_limit_bytes=vmem_mib * MIB)


def _dot(a, b):
    return jnp.dot(a, b, preferred_element_type=F32)


def _dot_nt(a, b):
    return lax.dot_general(a, b, (((1,), (1,)), ((), ())), preferred_element_type=F32)


def _rms_mod(x, g, sh, sc):
    y = x * lax.rsqrt(jnp.mean(x * x, axis=-1, keepdims=True) + EPS) * g
    return y * (1.0 + sc) + sh


def _layer_norm(x, g, b):
    mu = jnp.mean(x, axis=-1, keepdims=True)
    xc = x - mu
    var = jnp.mean(xc * xc, axis=-1, keepdims=True)
    return xc * lax.rsqrt(var + EPS) * g + b


def _gelu(x):
    return 0.5 * x * (1.0 + lax.erf(x * (2.0 ** -0.5)))


def _silu(x):
    return x * jax.nn.sigmoid(x)


def _split_bf16(x):
    hi = x.astype(BF16)
    lo = (x - hi.astype(F32)).astype(BF16)
    return hi, lo


def _pack_halves(y):
    n = y.shape[1] // 2
    lo = lax.bitcast_convert_type(y[:, :n].astype(BF16).astype(F32), U32) >> 16
    hi = lax.bitcast_convert_type(y[:, n:].astype(BF16).astype(F32), U32) & U32(0xFFFF0000)
    return hi | lo


def _unpack_halves(p):
    lo = lax.bitcast_convert_type(p << 16, F32)
    hi = lax.bitcast_convert_type(p & U32(0xFFFF0000), F32)
    return lo, hi


def _unpack_bf16(p):
    lo, hi = _unpack_halves(p)
    return jnp.concatenate([lo.astype(BF16), hi.astype(BF16)], axis=1)


LANE_BLKS = HALF // 128


def _store_tile_rows(ref, packed):
    tm = packed.shape[0]
    for s in range(LANE_BLKS):
        ref[pl.ds(s, tm, stride=LANE_BLKS), :] = packed[:, s * 128:(s + 1) * 128]


def _load_tile_rows(ref, tm):
    return jnp.concatenate([ref[pl.ds(s, tm, stride=LANE_BLKS), :] for s in range(LANE_BLKS)], axis=1)


def _mod_spec(chunk, group_fn):
    return pl.BlockSpec((None, None, 1, D), lambda i, *_: (group_fn(i), chunk, 0, 0))


def _group_fn(tm):
    return lambda i: jnp.minimum(i // (SEQ // tm), CTX_GROUP)


def _const_spec(*shape):
    return pl.BlockSpec(shape, lambda *_: (0,) * len(shape))


def _ada_kernel(c_ref, w_ref, b_ref, o_ref):
    s = _silu(c_ref[...]).astype(BF16)
    o_ref[...] = _dot(s, w_ref[...].astype(BF16)) + b_ref[...]


def _ada_mod(c, c_ctx, ada_w, ada_b):
    tn = 1024
    cv = jnp.zeros((8, D), F32).at[:BATCH].set(c).at[CTX_GROUP].set(c_ctx)
    out = pl.pallas_call(
        _ada_kernel,
        grid=(DEPTH, 6 * D // tn),
        in_specs=[pl.BlockSpec((8, D), lambda l, j: (0, 0)),
                  pl.BlockSpec((None, D, tn), lambda l, j: (l, 0, j)),
                  pl.BlockSpec((None, 1, tn), lambda l, j: (l, 0, j))],
        out_specs=pl.BlockSpec((None, 8, tn), lambda l, j: (l, 0, j)),
        out_shape=jax.ShapeDtypeStruct((DEPTH, 8, 6 * D), F32),
        compiler_params=_cparams(("arbitrary", "arbitrary"), 40),
        name="ada_mod",
    )(cv, ada_w, ada_b.reshape(DEPTH, 1, 6 * D))
    return out.reshape(DEPTH, 8, 6, 1, D)


def _two_stream_specs(tm):
    lat_tiles = N_LAT // tm
    return [pl.BlockSpec((tm, D), lambda i: (jnp.minimum(i, lat_tiles - 1), 0)),
            pl.BlockSpec((tm, D), lambda i: (jnp.maximum(i - lat_tiles, 0), 0))]


def _two_stream_rows(xl_ref, xc_ref):
    tm = xl_ref.shape[0]
    return jnp.where(pl.program_id(0) < N_LAT // tm, xl_ref[...], xc_ref[...])


def _even_in_kernel(xl_ref, xc_ref, g_ref, sh_ref, sc_ref, w_ref, uv_ref, z_ref, h_ref):
    h_ref[...] = _rms_mod(_two_stream_rows(xl_ref, xc_ref), g_ref[...], sh_ref[...], sc_ref[...]).astype(BF16)
    for j in range(2):
        cs = slice(j * HALF, (j + 1) * HALF)
        uv_ref[:, cs] = _gelu(_dot(h_ref[...], w_ref[:, cs])).astype(BF16)
    z_ref[...] = _dot(h_ref[...], w_ref[:, 2 * HALF:])


def _resident_spec(*shape):
    return pl.BlockSpec(shape, lambda *_: (0,) * len(shape), pipeline_mode=pl.Buffered(1))


def _even_in(xl, xc, mod_l, norm_g, w_in, tm):
    n_rows = N_ALL
    group_fn = _group_fn(tm)
    return pl.pallas_call(
        _even_in_kernel,
        grid=(n_rows // tm,),
        in_specs=_two_stream_specs(tm) + [
                  _const_spec(1, D),
                  _mod_spec(0, group_fn), _mod_spec(1, group_fn),
                  _resident_spec(D, 3 * HALF)],
        out_specs=[pl.BlockSpec((tm, 2 * HALF), lambda i: (i, 0)),
                   pl.BlockSpec((tm, HALF), lambda i: (i, 0))],
        out_shape=[jax.ShapeDtypeStruct((n_rows, 2 * HALF), BF16),
                   jax.ShapeDtypeStruct((n_rows, HALF), F32)],
        scratch_shapes=[pltpu.VMEM((tm, D), BF16)],
        compiler_params=_cparams(("arbitrary",), 48),
        name="even_in",
    )(xl, xc, norm_g, mod_l, mod_l, w_in)


def _band(d, w):
    inside = lax.bitcast_convert_type(d + w // 2, U32) < U32(w)
    return jnp.where(inside, 1.0, 0.0).astype(BF16)


def _seq_tile(i, tm):
    lat_tiles = N_LAT // tm
    is_lat = i < lat_tiles
    it = i % (SEQ // tm)
    first = jnp.logical_or(jnp.logical_not(is_lat), it == 0)
    last = jnp.logical_or(jnp.logical_not(is_lat), it == SEQ // tm - 1)
    pos0 = jnp.where(is_lat, it * tm, 0)
    seq_len = jnp.where(is_lat, SEQ, CTX_LEN)
    return first, last, pos0, seq_len


def _even_mix_kernel(u_ref, v_ref, z_ref, zp_ref, zn_ref, lng_ref, lnb_ref, ws_ref, bs_ref,
                     wp_ref, ps_ref, y_ref, *, tm):
    first, last, pos0, seq_len = _seq_tile(pl.program_id(0), tm)
    vn = _layer_norm(v_ref[...].astype(F32), lng_ref[...], lnb_ref[...]).astype(BF16)
    for g in range(A_GROUPS):
        cs = slice(g * A_GW, (g + 1) * A_GW)
        for c in range(tm // CHUNK):
            rs = slice(c * CHUNK, (c + 1) * CHUNK)
            mixed = _dot(ws_ref[g], vn[rs, cs]) + bs_ref[g]
            y_ref[rs, cs] = (u_ref[rs, cs].astype(F32) * mixed).astype(BF16)
    z = z_ref[...]
    zp = jnp.where(first, 0.0, zp_ref[...])
    zn = jnp.where(last, 0.0, zn_ref[...])
    z_hi, z_lo = _split_bf16(z)
    zp_hi, zp_lo = _split_bf16(zp)
    zn_hi, zn_lo = _split_bf16(zn)
    d_main = (lax.broadcasted_iota(I32, (tm, tm), 1) - lax.broadcasted_iota(I32, (tm, tm), 0))
    d_halo = (lax.broadcasted_iota(I32, (tm, HALO), 1) - lax.broadcasted_iota(I32, (tm, HALO), 0))
    pos = pos0 + lax.broadcasted_iota(I32, (tm, 1), 0)
    for g, w in enumerate(POOL_WINDOWS):
        cs = slice(g * B_GW, (g + 1) * B_GW)
        bm_ = _band(d_main, w)
        bp = _band(d_halo - HALO, w)
        bn = _band(d_halo + tm, w)
        tot = (_dot(bm_, z_hi[:, cs]) + _dot(bm_, z_lo[:, cs])
               + _dot(bp, zp_hi[:, cs]) + _dot(bp, zp_lo[:, cs])
               + _dot(bn, zn_hi[:, cs]) + _dot(bn, zn_lo[:, cs]))
        cnt = (jnp.minimum(pos + w // 2, seq_len) - jnp.maximum(pos - w // 2, 0)).astype(F32)
        pooled = (tot / cnt - z[:, cs]).astype(BF16)
        y_ref[:, HALF + g * B_GW:HALF + (g + 1) * B_GW] = (
            _dot(pooled, wp_ref[g]) * ps_ref[:, cs]).astype(BF16)


def _even_mix(uv, z, ln_g, ln_b, ws, bs, wp, ps, tm):
    n_rows = z.shape[0]
    hb = tm // HALO
    n_hblk = n_rows // HALO
    return pl.pallas_call(
        functools.partial(_even_mix_kernel, tm=tm),
        grid=(n_rows // tm,),
        in_specs=[pl.BlockSpec((tm, HALF), lambda i: (i, 0)),
                  pl.BlockSpec((tm, HALF), lambda i: (i, 1)),
                  pl.BlockSpec((tm, HALF), lambda i: (i, 0)),
                  pl.BlockSpec((HALO, HALF), lambda i: (jnp.maximum(i * hb - 1, 0), 0)),
                  pl.BlockSpec((HALO, HALF), lambda i: (jnp.minimum((i + 1) * hb, n_hblk - 1), 0)),
                  _const_spec(1, HALF), _const_spec(1, HALF),
                  _const_spec(A_GROUPS, CHUNK, CHUNK), _const_spec(A_GROUPS, CHUNK, 1),
                  _const_spec(len(POOL_WINDOWS), B_GW, B_GW), _const_spec(1, HALF)],
        out_specs=pl.BlockSpec((tm, D), lambda i: (i, 0)),
        out_shape=jax.ShapeDtypeStruct((n_rows, D), BF16),
        compiler_params=_cparams(("arbitrary",), 40),
        name="even_mix",
    )(uv, uv, z, z, z, ln_g, ln_b, ws, bs, wp, ps)


def _mix_out_kernel(ya_ref, yb_ref, w_ref, xl_ref, xc_ref, g1_ref, n2_ref, sh_ref, sc_ref, rw_ref,
                    xo_ref, hp_ref, lg_ref):
    o = _dot(ya_ref[...], w_ref[0]) + _dot(yb_ref[...], w_ref[1])
    xn = _two_stream_rows(xl_ref, xc_ref) + g1_ref[...] * o
    xo_ref[...] = xn
    h = _rms_mod(xn, n2_ref[...], sh_ref[...], sc_ref[...])
    _store_tile_rows(hp_ref, _pack_halves(h))
    lg_ref[...] = _dot_nt(rw_ref[...], h.astype(BF16))


def _mix_out(ya, ya_col, yb, yb_col, w_out, xl, xc, n_rows, mod_l, norm2_g, rw, tm):
    group_fn = _group_fn(tm)
    return pl.pallas_call(
        _mix_out_kernel,
        grid=(n_rows // tm,),
        in_specs=[pl.BlockSpec((tm, HALF), lambda i: (i, ya_col)),
                  pl.BlockSpec((tm, HALF), lambda i: (i, yb_col)),
                  _resident_spec(2, HALF, D)] + _two_stream_specs(tm) + [
                  _mod_spec(2, group_fn), _const_spec(1, D), _mod_spec(3, group_fn), _mod_spec(4, group_fn),
                  _const_spec(N_EXPERTS, D)],
        out_specs=[pl.BlockSpec((tm, D), lambda i: (i, 0)),
                   pl.BlockSpec((tm * LANE_BLKS, 128), lambda i: (i, 0)),
                   pl.BlockSpec((N_EXPERTS, tm), lambda i: (0, i))],
        out_shape=[jax.ShapeDtypeStruct((n_rows, D), F32),
                   jax.ShapeDtypeStruct((n_rows * LANE_BLKS, 128), U32),
                   jax.ShapeDtypeStruct((N_EXPERTS, n_rows), F32)],
        compiler_params=_cparams(("arbitrary",), 56),
        name="mix_out",
    )(ya, yb, w_out, xl, xc, mod_l, norm2_g, mod_l, mod_l, rw)


def _rope(x, cos, sin_signed, first_half):
    partner = jnp.where(first_half, pltpu.roll(x, 128 - AXIS_DIM // 2, 1), pltpu.roll(x, AXIS_DIM // 2, 1))
    return x * cos + partner * sin_signed


def _odd_in_kernel(x_ref, g_ref, sh_ref, sc_ref, w_ref, wkv_ref, cos_ref, sin_ref,
                   zc_ref, q_ref, k_ref, v_ref, h_ref):
    tm = x_ref.shape[0]
    h_ref[...] = _rms_mod(x_ref[...], g_ref[...], sh_ref[...], sc_ref[...]).astype(BF16)
    first_half = (lax.broadcasted_iota(I32, (tm, 128), 1) % AXIS_DIM) < (AXIS_DIM // 2)
    cos = cos_ref[...]
    sin = sin_ref[...]
    for b in range(HALF // 256):
        a = _dot(h_ref[...], w_ref[:, b * 256:(b + 1) * 256])
        gate = _dot(h_ref[...], w_ref[:, HALF + b * 256:HALF + (b + 1) * 256])
        zc_ref[:, b * 256:(b + 1) * 256] = a * jax.nn.sigmoid(gate)
    for b in range(HALF // 256):
        q = _dot(h_ref[...], w_ref[:, 2 * HALF + b * 256:2 * HALF + (b + 1) * 256])
        for s in range(2):
            cs = slice(b * 256 + s * 128, b * 256 + (s + 1) * 128)
            q_ref[:, cs] = (_rope(q[:, s * 128:(s + 1) * 128], cos, sin, first_half)
                            * (HEAD_DIM ** -0.5)).astype(BF16)
    kv = _dot(h_ref[...], wkv_ref[...])
    k_ref[...] = _rope(kv[:, :KV_W], cos, sin, first_half).astype(BF16)
    v_ref[...] = kv[:, KV_W:].astype(BF16)


def _odd_in(x, mod_l, norm_g, w_main, w_kv, cos_t, sin_t, tm):
    n_rows = x.shape[0]
    group_fn = _group_fn(tm)
    lat_tiles = N_LAT // tm
    pos_blk = lambda i: (jnp.where(i < lat_tiles, i % (SEQ // tm), SEQ // tm), 0)
    row = lambda w: pl.BlockSpec((tm, w), lambda i: (i, 0))
    return pl.pallas_call(
        _odd_in_kernel,
        grid=(n_rows // tm,),
        in_specs=[pl.BlockSpec((tm, D), lambda i: (i, 0)),
                  _const_spec(1, D),
                  _mod_spec(0, group_fn), _mod_spec(1, group_fn),
                  _resident_spec(D, 3 * HALF),
                  _resident_spec(D, 2 * KV_W),
                  pl.BlockSpec((tm, 128), pos_blk), pl.BlockSpec((tm, 128), pos_blk)],
        out_specs=[row(HALF), row(HALF), row(KV_W), row(KV_W)],
        out_shape=[jax.ShapeDtypeStruct((n_rows, HALF), F32),
                   jax.ShapeDtypeStruct((n_rows, HALF), BF16),
                   jax.ShapeDtypeStruct((n_rows, KV_W), BF16),
                   jax.ShapeDtypeStruct((n_rows, KV_W), BF16)],
        scratch_shapes=[pltpu.VMEM((tm, D), BF16)],
        compiler_params=_cparams(("arbitrary",), 48),
        name="odd_in",
    )(x, norm_g, mod_l, mod_l, w_main, w_kv, cos_t, sin_t)


def _conv_kernel(z_ref, zp_ref, zn_ref, w_ref, b_ref, lng_ref, lnb_ref, y_ref, ze_ref, zs_ref, c_ref, *, tm):
    first, last, _, _ = _seq_tile(pl.program_id(0), tm)
    ze_ref[0:CONV_HALO, :] = jnp.where(first, 0.0, zp_ref[...])
    ze_ref[CONV_HALO:CONV_HALO + tm, :] = z_ref[...]
    ze_ref[CONV_HALO + tm:, :] = jnp.where(last, 0.0, zn_ref[...])
    rc = 64
    base = CONV_HALO - CONV_W // 2
    n_sh = zs_ref.shape[1]
    for b in range(8):
        zs_ref[b] = ze_ref[b:b + n_sh, :]

    def lane_block(cb, _):
        cs = pl.ds(pl.multiple_of(cb * 128, 128), 128)
        for r in range(tm // rc):
            acc = jnp.zeros((rc, 128), F32)
            for t in range(CONV_W):
                off = base + t
                start = r * rc + (off // 8) * 8
                acc = acc + w_ref[t:t + 1, cs] * zs_ref[off % 8, start:start + rc, cs]
            c_ref[r * rc:(r + 1) * rc, cs] = acc
        return 0

    lax.fori_loop(0, HALF // 128, lane_block, 0)
    y = _layer_norm(c_ref[...] + b_ref[...], lng_ref[...], lnb_ref[...])
    y_ref[...] = _silu(y).astype(BF16)


def _conv_module(zc, n_rows, conv_w, conv_b, ln_g, ln_b, tm):
    hb = tm // CONV_HALO
    n_hblk = zc.shape[0] // CONV_HALO
    return pl.pallas_call(
        functools.partial(_conv_kernel, tm=tm),
        grid=(n_rows // tm,),
        in_specs=[pl.BlockSpec((tm, HALF), lambda i: (i, 0)),
                  pl.BlockSpec((CONV_HALO, HALF), lambda i: (jnp.maximum(i * hb - 1, 0), 0)),
                  pl.BlockSpec((CONV_HALO, HALF), lambda i: (jnp.minimum((i + 1) * hb, n_hblk - 1), 0)),
                  _const_spec(CONV_W, HALF), _const_spec(1, HALF), _const_spec(1, HALF), _const_spec(1, HALF)],
        out_specs=pl.BlockSpec((tm, HALF), lambda i: (i, 0)),
        out_shape=jax.ShapeDtypeStruct((n_rows, HALF), BF16),
        scratch_shapes=[pltpu.VMEM((tm + 2 * CONV_HALO, HALF), F32),
                        pltpu.VMEM((8, tm + 2 * CONV_HALO - 8, HALF), F32),
                        pltpu.VMEM((tm, HALF), F32)],
        compiler_params=_cparams(("arbitrary",), 40),
        name="conv_module",
    )(zc, zc, zc, conv_w, conv_b, ln_g, ln_b)


def _attn_kernel(sink_ref, q_ref, kp_ref, kc_ref, kn_ref, vp_ref, vc_ref, vn_ref, kx_ref, vx_ref, o_ref):
    i = pl.program_id(1)
    n_keys = 3 * ATT_BLK + CTX_LEN
    kb = jnp.concatenate([kp_ref[...], kc_ref[...], kn_ref[...], kx_ref[...]], axis=0)
    vb = jnp.concatenate([vp_ref[...], vc_ref[...], vn_ref[...], vx_ref[...]], axis=0)
    key_lane = lax.broadcasted_iota(I32, (n_keys, 2 * HEAD_DIM), 1)
    k_head = [jnp.where(key_lane < HEAD_DIM, kb, jnp.zeros_like(kb)),
              jnp.where(key_lane >= HEAD_DIM, kb, jnp.zeros_like(kb))]
    r = lax.broadcasted_iota(I32, (ATT_BLK, n_keys), 0)
    c = lax.broadcasted_iota(I32, (ATT_BLK, n_keys), 1)
    kpos = c + (i - 1) * ATT_BLK
    band_ok = lax.bitcast_convert_type(c - r, U32) <= U32(2 * WINDOW)
    in_seq = lax.bitcast_convert_type(kpos, U32) < U32(SEQ)
    bias = jnp.where(c >= 3 * ATT_BLK, 0.0, jnp.where(band_ok, jnp.where(in_seq, 0.0, -jnp.inf), -jnp.inf))
    gs = 4
    rows = gs * ATT_BLK
    out_lane = lax.broadcasted_iota(I32, (rows, 2 * HEAD_DIM), 1)
    bias_s = jnp.concatenate([bias] * gs, axis=0)
    for g0 in range(0, Q_PER_KV, gs):
        qs = jnp.concatenate([q_ref[:, g * 128:(g + 1) * 128] for g in range(g0, g0 + gs)], axis=0)
        outs = []
        for kvh in range(N_KV_HEADS):
            sk = jnp.concatenate([jnp.full((ATT_BLK, 1), sink_ref[kvh * Q_PER_KV + g], F32)
                                  for g in range(g0, g0 + gs)], axis=0)
            s = _dot_nt(qs, k_head[kvh]) + bias_s
            m = jnp.maximum(jnp.max(s, axis=-1, keepdims=True), sk)
            e = jnp.exp(s - m)
            den = jnp.sum(e, axis=-1, keepdims=True) + jnp.exp(sk - m)
            outs.append(_dot(e.astype(BF16), vb) / den)
        o = jnp.where(out_lane < HEAD_DIM, outs[0], outs[1]).astype(BF16)
        for j in range(gs):
            o_ref[:, (g0 + j) * 128:(g0 + j + 1) * 128] = o[j * ATT_BLK:(j + 1) * ATT_BLK, :]


def _attention(q, k, v, sink):
    nb = SEQ // ATT_BLK
    ctx0 = N_LAT // CTX_LEN
    blk = lambda w, fn: pl.BlockSpec((ATT_BLK, w), fn)
    prev = lambda b, i, s: (b * nb + jnp.maximum(i - 1, 0), 0)
    cur = lambda b, i, s: (b * nb + i, 0)
    nxt = lambda b, i, s: (b * nb + jnp.minimum(i + 1, nb - 1), 0)
    ctx = pl.BlockSpec((CTX_LEN, KV_W), lambda b, i, s: (ctx0 + b, 0))
    grid_spec = pltpu.PrefetchScalarGridSpec(
        num_scalar_prefetch=1,
        grid=(BATCH, nb),
        in_specs=[blk(HALF, cur), blk(KV_W, prev), blk(KV_W, cur), blk(KV_W, nxt),
                  blk(KV_W, prev), blk(KV_W, cur), blk(KV_W, nxt), ctx, ctx],
        out_specs=blk(HALF, cur),
    )
    return pl.pallas_call(
        _attn_kernel,
        grid_spec=grid_spec,
        out_shape=jax.ShapeDtypeStruct((N_LAT, HALF), BF16),
        compiler_params=_cparams(("arbitrary", "arbitrary"), 32),
        name="attention",
    )(sink, q, k, k, k, v, v, v, k, v)


def _first_argmax(x, iota, n):
    m = jnp.max(x, axis=0, keepdims=True)
    first = jnp.min(jnp.where(x == m, iota, n), axis=0, keepdims=True)
    return m, first


def _route_kernel(lg_ref, bias_ref, idx_ref, gate_ref, rank_ref, cnt_ref, carry_ref):
    tm = lg_ref.shape[1]

    @pl.when(pl.program_id(0) == 0)
    def _():
        carry_ref[...] = jnp.zeros_like(carry_ref)

    scores = jax.nn.sigmoid(lg_ref[...])
    biased = scores + bias_ref[...]
    sub = lax.broadcasted_iota(I32, (PER_GROUP, tm), 0)
    blocks = [biased[g * PER_GROUP:(g + 1) * PER_GROUP, :] for g in range(N_EXPERT_GROUPS)]
    gs = []
    for blk in blocks:
        m1, f1 = _first_argmax(blk, sub, PER_GROUP)
        m2 = jnp.max(jnp.where(sub == f1, -jnp.inf, blk), axis=0, keepdims=True)
        gs.append(m1 + m2)
    gs = jnp.concatenate(gs, axis=0)
    giota = lax.broadcasted_iota(I32, (N_EXPERT_GROUPS, tm), 0)
    keep = jnp.zeros((N_EXPERT_GROUPS, tm), F32)
    for _ in range(TOPK_GROUPS):
        _, f = _first_argmax(gs, giota, N_EXPERT_GROUPS)
        hit = giota == f
        keep = jnp.where(hit, 1.0, keep)
        gs = jnp.where(hit, -jnp.inf, gs)
    cur = jnp.concatenate([jnp.where(keep[g:g + 1, :] > 0.0, blocks[g], -jnp.inf)
                           for g in range(N_EXPERT_GROUPS)], axis=0)
    eiota = lax.broadcasted_iota(I32, (N_EXPERTS, tm), 0)
    chosen = jnp.zeros((N_EXPERTS, tm), F32)
    idx, sel = [], []
    for _ in range(TOP_K):
        _, f = _first_argmax(cur, eiota, N_EXPERTS)
        hit = eiota == f
        idx.append(f)
        sel.append(jnp.sum(jnp.where(hit, scores, 0.0), axis=0, keepdims=True))
        cur = jnp.where(hit, -jnp.inf, cur)
        chosen = jnp.where(hit, 1.0, chosen)
    sel = jnp.concatenate(sel, axis=0)
    idx = jnp.concatenate(idx, axis=0)
    gate_ref[...] = sel / jnp.sum(sel, axis=0, keepdims=True) * ROUTED_SCALE
    idx_ref[...] = idx
    before = jnp.where(lax.broadcasted_iota(I32, (tm, tm), 0) < lax.broadcasted_iota(I32, (tm, tm), 1), 1.0, 0.0)
    rank = _dot(chosen.astype(BF16), before.astype(BF16)) + carry_ref[:, 0:1]
    rank_ref[...] = jnp.concatenate(
        [jnp.sum(jnp.where(eiota == idx[k:k + 1, :], rank, 0.0), axis=0, keepdims=True) for k in range(TOP_K)],
        axis=0).astype(I32)
    carry_ref[...] = carry_ref[...] + jnp.sum(chosen, axis=1, keepdims=True)
    cnt_ref[...] = carry_ref[...]


def _route(logits_t, router_bias, tm):
    n_tok = logits_t.shape[1]
    tok = lambda rows: pl.BlockSpec((rows, tm), lambda i: (0, i))
    return pl.pallas_call(
        _route_kernel,
        grid=(n_tok // tm,),
        in_specs=[tok(N_EXPERTS), _const_spec(N_EXPERTS, 1)],
        out_specs=[tok(TOP_K), tok(TOP_K), tok(TOP_K), _const_spec(N_EXPERTS, 128)],
        out_shape=[jax.ShapeDtypeStruct((TOP_K, n_tok), I32),
                   jax.ShapeDtypeStruct((TOP_K, n_tok), F32),
                   jax.ShapeDtypeStruct((TOP_K, n_tok), I32),
                   jax.ShapeDtypeStruct((N_EXPERTS, 128), F32)],
        scratch_shapes=[pltpu.VMEM((N_EXPERTS, 128), F32)],
        compiler_params=_cparams(("arbitrary",), 32),
        name="route",
    )(logits_t, router_bias.astype(F32).reshape(N_EXPERTS, 1))


def _moe_kernel(be_ref, na_ref, rp_ref, nx_ref, src_ref, srcn_ref, dst_ref, h_hbm, wg_hbm, wu_hbm, wd_hbm, y_hbm,
                xb, yb, wgf, wuf, wdf, wgb, wub, wdb, sem_in, sem_out, sem_w, *, layer):
    i = pl.program_id(0)
    n_steps = pl.num_programs(0)
    n_act = na_ref[0]
    cur = i % 2
    oth = 1 - cur
    bm = xb.shape[1] // LANE_BLKS

    def tile(ref, row):
        start = row * LANE_BLKS
        if not isinstance(row, int):
            start = pl.multiple_of(start, LANE_BLKS)
        return ref.at[pl.ds(start, LANE_BLKS), :]

    def gather(idx_ref, s):
        for r in range(bm):
            pltpu.make_async_copy(tile(h_hbm, idx_ref[0, 0, r]), tile(xb.at[s], r), sem_in.at[s]).start()

    def scatter(idx_ref, s):
        for r in range(bm):
            pltpu.make_async_copy(tile(yb.at[s], r), tile(y_hbm, idx_ref[0, 0, r]), sem_out.at[s]).start()

    def wait_gather(s):
        pltpu.make_async_copy(h_hbm.at[pl.ds(0, bm * LANE_BLKS), :], xb.at[s], sem_in.at[s]).wait()

    def wait_scatter(s):
        pltpu.make_async_copy(yb.at[s], y_hbm.at[pl.ds(0, bm * LANE_BLKS), :], sem_out.at[s]).wait()

    def weight_copies(e, s):
        return [pltpu.make_async_copy(w_hbm.at[layer, e], stage.at[s], sem_w.at[s])
                for w_hbm, stage in ((wg_hbm, wgf), (wu_hbm, wuf), (wd_hbm, wdf))]

    @pl.when(i == 0)
    def _():
        gather(src_ref, 0)
        for cp in weight_copies(be_ref[0], 0):
            cp.start(priority=1)

    @pl.when(i < n_act)
    def _():
        wait_gather(cur)

        @pl.when(i >= 2)
        def _():
            wait_scatter(cur)

        @pl.when(jnp.logical_or(i == 0, be_ref[i] != be_ref[jnp.maximum(i - 1, 0)]))
        def _():
            s = rp_ref[i]
            for cp in weight_copies(be_ref[i], s):
                cp.wait()

            @pl.when(nx_ref[i] >= 0)
            def _():
                for cp in weight_copies(nx_ref[i], 1 - s):
                    cp.start(priority=1)

            wgb[...] = wgf[s].astype(BF16)
            wub[...] = wuf[s].astype(BF16)
            wdb[...] = wdf[s].astype(BF16)

        gather(srcn_ref, oth)
        x = _unpack_bf16(_load_tile_rows(xb.at[cur], bm))
        hid = (_silu(_dot(x, wgb[...])) * _dot(x, wub[...])).astype(BF16)
        _store_tile_rows(yb.at[cur], _pack_halves(_dot(hid, wdb[...])))
        scatter(dst_ref, cur)

    @pl.when(i == n_steps - 1)
    def _():
        wait_gather(n_act % 2)
        wait_scatter((n_act + 1) % 2)

        @pl.when(n_act >= 2)
        def _():
            wait_scatter(n_act % 2)


def _moe_experts(layer, h2p, plan, w_gate, w_up, w_down):
    blk_exp, n_act, run_par, nxt_exp, src, dst = plan
    n_blocks = blk_exp.shape[0]
    bm = MOE_BM
    smem_blk = lambda fn: pl.BlockSpec((1, 1, bm), fn, memory_space=pltpu.SMEM)
    hbm = pl.BlockSpec(memory_space=pl.ANY)
    grid_spec = pltpu.PrefetchScalarGridSpec(
        num_scalar_prefetch=4,
        grid=(n_blocks,),
        in_specs=[smem_blk(lambda i, *_: (i, 0, 0)),
                  smem_blk(lambda i, *_: (jnp.minimum(i + 1, n_blocks - 1), 0, 0)),
                  smem_blk(lambda i, *_: (i, 0, 0)),
                  hbm, hbm, hbm, hbm],
        out_specs=hbm,
        scratch_shapes=[pltpu.VMEM((2, bm * LANE_BLKS, 128), U32), pltpu.VMEM((2, bm * LANE_BLKS, 128), U32),
                        pltpu.VMEM((2, D, EXPERT_FF), F32), pltpu.VMEM((2, D, EXPERT_FF), F32),
                        pltpu.VMEM((2, EXPERT_FF, D), F32),
                        pltpu.VMEM((D, EXPERT_FF), BF16), pltpu.VMEM((D, EXPERT_FF), BF16),
                        pltpu.VMEM((EXPERT_FF, D), BF16),
                        pltpu.SemaphoreType.DMA((2,)), pltpu.SemaphoreType.DMA((2,)),
                        pltpu.SemaphoreType.DMA((2,))],
    )
    src3 = src.reshape(n_blocks, 1, bm)
    return pl.pallas_call(
        functools.partial(_moe_kernel, layer=layer),
        grid_spec=grid_spec,
        out_shape=jax.ShapeDtypeStruct((n_blocks * bm * LANE_BLKS, 128), U32),
        compiler_params=_cparams(("arbitrary",), 48),
        name="moe_experts",
    )(blk_exp, n_act, run_par, nxt_exp, src3, src3, dst.reshape(n_blocks, 1, bm), h2p, w_gate, w_up, w_down)


def _combine_kernel(*refs, final):
    x_ref, h_ref, wg_ref, wu_ref, wd_ref, gt_ref, g2_ref = refs[:7]
    y_refs = refs[7:7 + TOP_K]
    fg_ref = refs[7 + TOP_K] if final else None
    o_ref, sh_ref = refs[-2:]
    tm = x_ref.shape[0]
    xh = _unpack_bf16(_load_tile_rows(h_ref, tm))
    hid = (_silu(_dot(xh, wg_ref[...])) * _dot(xh, wu_ref[...])).astype(BF16)
    sh_ref[...] = _dot(hid, wd_ref[...])
    gt = gt_ref[...]
    for b in range(LANE_BLKS):
        c_lo = slice(b * 128, (b + 1) * 128)
        c_hi = slice(HALF + b * 128, HALF + (b + 1) * 128)
        acc_lo = sh_ref[:, c_lo]
        acc_hi = sh_ref[:, c_hi]
        for k in range(TOP_K):
            lo, hi = _unpack_halves(y_refs[k][pl.ds(b, tm, stride=LANE_BLKS), :])
            acc_lo = acc_lo + lo * gt[:, k:k + 1]
            acc_hi = acc_hi + hi * gt[:, k:k + 1]
        o_ref[:, c_lo] = x_ref[:, c_lo] + g2_ref[:, c_lo] * acc_lo
        o_ref[:, c_hi] = x_ref[:, c_hi] + g2_ref[:, c_hi] * acc_hi
    if final:
        xn = o_ref[...]
        o_ref[...] = xn * lax.rsqrt(jnp.mean(xn * xn, axis=-1, keepdims=True) + EPS) * fg_ref[...]


def _combine(x, h2p, shared_w, gates, mod_l, y_tk, n_tok, final_g, tm):
    nt = n_tok // tm
    group_fn = _group_fn(tm)
    row = pl.BlockSpec((tm, D), lambda i: (i, 0))
    packed = lambda fn: pl.BlockSpec((tm * LANE_BLKS, 128), fn)
    in_specs = [row, packed(lambda i: (i, 0)),
                _resident_spec(D, EXPERT_FF), _resident_spec(D, EXPERT_FF), _resident_spec(EXPERT_FF, D),
                pl.BlockSpec((tm, TOP_K), lambda i: (i, 0)), _mod_spec(5, group_fn)]
    in_specs += [packed(functools.partial(lambda i, k: (k * nt + i, 0), k=k)) for k in range(TOP_K)]
    args = [x, h2p, *shared_w, gates, mod_l] + [y_tk] * TOP_K
    final = final_g is not None
    if final:
        in_specs.append(_const_spec(1, D))
        args.append(final_g)
    return pl.pallas_call(
        functools.partial(_combine_kernel, final=final),
        grid=(nt,),
        in_specs=in_specs,
        out_specs=row,
        out_shape=jax.ShapeDtypeStruct((n_tok, D), F32),
        scratch_shapes=[pltpu.VMEM((tm, D), F32)],
        compiler_params=_cparams(("arbitrary",), 48),
        name="moe_combine",
    )(*args)


def _dispatch_plan(idx_t, rank_t, cnt, n_tok):
    bm = MOE_BM
    n_assign = n_tok * TOP_K
    n_blocks = n_assign // bm + N_EXPERTS
    n_rows = n_blocks * bm
    sizes = cnt[:, 0].astype(I32)
    padded = (sizes + bm - 1) // bm * bm
    pad_end = jnp.cumsum(padded)
    pad_start = pad_end - padded
    blk_first = jnp.arange(n_blocks, dtype=I32) * bm
    blk_exp = jnp.minimum(jnp.sum((pad_end[None, :] <= blk_first[:, None]).astype(I32), axis=1), N_EXPERTS - 1)
    n_act = (pad_end[-1] // bm).astype(I32).reshape(1)
    experts = jnp.arange(N_EXPERTS, dtype=I32)
    slot = jnp.sum(jnp.where(idx_t[:, :, None] == experts, pad_start, 0), axis=-1) + rank_t
    inv = jnp.zeros((n_rows,), I32).at[slot.reshape(-1)].add(jnp.arange(1, n_assign + 1, dtype=I32)) - 1
    valid = inv >= 0
    spare = n_assign + jnp.repeat(blk_exp, bm) * bm + jnp.arange(n_rows, dtype=I32) % bm
    src = jnp.where(valid, inv % n_tok, 0).astype(I32)
    dst = jnp.where(valid, inv, spare).astype(I32)
    prev = jnp.concatenate([jnp.full((1,), -1, I32), blk_exp[:-1]])
    run_par = (jnp.cumsum((blk_exp != prev).astype(I32)) - 1) % 2
    nxt_blk = pad_end[blk_exp] // bm
    nxt_exp = jnp.where(nxt_blk < n_act[0], blk_exp[jnp.minimum(nxt_blk, n_blocks - 1)], -1).astype(I32)
    return blk_exp, n_act, run_par.astype(I32), nxt_exp, src, dst


def _moe_layer(layer, x, h2p, logits_t, router_bias, mod_l, exp_w, shared_w, final_g):
    n_tok = h2p.shape[0] // LANE_BLKS
    idx_t, gates_t, rank_t, cnt = _route(logits_t, router_bias, 512)
    plan = _dispatch_plan(idx_t, rank_t, cnt, n_tok)
    y_tk = _moe_experts(layer, h2p, plan, *exp_w)
    return _combine(x, h2p, [w[layer].astype(BF16) for w in shared_w], gates_t.T, mod_l, y_tk, n_tok,
                    final_g, 256)


def _rope_tables():
    rows = SEQ // GRID_W
    row = jnp.broadcast_to(jnp.arange(rows)[:, None], (rows, GRID_W)).reshape(-1).astype(F32)
    col = jnp.broadcast_to(jnp.arange(GRID_W)[None, :], (rows, GRID_W)).reshape(-1).astype(F32)
    inv = ROPE_BASE ** (-jnp.arange(0, AXIS_DIM, 2, dtype=F32) / AXIS_DIM)
    ang_r = row[:, None] * inv
    ang_c = col[:, None] * inv
    ang = jnp.concatenate([ang_r, ang_r, ang_c, ang_c], axis=-1)
    cos, sin = jnp.cos(ang), jnp.sin(ang)
    sign = jnp.where((jnp.arange(HEAD_DIM) % AXIS_DIM) < AXIS_DIM // 2, -1.0, 1.0).astype(F32)
    ident = 512
    cos_t = jnp.concatenate([jnp.tile(cos, (1, 2)), jnp.ones((ident, 128), F32)], axis=0)
    sin_t = jnp.concatenate([jnp.tile(sin * sign, (1, 2)), jnp.zeros((ident, 128), F32)], axis=0)
    return cos_t, sin_t


def _head_perm():
    g = jnp.arange(Q_PER_KV)[:, None, None]
    kvh = jnp.arange(N_KV_HEADS)[None, :, None]
    d = jnp.arange(HEAD_DIM)[None, None, :]
    return ((kvh * Q_PER_KV + g) * HEAD_DIM + d).reshape(-1)


def kernel(x, c, ctx, c_ctx, ada_w, ada_b, norm1_g, norm2_g, even_w_in, gmlp_ln_g, gmlp_ln_b, gmlp_ws, gmlp_bs, pool_w, pool_scale, even_w_out, odd_w_in, conv_w, conv_b, conv_ln_g, conv_ln_b, attn_sink, odd_w_out, router_w, router_bias, exp_w_gate, exp_w_up, exp_w_down, shared_w_gate, shared_w_up, shared_w_down, final_g):
    mod = _ada_mod(c, c_ctx, ada_w, ada_b)
    row = lambda a: a.reshape(1, -1)
    exp_w = (exp_w_gate, exp_w_up, exp_w_down)
    shared_w = (shared_w_gate, shared_w_up, shared_w_down)

    def router_t(i):
        return router_w[i].T.astype(BF16)

    xl = x.reshape(N_LAT, D)
    xc = ctx.reshape(N_CTX, D)

    uv, z = _even_in(xl, xc, mod[0], row(norm1_g[0]), even_w_in[0].astype(BF16), 512)
    y = _even_mix(uv, z, row(gmlp_ln_g[0]), row(gmlp_ln_b[0]), gmlp_ws[0].astype(BF16),
                  gmlp_bs[0].reshape(A_GROUPS, CHUNK, 1), pool_w[0].astype(BF16), row(pool_scale[0]), 256)
    x1, h2p, lg = _mix_out(y, 0, y, 1, even_w_out[0].astype(BF16).reshape(2, HALF, D), xl, xc, N_ALL, mod[0],
                           row(norm2_g[0]), router_t(0), 512)
    x1 = _moe_layer(0, x1, h2p, lg, router_bias[0], mod[0], exp_w, shared_w, None)

    perm = _head_perm()
    w_in1 = odd_w_in[0]
    w_main = jnp.concatenate([w_in1[:, :2 * HALF], w_in1[:, 2 * HALF:3 * HALF][:, perm]], axis=1).astype(BF16)
    w_kv = w_in1[:, 3 * HALF:].astype(BF16)
    w_out1 = odd_w_out[0]
    w_out1 = jnp.stack([w_out1[:HALF], w_out1[HALF:][perm]], axis=0).astype(BF16)
    cos_t, sin_t = _rope_tables()
    zc, q, k, v = _odd_in(x1, mod[1], row(norm1_g[1]), w_main, w_kv, cos_t, sin_t, 512)
    y_conv = _conv_module(zc, N_LAT, conv_w[0], row(conv_b[0]), row(conv_ln_g[0]), row(conv_ln_b[0]), 256)
    y_attn = _attention(q, k, v, attn_sink[0].astype(F32))
    x2, h2p, lg = _mix_out(y_conv, 0, y_attn, 0, w_out1, x1, x1, N_LAT, mod[1], row(norm2_g[1]),
                           router_t(1), 512)
    out = _moe_layer(1, x2, h2p, lg, router_bias[1], mod[1], exp_w, shared_w, row(final_g))
    return out.reshape(BATCH, SEQ, D)
```

```python
import functools

import jax
import jax.numpy as jnp
from jax import lax
from jax.experimental import pallas as pl
from jax.experimental.pallas import tpu as pltpu

F32 = jnp.float32
BF16 = jnp.bfloat16
U32 = jnp.uint32
I32 = jnp.int32

D = 2048
BATCH = 4
SEQ = 4096
DEPTH = 2
GRID_W = 64
CTX_LEN = 256
HALF = D // 2
CHUNK = 128
A_GROUPS = 4
A_GW = HALF // A_GROUPS
POOL_WINDOWS = (2, 4, 8, 16)
B_GW = HALF // len(POOL_WINDOWS)
CONV_W = 31
HEAD_DIM = 64
N_Q_HEADS = HALF // HEAD_DIM
N_KV_HEADS = 2
Q_PER_KV = N_Q_HEADS // N_KV_HEADS
KV_W = N_KV_HEADS * HEAD_DIM
ATT_BLK = 128
WINDOW = 128
AXIS_DIM = HEAD_DIM // 2
ROPE_BASE = 10000.0
N_EXPERTS = 64
N_EXPERT_GROUPS = 8
PER_GROUP = N_EXPERTS // N_EXPERT_GROUPS
TOPK_GROUPS = 4
TOP_K = 8
EXPERT_FF = 512
ROUTED_SCALE = 2.5
EPS = 1e-6

N_LAT = BATCH * SEQ
N_CTX = BATCH * CTX_LEN
N_ALL = N_LAT + N_CTX
CTX_GROUP = BATCH
HALO = 128
CONV_HALO = 16
MOE_BM = 256

MIB = 1024 * 1024


def _cparams(sem, vmem_mib):
    return pltpu.CompilerParams(dimension_semantics=sem, vmem_limit_bytes=vmem_mib * MIB)


def _dot(a, b):
    return jnp.dot(a, b, preferred_element_type=F32)


def _dot_nt(a, b):
    return lax.dot_general(a, b, (((1,), (1,)), ((), ())), preferred_element_type=F32)


def _rms_mod(x, g, sh, sc):
    y = x * lax.rsqrt(jnp.mean(x * x, axis=-1, keepdims=True) + EPS) * g
    return y * (1.0 + sc) + sh


def _layer_norm(x, g, b):
    mu = jnp.mean(x, axis=-1, keepdims=True)
    xc = x - mu
    var = jnp.mean(xc * xc, axis=-1, keepdims=True)
    return xc * lax.rsqrt(var + EPS) * g + b


def _gelu(x):
    return 0.5 * x * (1.0 + lax.erf(x * (2.0 ** -0.5)))


def _silu(x):
    return x * jax.nn.sigmoid(x)


def _split_bf16(x):
    hi = x.astype(BF16)
    lo = (x - hi.astype(F32)).astype(BF16)
    return hi, lo


def _pack_halves(y):
    n = y.shape[1] // 2
    lo = lax.bitcast_convert_type(y[:, :n].astype(BF16).astype(F32), U32) >> 16
    hi = lax.bitcast_convert_type(y[:, n:].astype(BF16).astype(F32), U32) & U32(0xFFFF0000)
    return hi | lo


def _unpack_halves(p):
    lo = lax.bitcast_convert_type(p << 16, F32)
    hi = lax.bitcast_convert_type(p & U32(0xFFFF0000), F32)
    return lo, hi


def _unpack_bf16(p):
    lo, hi = _unpack_halves(p)
    return jnp.concatenate([lo.astype(BF16), hi.astype(BF16)], axis=1)


LANE_BLKS = HALF // 128


def _store_tile_rows(ref, packed):
    tm = packed.shape[0]
    for s in range(LANE_BLKS):
        ref[pl.ds(s, tm, stride=LANE_BLKS), :] = packed[:, s * 128:(s + 1) * 128]


def _load_tile_rows(ref, tm):
    return jnp.concatenate([ref[pl.ds(s, tm, stride=LANE_BLKS), :] for s in range(LANE_BLKS)], axis=1)


def _mod_spec(chunk, group_fn):
    return pl.BlockSpec((None, None, 1, D), lambda i, *_: (group_fn(i), chunk, 0, 0))


def _group_fn(tm):
    return lambda i: jnp.minimum(i // (SEQ // tm), CTX_GROUP)


def _const_spec(*shape):
    return pl.BlockSpec(shape, lambda *_: (0,) * len(shape))


def _ada_kernel(c_ref, w_ref, b_ref, o_ref):
    s = _silu(c_ref[...]).astype(BF16)
    o_ref[...] = _dot(s, w_ref[...].astype(BF16)) + b_ref[...]


def _ada_mod(c, c_ctx, ada_w, ada_b):
    tn = 1024
    cv = jnp.zeros((8, D), F32).at[:BATCH].set(c).at[CTX_GROUP].set(c_ctx)
    out = pl.pallas_call(
        _ada_kernel,
        grid=(DEPTH, 6 * D // tn),
        in_specs=[pl.BlockSpec((8, D), lambda l, j: (0, 0)),
                  pl.BlockSpec((None, D, tn), lambda l, j: (l, 0, j)),
                  pl.BlockSpec((None, 1, tn), lambda l, j: (l, 0, j))],
        out_specs=pl.BlockSpec((None, 8, tn), lambda l, j: (l, 0, j)),
        out_shape=jax.ShapeDtypeStruct((DEPTH, 8, 6 * D), F32),
        compiler_params=_cparams(("arbitrary", "arbitrary"), 40),
        name="ada_mod",
    )(cv, ada_w, ada_b.reshape(DEPTH, 1, 6 * D))
    return out.reshape(DEPTH, 8, 6, 1, D)


def _two_stream_specs(tm):
    lat_tiles = N_LAT // tm
    return [pl.BlockSpec((tm, D), lambda i: (jnp.minimum(i, lat_tiles - 1), 0)),
            pl.BlockSpec((tm, D), lambda i: (jnp.maximum(i - lat_tiles, 0), 0))]


def _two_stream_rows(xl_ref, xc_ref):
    tm = xl_ref.shape[0]
    return jnp.where(pl.program_id(0) < N_LAT // tm, xl_ref[...], xc_ref[...])


def _even_in_kernel(xl_ref, xc_ref, g_ref, sh_ref, sc_ref, w_ref, uv_ref, z_ref, h_ref):
    h_ref[...] = _rms_mod(_two_stream_rows(xl_ref, xc_ref), g_ref[...], sh_ref[...], sc_ref[...]).astype(BF16)
    for j in range(2):
        cs = slice(j * HALF, (j + 1) * HALF)
        uv_ref[:, cs] = _gelu(_dot(h_ref[...], w_ref[:, cs])).astype(BF16)
    z_ref[...] = _dot(h_ref[...], w_ref[:, 2 * HALF:])


def _resident_spec(*shape):
    return pl.BlockSpec(shape, lambda *_: (0,) * len(shape), pipeline_mode=pl.Buffered(1))


def _even_in(xl, xc, mod_l, norm_g, w_in, tm):
    n_rows = N_ALL
    group_fn = _group_fn(tm)
    return pl.pallas_call(
        _even_in_kernel,
        grid=(n_rows // tm,),
        in_specs=_two_stream_specs(tm) + [
                  _const_spec(1, D),
                  _mod_spec(0, group_fn), _mod_spec(1, group_fn),
                  _resident_spec(D, 3 * HALF)],
        out_specs=[pl.BlockSpec((tm, 2 * HALF), lambda i: (i, 0)),
                   pl.BlockSpec((tm, HALF), lambda i: (i, 0))],
        out_shape=[jax.ShapeDtypeStruct((n_rows, 2 * HALF), BF16),
                   jax.ShapeDtypeStruct((n_rows, HALF), F32)],
        scratch_shapes=[pltpu.VMEM((tm, D), BF16)],
        compiler_params=_cparams(("arbitrary",), 48),
        name="even_in",
    )(xl, xc, norm_g, mod_l, mod_l, w_in)


def _band(d, w):
    inside = lax.bitcast_convert_type(d + w // 2, U32) < U32(w)
    return jnp.where(inside, 1.0, 0.0).astype(BF16)


def _seq_tile(i, tm):
    lat_tiles = N_LAT // tm
    is_lat = i < lat_tiles
    it = i % (SEQ // tm)
    first = jnp.logical_or(jnp.logical_not(is_lat), it == 0)
    last = jnp.logical_or(jnp.logical_not(is_lat), it == SEQ // tm - 1)
    pos0 = jnp.where(is_lat, it * tm, 0)
    seq_len = jnp.where(is_lat, SEQ, CTX_LEN)
    return first, last, pos0, seq_len


def _even_mix_kernel(u_ref, v_ref, z_ref, zp_ref, zn_ref, lng_ref, lnb_ref, ws_ref, bs_ref,
                     wp_ref, ps_ref, y_ref, *, tm):
    first, last, pos0, seq_len = _seq_tile(pl.program_id(0), tm)
    vn = _layer_norm(v_ref[...].astype(F32), lng_ref[...], lnb_ref[...]).astype(BF16)
    for g in range(A_GROUPS):
        cs = slice(g * A_GW, (g + 1) * A_GW)
        for c in range(tm // CHUNK):
            rs = slice(c * CHUNK, (c + 1) * CHUNK)
            mixed = _dot(ws_ref[g], vn[rs, cs]) + bs_ref[g]
            y_ref[rs, cs] = (u_ref[rs, cs].astype(F32) * mixed).astype(BF16)
    z = z_ref[...]
    zp = jnp.where(first, 0.0, zp_ref[...])
    zn = jnp.where(last, 0.0, zn_ref[...])
    z_hi, z_lo = _split_bf16(z)
    zp_hi, zp_lo = _split_bf16(zp)
    zn_hi, zn_lo = _split_bf16(zn)
    d_main = (lax.broadcasted_iota(I32, (tm, tm), 1) - lax.broadcasted_iota(I32, (tm, tm), 0))
    d_halo = (lax.broadcasted_iota(I32, (tm, HALO), 1) - lax.broadcasted_iota(I32, (tm, HALO), 0))
    pos = pos0 + lax.broadcasted_iota(I32, (tm, 1), 0)
    for g, w in enumerate(POOL_WINDOWS):
        cs = slice(g * B_GW, (g + 1) * B_GW)
        bm_ = _band(d_main, w)
        bp = _band(d_halo - HALO, w)
        bn = _band(d_halo + tm, w)
        tot = (_dot(bm_, z_hi[:, cs]) + _dot(bm_, z_lo[:, cs])
               + _dot(bp, zp_hi[:, cs]) + _dot(bp, zp_lo[:, cs])
               + _dot(bn, zn_hi[:, cs]) + _dot(bn, zn_lo[:, cs]))
        cnt = (jnp.minimum(pos + w // 2, seq_len) - jnp.maximum(pos - w // 2, 0)).astype(F32)
        pooled = (tot / cnt - z[:, cs]).astype(BF16)
        y_ref[:, HALF + g * B_GW:HALF + (g + 1) * B_GW] = (
            _dot(pooled, wp_ref[g]) * ps_ref[:, cs]).astype(BF16)


def _even_mix(uv, z, ln_g, ln_b, ws, bs, wp, ps, tm):
    n_rows = z.shape[0]
    hb = tm // HALO
    n_hblk = n_rows // HALO
    return pl.pallas_call(
        functools.partial(_even_mix_kernel, tm=tm),
        grid=(n_rows // tm,),
        in_specs=[pl.BlockSpec((tm, HALF), lambda i: (i, 0)),
                  pl.BlockSpec((tm, HALF), lambda i: (i, 1)),
                  pl.BlockSpec((tm, HALF), lambda i: (i, 0)),
                  pl.BlockSpec((HALO, HALF), lambda i: (jnp.maximum(i * hb - 1, 0), 0)),
                  pl.BlockSpec((HALO, HALF), lambda i: (jnp.minimum((i + 1) * hb, n_hblk - 1), 0)),
                  _const_spec(1, HALF), _const_spec(1, HALF),
                  _const_spec(A_GROUPS, CHUNK, CHUNK), _const_spec(A_GROUPS, CHUNK, 1),
                  _const_spec(len(POOL_WINDOWS), B_GW, B_GW), _const_spec(1, HALF)],
        out_specs=pl.BlockSpec((tm, D), lambda i: (i, 0)),
        out_shape=jax.ShapeDtypeStruct((n_rows, D), BF16),
        compiler_params=_cparams(("arbitrary",), 40),
        name="even_mix",
    )(uv, uv, z, z, z, ln_g, ln_b, ws, bs, wp, ps)


def _mix_out_kernel(ya_ref, yb_ref, w_ref, xl_ref, xc_ref, g1_ref, n2_ref, sh_ref, sc_ref, rw_ref,
                    xo_ref, hp_ref, lg_ref):
    o = _dot(ya_ref[...], w_ref[0]) + _dot(yb_ref[...], w_ref[1])
    xn = _two_stream_rows(xl_ref, xc_ref) + g1_ref[...] * o
    xo_ref[...] = xn
    h = _rms_mod(xn, n2_ref[...], sh_ref[...], sc_ref[...])
    _store_tile_rows(hp_ref, _pack_halves(h))
    lg_ref[...] = _dot_nt(rw_ref[...], h.astype(BF16))


def _mix_out(ya, ya_col, yb, yb_col, w_out, xl, xc, n_rows, mod_l, norm2_g, rw, tm):
    group_fn = _group_fn(tm)
    return pl.pallas_call(
        _mix_out_kernel,
        grid=(n_rows // tm,),
        in_specs=[pl.BlockSpec((tm, HALF), lambda i: (i, ya_col)),
                  pl.BlockSpec((tm, HALF), lambda i: (i, yb_col)),
                  _resident_spec(2, HALF, D)] + _two_stream_specs(tm) + [
                  _mod_spec(2, group_fn), _const_spec(1, D), _mod_spec(3, group_fn), _mod_spec(4, group_fn),
                  _const_spec(N_EXPERTS, D)],
        out_specs=[pl.BlockSpec((tm, D), lambda i: (i, 0)),
                   pl.BlockSpec((tm * LANE_BLKS, 128), lambda i: (i, 0)),
                   pl.BlockSpec((N_EXPERTS, tm), lambda i: (0, i))],
        out_shape=[jax.ShapeDtypeStruct((n_rows, D), F32),
                   jax.ShapeDtypeStruct((n_rows * LANE_BLKS, 128), U32),
                   jax.ShapeDtypeStruct((N_EXPERTS, n_rows), F32)],
        compiler_params=_cparams(("arbitrary",), 56),
        name="mix_out",
    )(ya, yb, w_out, xl, xc, mod_l, norm2_g, mod_l, mod_l, rw)


def _rope(x, cos, sin_signed, first_half):
    partner = jnp.where(first_half, pltpu.roll(x, 128 - AXIS_DIM // 2, 1), pltpu.roll(x, AXIS_DIM // 2, 1))
    return x * cos + partner * sin_signed


def _odd_in_kernel(x_ref, g_ref, sh_ref, sc_ref, w_ref, wkv_ref, cos_ref, sin_ref,
                   zc_ref, q_ref, k_ref, v_ref, h_ref):
    tm = x_ref.shape[0]
    h_ref[...] = _rms_mod(x_ref[...], g_ref[...], sh_ref[...], sc_ref[...]).astype(BF16)
    first_half = (lax.broadcasted_iota(I32, (tm, 128), 1) % AXIS_DIM) < (AXIS_DIM // 2)
    cos = cos_ref[...]
    sin = sin_ref[...]
    for b in range(HALF // 256):
        a = _dot(h_ref[...], w_ref[:, b * 256:(b + 1) * 256])
        gate = _dot(h_ref[...], w_ref[:, HALF + b * 256:HALF + (b + 1) * 256])
        zc_ref[:, b * 256:(b + 1) * 256] = a * jax.nn.sigmoid(gate)
    for b in range(HALF // 256):
        q = _dot(h_ref[...], w_ref[:, 2 * HALF + b * 256:2 * HALF + (b + 1) * 256])
        for s in range(2):
            cs = slice(b * 256 + s * 128, b * 256 + (s + 1) * 128)
            q_ref[:, cs] = (_rope(q[:, s * 128:(s + 1) * 128], cos, sin, first_half)
                            * (HEAD_DIM ** -0.5)).astype(BF16)
    kv = _dot(h_ref[...], wkv_ref[...])
    k_ref[...] = _rope(kv[:, :KV_W], cos, sin, first_half).astype(BF16)
    v_ref[...] = kv[:, KV_W:].astype(BF16)


def _odd_in(x, mod_l, norm_g, w_main, w_kv, cos_t, sin_t, tm):
    n_rows = x.shape[0]
    group_fn = _group_fn(tm)
    lat_tiles = N_LAT // tm
    pos_blk = lambda i: (jnp.where(i < lat_tiles, i % (SEQ // tm), SEQ // tm), 0)
    row = lambda w: pl.BlockSpec((tm, w), lambda i: (i, 0))
    return pl.pallas_call(
        _odd_in_kernel,
        grid=(n_rows // tm,),
        in_specs=[pl.BlockSpec((tm, D), lambda i: (i, 0)),
                  _const_spec(1, D),
                  _mod_spec(0, group_fn), _mod_spec(1, group_fn),
                  _resident_spec(D, 3 * HALF),
                  _resident_spec(D, 2 * KV_W),
                  pl.BlockSpec((tm, 128), pos_blk), pl.BlockSpec((tm, 128), pos_blk)],
        out_specs=[row(HALF), row(HALF), row(KV_W), row(KV_W)],
        out_shape=[jax.ShapeDtypeStruct((n_rows, HALF), F32),
                   jax.ShapeDtypeStruct((n_rows, HALF), BF16),
                   jax.ShapeDtypeStruct((n_rows, KV_W), BF16),
                   jax.ShapeDtypeStruct((n_rows, KV_W), BF16)],
        scratch_shapes=[pltpu.VMEM((tm, D), BF16)],
        compiler_params=_cparams(("arbitrary",), 48),
        name="odd_in",
    )(x, norm_g, mod_l, mod_l, w_main, w_kv, cos_t, sin_t)


def _conv_kernel(z_ref, zp_ref, zn_ref, w_ref, b_ref, lng_ref, lnb_ref, y_ref, ze_ref, zs_ref, c_ref, *, tm):
    first, last, _, _ = _seq_tile(pl.program_id(0), tm)
    ze_ref[0:CONV_HALO, :] = jnp.where(first, 0.0, zp_ref[...])
    ze_ref[CONV_HALO:CONV_HALO + tm, :] = z_ref[...]
    ze_ref[CONV_HALO + tm:, :] = jnp.where(last, 0.0, zn_ref[...])
    rc = 64
    base = CONV_HALO - CONV_W // 2
    n_sh = zs_ref.shape[1]
    for b in range(8):
        zs_ref[b] = ze_ref[b:b + n_sh, :]

    def lane_block(cb, _):
        cs = pl.ds(pl.multiple_of(cb * 128, 128), 128)
        for r in range(tm // rc):
            acc = jnp.zeros((rc, 128), F32)
            for t in range(CONV_W):
                off = base + t
                start = r * rc + (off // 8) * 8
                acc = acc + w_ref[t:t + 1, cs] * zs_ref[off % 8, start:start + rc, cs]
            c_ref[r * rc:(r + 1) * rc, cs] = acc
        return 0

    lax.fori_loop(0, HALF // 128, lane_block, 0)
    y = _layer_norm(c_ref[...] + b_ref[...], lng_ref[...], lnb_ref[...])
    y_ref[...] = _silu(y).astype(BF16)


def _conv_module(zc, n_rows, conv_w, conv_b, ln_g, ln_b, tm):
    hb = tm // CONV_HALO
    n_hblk = zc.shape[0] // CONV_HALO
    return pl.pallas_call(
        functools.partial(_conv_kernel, tm=tm),
        grid=(n_rows // tm,),
        in_specs=[pl.BlockSpec((tm, HALF), lambda i: (i, 0)),
                  pl.BlockSpec((CONV_HALO, HALF), lambda i: (jnp.maximum(i * hb - 1, 0), 0)),
                  pl.BlockSpec((CONV_HALO, HALF), lambda i: (jnp.minimum((i + 1) * hb, n_hblk - 1), 0)),
                  _const_spec(CONV_W, HALF), _const_spec(1, HALF), _const_spec(1, HALF), _const_spec(1, HALF)],
        out_specs=pl.BlockSpec((tm, HALF), lambda i: (i, 0)),
        out_shape=jax.ShapeDtypeStruct((n_rows, HALF), BF16),
        scratch_shapes=[pltpu.VMEM((tm + 2 * CONV_HALO, HALF), F32),
                        pltpu.VMEM((8, tm + 2 * CONV_HALO - 8, HALF), F32),
                        pltpu.VMEM((tm, HALF), F32)],
        compiler_params=_cparams(("arbitrary",), 40),
        name="conv_module",
    )(zc, zc, zc, conv_w, conv_b, ln_g, ln_b)


def _attn_kernel(sink_ref, q_ref, kp_ref, kc_ref, kn_ref, vp_ref, vc_ref, vn_ref, kx_ref, vx_ref, o_ref):
    i = pl.program_id(1)
    n_keys = 3 * ATT_BLK + CTX_LEN
    kb = jnp.concatenate([kp_ref[...], kc_ref[...], kn_ref[...], kx_ref[...]], axis=0)
    vb = jnp.concatenate([vp_ref[...], vc_ref[...], vn_ref[...], vx_ref[...]], axis=0)
    key_lane = lax.broadcasted_iota(I32, (n_keys, 2 * HEAD_DIM), 1)
    k_head = [jnp.where(key_lane < HEAD_DIM, kb, jnp.zeros_like(kb)),
              jnp.where(key_lane >= HEAD_DIM, kb, jnp.zeros_like(kb))]
    r = lax.broadcasted_iota(I32, (ATT_BLK, n_keys), 0)
    c = lax.broadcasted_iota(I32, (ATT_BLK, n_keys), 1)
    kpos = c + (i - 1) * ATT_BLK
    band_ok = lax.bitcast_convert_type(c - r, U32) <= U32(2 * WINDOW)
    in_seq = lax.bitcast_convert_type(kpos, U32) < U32(SEQ)
    bias = jnp.where(c >= 3 * ATT_BLK, 0.0, jnp.where(band_ok, jnp.where(in_seq, 0.0, -jnp.inf), -jnp.inf))
    gs = 4
    rows = gs * ATT_BLK
    out_lane = lax.broadcasted_iota(I32, (rows, 2 * HEAD_DIM), 1)
    bias_s = jnp.concatenate([bias] * gs, axis=0)
    for g0 in range(0, Q_PER_KV, gs):
        qs = jnp.concatenate([q_ref[:, g * 128:(g + 1) * 128] for g in range(g0, g0 + gs)], axis=0)
        outs = []
        for kvh in range(N_KV_HEADS):
            sk = jnp.concatenate([jnp.full((ATT_BLK, 1), sink_ref[kvh * Q_PER_KV + g], F32)
                                  for g in range(g0, g0 + gs)], axis=0)
            s = _dot_nt(qs, k_head[kvh]) + bias_s
            m = jnp.maximum(jnp.max(s, axis=-1, keepdims=True), sk)
            e = jnp.exp(s - m)
            den = jnp.sum(e, axis=-1, keepdims=True) + jnp.exp(sk - m)
            outs.append(_dot(e.astype(BF16), vb) / den)
        o = jnp.where(out_lane < HEAD_DIM, outs[0], outs[1]).astype(BF16)
        for j in range(gs):
            o_ref[:, (g0 + j) * 128:(g0 + j + 1) * 128] = o[j * ATT_BLK:(j + 1) * ATT_BLK, :]


def _attention(q, k, v, sink):
    nb = SEQ // ATT_BLK
    ctx0 = N_LAT // CTX_LEN
    blk = lambda w, fn: pl.BlockSpec((ATT_BLK, w), fn)
    prev = lambda b, i, s: (b * nb + jnp.maximum(i - 1, 0), 0)
    cur = lambda b, i, s: (b * nb + i, 0)
    nxt = lambda b, i, s: (b * nb + jnp.minimum(i + 1, nb - 1), 0)
    ctx = pl.BlockSpec((CTX_LEN, KV_W), lambda b, i, s: (ctx0 + b, 0))
    grid_spec = pltpu.PrefetchScalarGridSpec(
        num_scalar_prefetch=1,
        grid=(BATCH, nb),
        in_specs=[blk(HALF, cur), blk(KV_W, prev), blk(KV_W, cur), blk(KV_W, nxt),
                  blk(KV_W, prev), blk(KV_W, cur), blk(KV_W, nxt), ctx, ctx],
        out_specs=blk(HALF, cur),
    )
    return pl.pallas_call(
        _attn_kernel,
        grid_spec=grid_spec,
        out_shape=jax.ShapeDtypeStruct((N_LAT, HALF), BF16),
        compiler_params=_cparams(("arbitrary", "arbitrary"), 32),
        name="attention",
    )(sink, q, k, k, k, v, v, v, k, v)


def _first_argmax(x, iota, n):
    m = jnp.max(x, axis=0, keepdims=True)
    first = jnp.min(jnp.where(x == m, iota, n), axis=0, keepdims=True)
    return m, first


def _route_kernel(lg_ref, bias_ref, idx_ref, gate_ref, rank_ref, cnt_ref, carry_ref):
    tm = lg_ref.shape[1]

    @pl.when(pl.program_id(0) == 0)
    def _():
        carry_ref[...] = jnp.zeros_like(carry_ref)

    scores = jax.nn.sigmoid(lg_ref[...])
    biased = scores + bias_ref[...]
    sub = lax.broadcasted_iota(I32, (PER_GROUP, tm), 0)
    blocks = [biased[g * PER_GROUP:(g + 1) * PER_GROUP, :] for g in range(N_EXPERT_GROUPS)]
    gs = []
    for blk in blocks:
        m1, f1 = _first_argmax(blk, sub, PER_GROUP)
        m2 = jnp.max(jnp.where(sub == f1, -jnp.inf, blk), axis=0, keepdims=True)
        gs.append(m1 + m2)
    gs = jnp.concatenate(gs, axis=0)
    giota = lax.broadcasted_iota(I32, (N_EXPERT_GROUPS, tm), 0)
    keep = jnp.zeros((N_EXPERT_GROUPS, tm), F32)
    for _ in range(TOPK_GROUPS):
        _, f = _first_argmax(gs, giota, N_EXPERT_GROUPS)
        hit = giota == f
        keep = jnp.where(hit, 1.0, keep)
        gs = jnp.where(hit, -jnp.inf, gs)
    cur = jnp.concatenate([jnp.where(keep[g:g + 1, :] > 0.0, blocks[g], -jnp.inf)
                           for g in range(N_EXPERT_GROUPS)], axis=0)
    eiota = lax.broadcasted_iota(I32, (N_EXPERTS, tm), 0)
    chosen = jnp.zeros((N_EXPERTS, tm), F32)
    idx, sel = [], []
    for _ in range(TOP_K):
        _, f = _first_argmax(cur, eiota, N_EXPERTS)
        hit = eiota == f
        idx.append(f)
        sel.append(jnp.sum(jnp.where(hit, scores, 0.0), axis=0, keepdims=True))
        cur = jnp.where(hit, -jnp.inf, cur)
        chosen = jnp.where(hit, 1.0, chosen)
    sel = jnp.concatenate(sel, axis=0)
    idx = jnp.concatenate(idx, axis=0)
    gate_ref[...] = sel / jnp.sum(sel, axis=0, keepdims=True) * ROUTED_SCALE
    idx_ref[...] = idx
    before = jnp.where(lax.broadcasted_iota(I32, (tm, tm), 0) < lax.broadcasted_iota(I32, (tm, tm), 1), 1.0, 0.0)
    rank = _dot(chosen.astype(BF16), before.astype(BF16)) + carry_ref[:, 0:1]
    rank_ref[...] = jnp.concatenate(
        [jnp.sum(jnp.where(eiota == idx[k:k + 1, :], rank, 0.0), axis=0, keepdims=True) for k in range(TOP_K)],
        axis=0).astype(I32)
    carry_ref[...] = carry_ref[...] + jnp.sum(chosen, axis=1, keepdims=True)
    cnt_ref[...] = carry_ref[...]


def _route(logits_t, router_bias, tm):
    n_tok = logits_t.shape[1]
    tok = lambda rows: pl.BlockSpec((rows, tm), lambda i: (0, i))
    return pl.pallas_call(
        _route_kernel,
        grid=(n_tok // tm,),
        in_specs=[tok(N_EXPERTS), _const_spec(N_EXPERTS, 1)],
        out_specs=[tok(TOP_K), tok(TOP_K), tok(TOP_K), _const_spec(N_EXPERTS, 128)],
        out_shape=[jax.ShapeDtypeStruct((TOP_K, n_tok), I32),
                   jax.ShapeDtypeStruct((TOP_K, n_tok), F32),
                   jax.ShapeDtypeStruct((TOP_K, n_tok), I32),
                   jax.ShapeDtypeStruct((N_EXPERTS, 128), F32)],
        scratch_shapes=[pltpu.VMEM((N_EXPERTS, 128), F32)],
        compiler_params=_cparams(("arbitrary",), 32),
        name="route",
    )(logits_t, router_bias.astype(F32).reshape(N_EXPERTS, 1))


def _moe_kernel(be_ref, na_ref, rp_ref, nx_ref, src_ref, srcn_ref, dst_ref, h_hbm, wg_hbm, wu_hbm, wd_hbm, y_hbm,
                xb, yb, wgf, wuf, wdf, wgb, wub, wdb, sem_in, sem_out, sem_w, *, layer):
    i = pl.program_id(0)
    n_steps = pl.num_programs(0)
    n_act = na_ref[0]
    cur = i % 2
    oth = 1 - cur
    bm = xb.shape[1] // LANE_BLKS

    def tile(ref, row):
        start = row * LANE_BLKS
        if not isinstance(row, int):
            start = pl.multiple_of(start, LANE_BLKS)
        return ref.at[pl.ds(start, LANE_BLKS), :]

    def gather(idx_ref, s):
        for r in range(bm):
            pltpu.make_async_copy(tile(h_hbm, idx_ref[0, 0, r]), tile(xb.at[s], r),
                                  sem_in.at[s]).start(priority=r % 2)

    def scatter(idx_ref, s):
        for r in range(bm):
            pltpu.make_async_copy(tile(yb.at[s], r), tile(y_hbm, idx_ref[0, 0, r]),
                                  sem_out.at[s]).start(priority=r % 2)

    def wait_gather(s):
        pltpu.make_async_copy(h_hbm.at[pl.ds(0, bm * LANE_BLKS), :], xb.at[s], sem_in.at[s]).wait()

    def wait_scatter(s):
        pltpu.make_async_copy(yb.at[s], y_hbm.at[pl.ds(0, bm * LANE_BLKS), :], sem_out.at[s]).wait()

    def weight_copies(e, s):
        return [pltpu.make_async_copy(w_hbm.at[layer, e], stage.at[s], sem_w.at[s])
                for w_hbm, stage in ((wg_hbm, wgf), (wu_hbm, wuf), (wd_hbm, wdf))]

    @pl.when(i == 0)
    def _():
        gather(src_ref, 0)
        for cp in weight_copies(be_ref[0], 0):
            cp.start(priority=1)

    @pl.when(i < n_act)
    def _():
        wait_gather(cur)

        @pl.when(i >= 2)
        def _():
            wait_scatter(cur)

        @pl.when(jnp.logical_or(i == 0, be_ref[i] != be_ref[jnp.maximum(i - 1, 0)]))
        def _():
            s = rp_ref[i]
            for cp in weight_copies(be_ref[i], s):
                cp.wait()

            @pl.when(nx_ref[i] >= 0)
            def _():
                for cp in weight_copies(nx_ref[i], 1 - s):
                    cp.start(priority=1)

            wgb[...] = wgf[s].astype(BF16)
            wub[...] = wuf[s].astype(BF16)
            wdb[...] = wdf[s].astype(BF16)

        gather(srcn_ref, oth)
        x = _unpack_bf16(_load_tile_rows(xb.at[cur], bm))
        hid = (_silu(_dot(x, wgb[...])) * _dot(x, wub[...])).astype(BF16)
        _store_tile_rows(yb.at[cur], _pack_halves(_dot(hid, wdb[...])))
        scatter(dst_ref, cur)

    @pl.when(i == n_steps - 1)
    def _():
        wait_gather(n_act % 2)
        wait_scatter((n_act + 1) % 2)

        @pl.when(n_act >= 2)
        def _():
            wait_scatter(n_act % 2)


def _moe_experts(layer, h2p, plan, w_gate, w_up, w_down):
    blk_exp, n_act, run_par, nxt_exp, src, dst = plan
    n_blocks = blk_exp.shape[0]
    bm = MOE_BM
    smem_blk = lambda fn: pl.BlockSpec((1, 1, bm), fn, memory_space=pltpu.SMEM)
    hbm = pl.BlockSpec(memory_space=pl.ANY)
    grid_spec = pltpu.PrefetchScalarGridSpec(
        num_scalar_prefetch=4,
        grid=(n_blocks,),
        in_specs=[smem_blk(lambda i, *_: (i, 0, 0)),
                  smem_blk(lambda i, *_: (jnp.minimum(i + 1, n_blocks - 1), 0, 0)),
                  smem_blk(lambda i, *_: (i, 0, 0)),
                  hbm, hbm, hbm, hbm],
        out_specs=hbm,
        scratch_shapes=[pltpu.VMEM((2, bm * LANE_BLKS, 128), U32), pltpu.VMEM((2, bm * LANE_BLKS, 128), U32),
                        pltpu.VMEM((2, D, EXPERT_FF), F32), pltpu.VMEM((2, D, EXPERT_FF), F32),
                        pltpu.VMEM((2, EXPERT_FF, D), F32),
                        pltpu.VMEM((D, EXPERT_FF), BF16), pltpu.VMEM((D, EXPERT_FF), BF16),
                        pltpu.VMEM((EXPERT_FF, D), BF16),
                        pltpu.SemaphoreType.DMA((2,)), pltpu.SemaphoreType.DMA((2,)),
                        pltpu.SemaphoreType.DMA((2,))],
    )
    src3 = src.reshape(n_blocks, 1, bm)
    return pl.pallas_call(
        functools.partial(_moe_kernel, layer=layer),
        grid_spec=grid_spec,
        out_shape=jax.ShapeDtypeStruct((n_blocks * bm * LANE_BLKS, 128), U32),
        compiler_params=_cparams(("arbitrary",), 48),
        name="moe_experts",
    )(blk_exp, n_act, run_par, nxt_exp, src3, src3, dst.reshape(n_blocks, 1, bm), h2p, w_gate, w_up, w_down)


def _combine_kernel(*refs, final):
    x_ref, h_ref, wg_ref, wu_ref, wd_ref, gt_ref, g2_ref = refs[:7]
    y_refs = refs[7:7 + TOP_K]
    fg_ref = refs[7 + TOP_K] if final else None
    o_ref, sh_ref = refs[-2:]
    tm = x_ref.shape[0]
    xh = _unpack_bf16(_load_tile_rows(h_ref, tm))
    hid = (_silu(_dot(xh, wg_ref[...])) * _dot(xh, wu_ref[...])).astype(BF16)
    sh_ref[...] = _dot(hid, wd_ref[...])
    gt = gt_ref[...]
    for b in range(LANE_BLKS):
        c_lo = slice(b * 128, (b + 1) * 128)
        c_hi = slice(HALF + b * 128, HALF + (b + 1) * 128)
        acc_lo = sh_ref[:, c_lo]
        acc_hi = sh_ref[:, c_hi]
        for k in range(TOP_K):
            lo, hi = _unpack_halves(y_refs[k][pl.ds(b, tm, stride=LANE_BLKS), :])
            acc_lo = acc_lo + lo * gt[:, k:k + 1]
            acc_hi = acc_hi + hi * gt[:, k:k + 1]
        o_ref[:, c_lo] = x_ref[:, c_lo] + g2_ref[:, c_lo] * acc_lo
        o_ref[:, c_hi] = x_ref[:, c_hi] + g2_ref[:, c_hi] * acc_hi
    if final:
        xn = o_ref[...]
        o_ref[...] = xn * lax.rsqrt(jnp.mean(xn * xn, axis=-1, keepdims=True) + EPS) * fg_ref[...]


def _combine(x, h2p, shared_w, gates, mod_l, y_tk, n_tok, final_g, tm):
    nt = n_tok // tm
    group_fn = _group_fn(tm)
    row = pl.BlockSpec((tm, D), lambda i: (i, 0))
    packed = lambda fn: pl.BlockSpec((tm * LANE_BLKS, 128), fn)
    in_specs = [row, packed(lambda i: (i, 0)),
                _resident_spec(D, EXPERT_FF), _resident_spec(D, EXPERT_FF), _resident_spec(EXPERT_FF, D),
                pl.BlockSpec((tm, TOP_K), lambda i: (i, 0)), _mod_spec(5, group_fn)]
    in_specs += [packed(functools.partial(lambda i, k: (k * nt + i, 0), k=k)) for k in range(TOP_K)]
    args = [x, h2p, *shared_w, gates, mod_l] + [y_tk] * TOP_K
    final = final_g is not None
    if final:
        in_specs.append(_const_spec(1, D))
        args.append(final_g)
    return pl.pallas_call(
        functools.partial(_combine_kernel, final=final),
        grid=(nt,),
        in_specs=in_specs,
        out_specs=row,
        out_shape=jax.ShapeDtypeStruct((n_tok, D), F32),
        scratch_shapes=[pltpu.VMEM((tm, D), F32)],
        compiler_params=_cparams(("arbitrary",), 48),
        name="moe_combine",
    )(*args)


def _dispatch_plan(idx_t, rank_t, cnt, n_tok):
    bm = MOE_BM
    n_assign = n_tok * TOP_K
    n_blocks = n_assign // bm + N_EXPERTS
    n_rows = n_blocks * bm
    sizes = cnt[:, 0].astype(I32)
    padded = (sizes + bm - 1) // bm * bm
    pad_end = jnp.cumsum(padded)
    pad_start = pad_end - padded
    blk_first = jnp.arange(n_blocks, dtype=I32) * bm
    blk_exp = jnp.minimum(jnp.sum((pad_end[None, :] <= blk_first[:, None]).astype(I32), axis=1), N_EXPERTS - 1)
    n_act = (pad_end[-1] // bm).astype(I32).reshape(1)
    experts = jnp.arange(N_EXPERTS, dtype=I32)
    slot = jnp.sum(jnp.where(idx_t[:, :, None] == experts, pad_start, 0), axis=-1) + rank_t
    inv = jnp.zeros((n_rows,), I32).at[slot.reshape(-1)].add(jnp.arange(1, n_assign + 1, dtype=I32)) - 1
    valid = inv >= 0
    spare = n_assign + jnp.repeat(blk_exp, bm) * bm + jnp.arange(n_rows, dtype=I32) % bm
    src = jnp.where(valid, inv % n_tok, 0).astype(I32)
    dst = jnp.where(valid, inv, spare).astype(I32)
    prev = jnp.concatenate([jnp.full((1,), -1, I32), blk_exp[:-1]])
    run_par = (jnp.cumsum((blk_exp != prev).astype(I32)) - 1) % 2
    nxt_blk = pad_end[blk_exp] // bm
    nxt_exp = jnp.where(nxt_blk < n_act[0], blk_exp[jnp.minimum(nxt_blk, n_blocks - 1)], -1).astype(I32)
    return blk_exp, n_act, run_par.astype(I32), nxt_exp, src, dst


def _moe_layer(layer, x, h2p, logits_t, router_bias, mod_l, exp_w, shared_w, final_g):
    n_tok = h2p.shape[0] // LANE_BLKS
    idx_t, gates_t, rank_t, cnt = _route(logits_t, router_bias, 512)
    plan = _dispatch_plan(idx_t, rank_t, cnt, n_tok)
    y_tk = _moe_experts(layer, h2p, plan, *exp_w)
    return _combine(x, h2p, [w[layer].astype(BF16) for w in shared_w], gates_t.T, mod_l, y_tk, n_tok,
                    final_g, 256)


def _rope_tables():
    rows = SEQ // GRID_W
    row = jnp.broadcast_to(jnp.arange(rows)[:, None], (rows, GRID_W)).reshape(-1).astype(F32)
    col = jnp.broadcast_to(jnp.arange(GRID_W)[None, :], (rows, GRID_W)).reshape(-1).astype(F32)
    inv = ROPE_BASE ** (-jnp.arange(0, AXIS_DIM, 2, dtype=F32) / AXIS_DIM)
    ang_r = row[:, None] * inv
    ang_c = col[:, None] * inv
    ang = jnp.concatenate([ang_r, ang_r, ang_c, ang_c], axis=-1)
    cos, sin = jnp.cos(ang), jnp.sin(ang)
    sign = jnp.where((jnp.arange(HEAD_DIM) % AXIS_DIM) < AXIS_DIM // 2, -1.0, 1.0).astype(F32)
    ident = 512
    cos_t = jnp.concatenate([jnp.tile(cos, (1, 2)), jnp.ones((ident, 128), F32)], axis=0)
    sin_t = jnp.concatenate([jnp.tile(sin * sign, (1, 2)), jnp.zeros((ident, 128), F32)], axis=0)
    return cos_t, sin_t


def _head_perm():
    g = jnp.arange(Q_PER_KV)[:, None, None]
    kvh = jnp.arange(N_KV_HEADS)[None, :, None]
    d = jnp.arange(HEAD_DIM)[None, None, :]
    return ((kvh * Q_PER_KV + g) * HEAD_DIM + d).reshape(-1)


def kernel(x, c, ctx, c_ctx, ada_w, ada_b, norm1_g, norm2_g, even_w_in, gmlp_ln_g, gmlp_ln_b, gmlp_ws, gmlp_bs, pool_w, pool_scale, even_w_out, odd_w_in, conv_w, conv_b, conv_ln_g, conv_ln_b, attn_sink, odd_w_out, router_w, router_bias, exp_w_gate, exp_w_up, exp_w_down, shared_w_gate, shared_w_up, shared_w_down, final_g):
    mod = _ada_mod(c, c_ctx, ada_w, ada_b)
    row = lambda a: a.reshape(1, -1)
    exp_w = (exp_w_gate, exp_w_up, exp_w_down)
    shared_w = (shared_w_gate, shared_w_up, shared_w_down)

    def router_t(i):
        return router_w[i].T.astype(BF16)

    xl = x.reshape(N_LAT, D)
    xc = ctx.reshape(N_CTX, D)

    uv, z = _even_in(xl, xc, mod[0], row(norm1_g[0]), even_w_in[0].astype(BF16), 512)
    y = _even_mix(uv, z, row(gmlp_ln_g[0]), row(gmlp_ln_b[0]), gmlp_ws[0].astype(BF16),
                  gmlp_bs[0].reshape(A_GROUPS, CHUNK, 1), pool_w[0].astype(BF16), row(pool_scale[0]), 256)
    x1, h2p, lg = _mix_out(y, 0, y, 1, even_w_out[0].astype(BF16).reshape(2, HALF, D), xl, xc, N_ALL, mod[0],
                           row(norm2_g[0]), router_t(0), 512)
    x1 = _moe_layer(0, x1, h2p, lg, router_bias[0], mod[0], exp_w, shared_w, None)

    perm = _head_perm()
    w_in1 = odd_w_in[0]
    w_main = jnp.concatenate([w_in1[:, :2 * HALF], w_in1[:, 2 * HALF:3 * HALF][:, perm]], axis=1).astype(BF16)
    w_kv = w_in1[:, 3 * HALF:].astype(BF16)
    w_out1 = odd_w_out[0]
    w_out1 = jnp.stack([w_out1[:HALF], w_out1[HALF:][perm]], axis=0).astype(BF16)
    cos_t, sin_t = _rope_tables()
    zc, q, k, v = _odd_in(x1, mod[1], row(norm1_g[1]), w_main, w_kv, cos_t, sin_t, 512)
    y_conv = _conv_module(zc, N_LAT, conv_w[0], row(conv_b[0]), row(conv_ln_g[0]), row(conv_ln_b[0]), 256)
    y_attn = _attention(q, k, v, attn_sink[0].astype(F32))
    x2, h2p, lg = _mix_out(y_conv, 0, y_attn, 0, w_out1, x1, x1, N_LAT, mod[1], row(norm2_g[1]),
                           router_t(1), 512)
    out = _moe_layer(1, x2, h2p, lg, router_bias[1], mod[1], exp_w, shared_w, row(final_g))
    return out.reshape(BATCH, SEQ, D)
```

```python
import functools

import jax
import jax.numpy as jnp
from jax import lax
from jax.experimental import pallas as pl
from jax.experimental.pallas import tpu as pltpu

F32 = jnp.float32
BF16 = jnp.bfloat16
U32 = jnp.uint32
I32 = jnp.int32

D = 2048
BATCH = 4
SEQ = 4096
DEPTH = 2
GRID_W = 64
CTX_LEN = 256
HALF = D // 2
CHUNK = 128
A_GROUPS = 4
A_GW = HALF // A_GROUPS
POOL_WINDOWS = (2, 4, 8, 16)
B_GW = HALF // len(POOL_WINDOWS)
CONV_W = 31
HEAD_DIM = 64
N_Q_HEADS = HALF // HEAD_DIM
N_KV_HEADS = 2
Q_PER_KV = N_Q_HEADS // N_KV_HEADS
KV_W = N_KV_HEADS * HEAD_DIM
ATT_BLK = 128
WINDOW = 128
AXIS_DIM = HEAD_DIM // 2
ROPE_BASE = 10000.0
N_EXPERTS = 64
N_EXPERT_GROUPS = 8
PER_GROUP = N_EXPERTS // N_EXPERT_GROUPS
TOPK_GROUPS = 4
TOP_K = 8
EXPERT_FF = 512
ROUTED_SCALE = 2.5
EPS = 1e-6

N_LAT = BATCH * SEQ
N_CTX = BATCH * CTX_LEN
N_ALL = N_LAT + N_CTX
CTX_GROUP = BATCH
HALO = 128
CONV_HALO = 16
MOE_BM = 256

TM_PROJ = 512
TM_SEQ = 256
TM_ROUTE = 512
TM_COMBINE = 256

MIB = 1024 * 1024


def _cparams(sem, vmem_mib):
    return pltpu.CompilerParams(dimension_semantics=sem, vmem_limit_bytes=vmem_mib * MIB)


def _dot(a, b):
    return jnp.dot(a, b, preferred_element_type=F32)


def _dot_nt(a, b):
    return lax.dot_general(a, b, (((1,), (1,)), ((), ())), preferred_element_type=F32)


def _rms_mod(x, g, sh, sc):
    y = x * lax.rsqrt(jnp.mean(x * x, axis=-1, keepdims=True) + EPS) * g
    return y * (1.0 + sc) + sh


def _layer_norm(x, g, b):
    mu = jnp.mean(x, axis=-1, keepdims=True)
    xc = x - mu
    var = jnp.mean(xc * xc, axis=-1, keepdims=True)
    return xc * lax.rsqrt(var + EPS) * g + b


def _gelu(x):
    return 0.5 * x * (1.0 + lax.erf(x * (2.0 ** -0.5)))


def _silu(x):
    return x * jax.nn.sigmoid(x)


def _split_bf16(x):
    hi = x.astype(BF16)
    lo = (x - hi.astype(F32)).astype(BF16)
    return hi, lo


def _pack_halves(y):
    n = y.shape[1] // 2
    lo = lax.bitcast_convert_type(y[:, :n].astype(BF16).astype(F32), U32) >> 16
    hi = lax.bitcast_convert_type(y[:, n:].astype(BF16).astype(F32), U32) & U32(0xFFFF0000)
    return hi | lo


def _unpack_halves(p):
    lo = lax.bitcast_convert_type(p << 16, F32)
    hi = lax.bitcast_convert_type(p & U32(0xFFFF0000), F32)
    return lo, hi


def _unpack_bf16(p):
    lo, hi = _unpack_halves(p)
    return jnp.concatenate([lo.astype(BF16), hi.astype(BF16)], axis=1)


LANE_BLKS = HALF // 128


def _store_tile_rows(ref, packed):
    tm = packed.shape[0]
    for s in range(LANE_BLKS):
        ref[pl.ds(s, tm, stride=LANE_BLKS), :] = packed[:, s * 128:(s + 1) * 128]


def _load_tile_rows(ref, tm):
    return jnp.concatenate([ref[pl.ds(s, tm, stride=LANE_BLKS), :] for s in range(LANE_BLKS)], axis=1)


def _mod_spec(chunk, group_fn):
    return pl.BlockSpec((None, None, 1, D), lambda i, *_: (group_fn(i), chunk, 0, 0))


def _group_fn(tm):
    return lambda i: jnp.minimum(i // (SEQ // tm), CTX_GROUP)


def _const_spec(*shape):
    return pl.BlockSpec(shape, lambda *_: (0,) * len(shape))


def _ada_kernel(c_ref, w_ref, b_ref, o_ref):
    s = _silu(c_ref[...]).astype(BF16)
    o_ref[...] = _dot(s, w_ref[...].astype(BF16)) + b_ref[...]


def _ada_mod(c, c_ctx, ada_w, ada_b):
    tn = 1024
    cv = jnp.zeros((8, D), F32).at[:BATCH].set(c).at[CTX_GROUP].set(c_ctx)
    out = pl.pallas_call(
        _ada_kernel,
        grid=(DEPTH, 6 * D // tn),
        in_specs=[pl.BlockSpec((8, D), lambda l, j: (0, 0)),
                  pl.BlockSpec((None, D, tn), lambda l, j: (l, 0, j)),
                  pl.BlockSpec((None, 1, tn), lambda l, j: (l, 0, j))],
        out_specs=pl.BlockSpec((None, 8, tn), lambda l, j: (l, 0, j)),
        out_shape=jax.ShapeDtypeStruct((DEPTH, 8, 6 * D), F32),
        compiler_params=_cparams(("arbitrary", "arbitrary"), 40),
        name="ada_mod",
    )(cv, ada_w, ada_b.reshape(DEPTH, 1, 6 * D))
    return out.reshape(DEPTH, 8, 6, 1, D)


def _two_stream_specs(tm):
    lat_tiles = N_LAT // tm
    return [pl.BlockSpec((tm, D), lambda i: (jnp.minimum(i, lat_tiles - 1), 0)),
            pl.BlockSpec((tm, D), lambda i: (jnp.maximum(i - lat_tiles, 0), 0))]


def _two_stream_rows(xl_ref, xc_ref):
    tm = xl_ref.shape[0]
    return jnp.where(pl.program_id(0) < N_LAT // tm, xl_ref[...], xc_ref[...])


def _even_in_kernel(xl_ref, xc_ref, g_ref, sh_ref, sc_ref, w_ref, uv_ref, z_ref, h_ref):
    h_ref[...] = _rms_mod(_two_stream_rows(xl_ref, xc_ref), g_ref[...], sh_ref[...], sc_ref[...]).astype(BF16)
    for j in range(2):
        cs = slice(j * HALF, (j + 1) * HALF)
        uv_ref[:, cs] = _gelu(_dot(h_ref[...], w_ref[:, cs])).astype(BF16)
    z_ref[...] = _dot(h_ref[...], w_ref[:, 2 * HALF:])


def _resident_spec(*shape):
    return pl.BlockSpec(shape, lambda *_: (0,) * len(shape), pipeline_mode=pl.Buffered(1))


def _even_in(xl, xc, mod_l, norm_g, w_in, tm):
    n_rows = N_ALL
    group_fn = _group_fn(tm)
    return pl.pallas_call(
        _even_in_kernel,
        grid=(n_rows // tm,),
        in_specs=_two_stream_specs(tm) + [
                  _const_spec(1, D),
                  _mod_spec(0, group_fn), _mod_spec(1, group_fn),
                  _resident_spec(D, 3 * HALF)],
        out_specs=[pl.BlockSpec((tm, 2 * HALF), lambda i: (i, 0)),
                   pl.BlockSpec((tm, HALF), lambda i: (i, 0))],
        out_shape=[jax.ShapeDtypeStruct((n_rows, 2 * HALF), BF16),
                   jax.ShapeDtypeStruct((n_rows, HALF), F32)],
        scratch_shapes=[pltpu.VMEM((tm, D), BF16)],
        compiler_params=_cparams(("arbitrary",), 48),
        name="even_in",
    )(xl, xc, norm_g, mod_l, mod_l, w_in)


def _band(d, w):
    inside = lax.bitcast_convert_type(d + w // 2, U32) < U32(w)
    return jnp.where(inside, 1.0, 0.0).astype(BF16)


def _seq_tile(i, tm):
    lat_tiles = N_LAT // tm
    is_lat = i < lat_tiles
    it = i % (SEQ // tm)
    first = jnp.logical_or(jnp.logical_not(is_lat), it == 0)
    last = jnp.logical_or(jnp.logical_not(is_lat), it == SEQ // tm - 1)
    pos0 = jnp.where(is_lat, it * tm, 0)
    seq_len = jnp.where(is_lat, SEQ, CTX_LEN)
    return first, last, pos0, seq_len


def _even_mix_kernel(u_ref, v_ref, z_ref, zp_ref, zn_ref, lng_ref, lnb_ref, ws_ref, bs_ref,
                     wp_ref, ps_ref, y_ref, *, tm):
    first, last, pos0, seq_len = _seq_tile(pl.program_id(0), tm)
    vn = _layer_norm(v_ref[...].astype(F32), lng_ref[...], lnb_ref[...]).astype(BF16)
    for g in range(A_GROUPS):
        cs = slice(g * A_GW, (g + 1) * A_GW)
        for c in range(tm // CHUNK):
            rs = slice(c * CHUNK, (c + 1) * CHUNK)
            mixed = _dot(ws_ref[g], vn[rs, cs]) + bs_ref[g]
            y_ref[rs, cs] = (u_ref[rs, cs].astype(F32) * mixed).astype(BF16)
    z = z_ref[...]
    zp = jnp.where(first, 0.0, zp_ref[...])
    zn = jnp.where(last, 0.0, zn_ref[...])
    z_hi, z_lo = _split_bf16(z)
    zp_hi, zp_lo = _split_bf16(zp)
    zn_hi, zn_lo = _split_bf16(zn)
    d_main = (lax.broadcasted_iota(I32, (tm, tm), 1) - lax.broadcasted_iota(I32, (tm, tm), 0))
    d_halo = (lax.broadcasted_iota(I32, (tm, HALO), 1) - lax.broadcasted_iota(I32, (tm, HALO), 0))
    pos = pos0 + lax.broadcasted_iota(I32, (tm, 1), 0)
    for g, w in enumerate(POOL_WINDOWS):
        cs = slice(g * B_GW, (g + 1) * B_GW)
        bm_ = _band(d_main, w)
        bp = _band(d_halo - HALO, w)
        bn = _band(d_halo + tm, w)
        tot = (_dot(bm_, z_hi[:, cs]) + _dot(bm_, z_lo[:, cs])
               + _dot(bp, zp_hi[:, cs]) + _dot(bp, zp_lo[:, cs])
               + _dot(bn, zn_hi[:, cs]) + _dot(bn, zn_lo[:, cs]))
        cnt = (jnp.minimum(pos + w // 2, seq_len) - jnp.maximum(pos - w // 2, 0)).astype(F32)
        pooled = (tot / cnt - z[:, cs]).astype(BF16)
        y_ref[:, HALF + g * B_GW:HALF + (g + 1) * B_GW] = (
            _dot(pooled, wp_ref[g]) * ps_ref[:, cs]).astype(BF16)


def _even_mix(uv, z, ln_g, ln_b, ws, bs, wp, ps, tm):
    n_rows = z.shape[0]
    hb = tm // HALO
    n_hblk = n_rows // HALO
    return pl.pallas_call(
        functools.partial(_even_mix_kernel, tm=tm),
        grid=(n_rows // tm,),
        in_specs=[pl.BlockSpec((tm, HALF), lambda i: (i, 0)),
                  pl.BlockSpec((tm, HALF), lambda i: (i, 1)),
                  pl.BlockSpec((tm, HALF), lambda i: (i, 0)),
                  pl.BlockSpec((HALO, HALF), lambda i: (jnp.maximum(i * hb - 1, 0), 0)),
                  pl.BlockSpec((HALO, HALF), lambda i: (jnp.minimum((i + 1) * hb, n_hblk - 1), 0)),
                  _const_spec(1, HALF), _const_spec(1, HALF),
                  _const_spec(A_GROUPS, CHUNK, CHUNK), _const_spec(A_GROUPS, CHUNK, 1),
                  _const_spec(len(POOL_WINDOWS), B_GW, B_GW), _const_spec(1, HALF)],
        out_specs=pl.BlockSpec((tm, D), lambda i: (i, 0)),
        out_shape=jax.ShapeDtypeStruct((n_rows, D), BF16),
        compiler_params=_cparams(("arbitrary",), 40),
        name="even_mix",
    )(uv, uv, z, z, z, ln_g, ln_b, ws, bs, wp, ps)


def _mix_out_kernel(ya_ref, yb_ref, w_ref, xl_ref, xc_ref, g1_ref, n2_ref, sh_ref, sc_ref, rw_ref,
                    xo_ref, hp_ref, lg_ref):
    o = _dot(ya_ref[...], w_ref[0]) + _dot(yb_ref[...], w_ref[1])
    xn = _two_stream_rows(xl_ref, xc_ref) + g1_ref[...] * o
    xo_ref[...] = xn
    h = _rms_mod(xn, n2_ref[...], sh_ref[...], sc_ref[...])
    _store_tile_rows(hp_ref, _pack_halves(h))
    lg_ref[...] = _dot_nt(rw_ref[...], h.astype(BF16))


def _mix_out(ya, ya_col, yb, yb_col, w_out, xl, xc, n_rows, mod_l, norm2_g, rw, tm):
    group_fn = _group_fn(tm)
    return pl.pallas_call(
        _mix_out_kernel,
        grid=(n_rows // tm,),
        in_specs=[pl.BlockSpec((tm, HALF), lambda i: (i, ya_col)),
                  pl.BlockSpec((tm, HALF), lambda i: (i, yb_col)),
                  _resident_spec(2, HALF, D)] + _two_stream_specs(tm) + [
                  _mod_spec(2, group_fn), _const_spec(1, D), _mod_spec(3, group_fn), _mod_spec(4, group_fn),
                  _const_spec(N_EXPERTS, D)],
        out_specs=[pl.BlockSpec((tm, D), lambda i: (i, 0)),
                   pl.BlockSpec((tm * LANE_BLKS, 128), lambda i: (i, 0)),
                   pl.BlockSpec((N_EXPERTS, tm), lambda i: (0, i))],
        out_shape=[jax.ShapeDtypeStruct((n_rows, D), F32),
                   jax.ShapeDtypeStruct((n_rows * LANE_BLKS, 128), U32),
                   jax.ShapeDtypeStruct((N_EXPERTS, n_rows), F32)],
        compiler_params=_cparams(("arbitrary",), 56),
        name="mix_out",
    )(ya, yb, w_out, xl, xc, mod_l, norm2_g, mod_l, mod_l, rw)


def _rope(x, cos, sin_signed, first_half):
    partner = jnp.where(first_half, pltpu.roll(x, 128 - AXIS_DIM // 2, 1), pltpu.roll(x, AXIS_DIM // 2, 1))
    return x * cos + partner * sin_signed


def _odd_in_kernel(x_ref, g_ref, sh_ref, sc_ref, w_ref, wkv_ref, cos_ref, sin_ref,
                   zc_ref, q_ref, k_ref, v_ref, h_ref):
    tm = x_ref.shape[0]
    h_ref[...] = _rms_mod(x_ref[...], g_ref[...], sh_ref[...], sc_ref[...]).astype(BF16)
    first_half = (lax.broadcasted_iota(I32, (tm, 128), 1) % AXIS_DIM) < (AXIS_DIM // 2)
    cos = cos_ref[...]
    sin = sin_ref[...]
    for b in range(HALF // 256):
        a = _dot(h_ref[...], w_ref[:, b * 256:(b + 1) * 256])
        gate = _dot(h_ref[...], w_ref[:, HALF + b * 256:HALF + (b + 1) * 256])
        zc_ref[:, b * 256:(b + 1) * 256] = a * jax.nn.sigmoid(gate)
    for b in range(HALF // 256):
        q = _dot(h_ref[...], w_ref[:, 2 * HALF + b * 256:2 * HALF + (b + 1) * 256])
        for s in range(2):
            cs = slice(b * 256 + s * 128, b * 256 + (s + 1) * 128)
            q_ref[:, cs] = (_rope(q[:, s * 128:(s + 1) * 128], cos, sin, first_half)
                            * (HEAD_DIM ** -0.5)).astype(BF16)
    kv = _dot(h_ref[...], wkv_ref[...])
    k_ref[...] = _rope(kv[:, :KV_W], cos, sin, first_half).astype(BF16)
    v_ref[...] = kv[:, KV_W:].astype(BF16)


def _odd_in(x, mod_l, norm_g, w_main, w_kv, cos_t, sin_t, tm):
    n_rows = x.shape[0]
    group_fn = _group_fn(tm)
    lat_tiles = N_LAT // tm
    pos_blk = lambda i: (jnp.where(i < lat_tiles, i % (SEQ // tm), SEQ // tm), 0)
    row = lambda w: pl.BlockSpec((tm, w), lambda i: (i, 0))
    return pl.pallas_call(
        _odd_in_kernel,
        grid=(n_rows // tm,),
        in_specs=[pl.BlockSpec((tm, D), lambda i: (i, 0)),
                  _const_spec(1, D),
                  _mod_spec(0, group_fn), _mod_spec(1, group_fn),
                  _resident_spec(D, 3 * HALF),
                  _resident_spec(D, 2 * KV_W),
                  pl.BlockSpec((tm, 128), pos_blk), pl.BlockSpec((tm, 128), pos_blk)],
        out_specs=[row(HALF), row(HALF), row(KV_W), row(KV_W)],
        out_shape=[jax.ShapeDtypeStruct((n_rows, HALF), F32),
                   jax.ShapeDtypeStruct((n_rows, HALF), BF16),
                   jax.ShapeDtypeStruct((n_rows, KV_W), BF16),
                   jax.ShapeDtypeStruct((n_rows, KV_W), BF16)],
        scratch_shapes=[pltpu.VMEM((tm, D), BF16)],
        compiler_params=_cparams(("arbitrary",), 48),
        name="odd_in",
    )(x, norm_g, mod_l, mod_l, w_main, w_kv, cos_t, sin_t)


def _conv_kernel(z_ref, zp_ref, zn_ref, w_ref, b_ref, lng_ref, lnb_ref, y_ref, ze_ref, zs_ref, c_ref, *, tm):
    first, last, _, _ = _seq_tile(pl.program_id(0), tm)
    ze_ref[0:CONV_HALO, :] = jnp.where(first, 0.0, zp_ref[...])
    ze_ref[CONV_HALO:CONV_HALO + tm, :] = z_ref[...]
    ze_ref[CONV_HALO + tm:, :] = jnp.where(last, 0.0, zn_ref[...])
    rc = 64
    base = CONV_HALO - CONV_W // 2
    n_sh = zs_ref.shape[1]
    for b in range(8):
        zs_ref[b] = ze_ref[b:b + n_sh, :]

    def lane_block(cb, _):
        cs = pl.ds(pl.multiple_of(cb * 128, 128), 128)
        for r in range(tm // rc):
            acc = jnp.zeros((rc, 128), F32)
            for t in range(CONV_W):
                off = base + t
                start = r * rc + (off // 8) * 8
                acc = acc + w_ref[t:t + 1, cs] * zs_ref[off % 8, start:start + rc, cs]
            c_ref[r * rc:(r + 1) * rc, cs] = acc
        return 0

    lax.fori_loop(0, HALF // 128, lane_block, 0)
    y = _layer_norm(c_ref[...] + b_ref[...], lng_ref[...], lnb_ref[...])
    y_ref[...] = _silu(y).astype(BF16)


def _conv_module(zc, n_rows, conv_w, conv_b, ln_g, ln_b, tm):
    hb = tm // CONV_HALO
    n_hblk = zc.shape[0] // CONV_HALO
    return pl.pallas_call(
        functools.partial(_conv_kernel, tm=tm),
        grid=(n_rows // tm,),
        in_specs=[pl.BlockSpec((tm, HALF), lambda i: (i, 0)),
                  pl.BlockSpec((CONV_HALO, HALF), lambda i: (jnp.maximum(i * hb - 1, 0), 0)),
                  pl.BlockSpec((CONV_HALO, HALF), lambda i: (jnp.minimum((i + 1) * hb, n_hblk - 1), 0)),
                  _const_spec(CONV_W, HALF), _const_spec(1, HALF), _const_spec(1, HALF), _const_spec(1, HALF)],
        out_specs=pl.BlockSpec((tm, HALF), lambda i: (i, 0)),
        out_shape=jax.ShapeDtypeStruct((n_rows, HALF), BF16),
        scratch_shapes=[pltpu.VMEM((tm + 2 * CONV_HALO, HALF), F32),
                        pltpu.VMEM((8, tm + 2 * CONV_HALO - 8, HALF), F32),
                        pltpu.VMEM((tm, HALF), F32)],
        compiler_params=_cparams(("arbitrary",), 40),
        name="conv_module",
    )(zc, zc, zc, conv_w, conv_b, ln_g, ln_b)


def _attn_kernel(sink_ref, q_ref, kp_ref, kc_ref, kn_ref, vp_ref, vc_ref, vn_ref, kx_ref, vx_ref, o_ref):
    i = pl.program_id(1)
    n_keys = 3 * ATT_BLK + CTX_LEN
    kb = jnp.concatenate([kp_ref[...], kc_ref[...], kn_ref[...], kx_ref[...]], axis=0)
    vb = jnp.concatenate([vp_ref[...], vc_ref[...], vn_ref[...], vx_ref[...]], axis=0)
    key_lane = lax.broadcasted_iota(I32, (n_keys, 2 * HEAD_DIM), 1)
    k_head = [jnp.where(key_lane < HEAD_DIM, kb, jnp.zeros_like(kb)),
              jnp.where(key_lane >= HEAD_DIM, kb, jnp.zeros_like(kb))]
    r = lax.broadcasted_iota(I32, (ATT_BLK, n_keys), 0)
    c = lax.broadcasted_iota(I32, (ATT_BLK, n_keys), 1)
    kpos = c + (i - 1) * ATT_BLK
    band_ok = lax.bitcast_convert_type(c - r, U32) <= U32(2 * WINDOW)
    in_seq = lax.bitcast_convert_type(kpos, U32) < U32(SEQ)
    bias = jnp.where(c >= 3 * ATT_BLK, 0.0, jnp.where(band_ok, jnp.where(in_seq, 0.0, -jnp.inf), -jnp.inf))
    gs = 4
    rows = gs * ATT_BLK
    out_lane = lax.broadcasted_iota(I32, (rows, 2 * HEAD_DIM), 1)
    bias_s = jnp.concatenate([bias] * gs, axis=0)
    for g0 in range(0, Q_PER_KV, gs):
        qs = jnp.concatenate([q_ref[:, g * 128:(g + 1) * 128] for g in range(g0, g0 + gs)], axis=0)
        outs = []
        for kvh in range(N_KV_HEADS):
            sk = jnp.concatenate([jnp.full((ATT_BLK, 1), sink_ref[kvh * Q_PER_KV + g], F32)
                                  for g in range(g0, g0 + gs)], axis=0)
            s = _dot_nt(qs, k_head[kvh]) + bias_s
            m = jnp.maximum(jnp.max(s, axis=-1, keepdims=True), sk)
            e = jnp.exp(s - m)
            den = jnp.sum(e, axis=-1, keepdims=True) + jnp.exp(sk - m)
            outs.append(_dot(e.astype(BF16), vb) / den)
        o = jnp.where(out_lane < HEAD_DIM, outs[0], outs[1]).astype(BF16)
        for j in range(gs):
            o_ref[:, (g0 + j) * 128:(g0 + j + 1) * 128] = o[j * ATT_BLK:(j + 1) * ATT_BLK, :]


def _attention(q, k, v, sink):
    nb = SEQ // ATT_BLK
    ctx0 = N_LAT // CTX_LEN
    blk = lambda w, fn: pl.BlockSpec((ATT_BLK, w), fn)
    prev = lambda b, i, s: (b * nb + jnp.maximum(i - 1, 0), 0)
    cur = lambda b, i, s: (b * nb + i, 0)
    nxt = lambda b, i, s: (b * nb + jnp.minimum(i + 1, nb - 1), 0)
    ctx = pl.BlockSpec((CTX_LEN, KV_W), lambda b, i, s: (ctx0 + b, 0))
    grid_spec = pltpu.PrefetchScalarGridSpec(
        num_scalar_prefetch=1,
        grid=(BATCH, nb),
        in_specs=[blk(HALF, cur), blk(KV_W, prev), blk(KV_W, cur), blk(KV_W, nxt),
                  blk(KV_W, prev), blk(KV_W, cur), blk(KV_W, nxt), ctx, ctx],
        out_specs=blk(HALF, cur),
    )
    return pl.pallas_call(
        _attn_kernel,
        grid_spec=grid_spec,
        out_shape=jax.ShapeDtypeStruct((N_LAT, HALF), BF16),
        compiler_params=_cparams(("arbitrary", "arbitrary"), 32),
        name="attention",
    )(sink, q, k, k, k, v, v, v, k, v)


def _first_argmax(x, iota, n):
    m = jnp.max(x, axis=0, keepdims=True)
    first = jnp.min(jnp.where(x == m, iota, n), axis=0, keepdims=True)
    return m, first


def _route_kernel(lg_ref, bias_ref, idx_ref, gate_ref, rank_ref, cnt_ref, carry_ref):
    tm = lg_ref.shape[1]

    @pl.when(pl.program_id(0) == 0)
    def _():
        carry_ref[...] = jnp.zeros_like(carry_ref)

    scores = jax.nn.sigmoid(lg_ref[...])
    biased = scores + bias_ref[...]
    sub = lax.broadcasted_iota(I32, (PER_GROUP, tm), 0)
    blocks = [biased[g * PER_GROUP:(g + 1) * PER_GROUP, :] for g in range(N_EXPERT_GROUPS)]
    gs = []
    for blk in blocks:
        m1, f1 = _first_argmax(blk, sub, PER_GROUP)
        m2 = jnp.max(jnp.where(sub == f1, -jnp.inf, blk), axis=0, keepdims=True)
        gs.append(m1 + m2)
    gs = jnp.concatenate(gs, axis=0)
    giota = lax.broadcasted_iota(I32, (N_EXPERT_GROUPS, tm), 0)
    keep = jnp.zeros((N_EXPERT_GROUPS, tm), F32)
    for _ in range(TOPK_GROUPS):
        _, f = _first_argmax(gs, giota, N_EXPERT_GROUPS)
        hit = giota == f
        keep = jnp.where(hit, 1.0, keep)
        gs = jnp.where(hit, -jnp.inf, gs)
    cur = jnp.concatenate([jnp.where(keep[g:g + 1, :] > 0.0, blocks[g], -jnp.inf)
                           for g in range(N_EXPERT_GROUPS)], axis=0)
    eiota = lax.broadcasted_iota(I32, (N_EXPERTS, tm), 0)
    chosen = jnp.zeros((N_EXPERTS, tm), F32)
    idx, sel = [], []
    for _ in range(TOP_K):
        _, f = _first_argmax(cur, eiota, N_EXPERTS)
        hit = eiota == f
        idx.append(f)
        sel.append(jnp.sum(jnp.where(hit, scores, 0.0), axis=0, keepdims=True))
        cur = jnp.where(hit, -jnp.inf, cur)
        chosen = jnp.where(hit, 1.0, chosen)
    sel = jnp.concatenate(sel, axis=0)
    idx = jnp.concatenate(idx, axis=0)
    gate_ref[...] = sel / jnp.sum(sel, axis=0, keepdims=True) * ROUTED_SCALE
    idx_ref[...] = idx
    before = jnp.where(lax.broadcasted_iota(I32, (tm, tm), 0) < lax.broadcasted_iota(I32, (tm, tm), 1), 1.0, 0.0)
    rank = _dot(chosen.astype(BF16), before.astype(BF16)) + carry_ref[:, 0:1]
    rank_ref[...] = jnp.concatenate(
        [jnp.sum(jnp.where(eiota == idx[k:k + 1, :], rank, 0.0), axis=0, keepdims=True) for k in range(TOP_K)],
        axis=0).astype(I32)
    carry_ref[...] = carry_ref[...] + jnp.sum(chosen, axis=1, keepdims=True)
    cnt_ref[...] = carry_ref[...]


def _route(logits_t, router_bias, tm):
    n_tok = logits_t.shape[1]
    tok = lambda rows: pl.BlockSpec((rows, tm), lambda i: (0, i))
    return pl.pallas_call(
        _route_kernel,
        grid=(n_tok // tm,),
        in_specs=[tok(N_EXPERTS), _const_spec(N_EXPERTS, 1)],
        out_specs=[tok(TOP_K), tok(TOP_K), tok(TOP_K), _const_spec(N_EXPERTS, 128)],
        out_shape=[jax.ShapeDtypeStruct((TOP_K, n_tok), I32),
                   jax.ShapeDtypeStruct((TOP_K, n_tok), F32),
                   jax.ShapeDtypeStruct((TOP_K, n_tok), I32),
                   jax.ShapeDtypeStruct((N_EXPERTS, 128), F32)],
        scratch_shapes=[pltpu.VMEM((N_EXPERTS, 128), F32)],
        compiler_params=_cparams(("arbitrary",), 32),
        name="route",
    )(logits_t, router_bias.astype(F32).reshape(N_EXPERTS, 1))


def _moe_kernel(be_ref, na_ref, rp_ref, nx_ref, src_ref, srcn_ref, dst_ref, h_hbm, wg_hbm, wu_hbm, wd_hbm, y_hbm,
                xb, yb, wgf, wuf, wdf, wgb, wub, wdb, sem_in, sem_out, sem_w, *, layer):
    i = pl.program_id(0)
    n_steps = pl.num_programs(0)
    n_act = na_ref[0]
    cur = i % 2
    oth = 1 - cur
    bm = xb.shape[1] // LANE_BLKS

    def tile(ref, row):
        start = row * LANE_BLKS
        if not isinstance(row, int):
            start = pl.multiple_of(start, LANE_BLKS)
        return ref.at[pl.ds(start, LANE_BLKS), :]

    def gather(idx_ref, s):
        for r in range(bm):
            pltpu.make_async_copy(tile(h_hbm, idx_ref[0, 0, r]), tile(xb.at[s], r), sem_in.at[s]).start()

    def scatter(idx_ref, s):
        for r in range(bm):
            pltpu.make_async_copy(tile(yb.at[s], r), tile(y_hbm, idx_ref[0, 0, r]), sem_out.at[s]).start()

    def wait_gather(s):
        pltpu.make_async_copy(h_hbm.at[pl.ds(0, bm * LANE_BLKS), :], xb.at[s], sem_in.at[s]).wait()

    def wait_scatter(s):
        pltpu.make_async_copy(yb.at[s], y_hbm.at[pl.ds(0, bm * LANE_BLKS), :], sem_out.at[s]).wait()

    def weight_copies(e, s):
        return [pltpu.make_async_copy(w_hbm.at[layer, e], stage.at[s], sem_w.at[s])
                for w_hbm, stage in ((wg_hbm, wgf), (wu_hbm, wuf), (wd_hbm, wdf))]

    @pl.when(i == 0)
    def _():
        gather(src_ref, 0)
        for cp in weight_copies(be_ref[0], 0):
            cp.start(priority=1)

    @pl.when(i < n_act)
    def _():
        wait_gather(cur)

        @pl.when(i >= 2)
        def _():
            wait_scatter(cur)

        @pl.when(jnp.logical_or(i == 0, be_ref[i] != be_ref[jnp.maximum(i - 1, 0)]))
        def _():
            s = rp_ref[i]
            for cp in weight_copies(be_ref[i], s):
                cp.wait()

            @pl.when(nx_ref[i] >= 0)
            def _():
                for cp in weight_copies(nx_ref[i], 1 - s):
                    cp.start(priority=1)

            wgb[...] = wgf[s].astype(BF16)
            wub[...] = wuf[s].astype(BF16)
            wdb[...] = wdf[s].astype(BF16)

        gather(srcn_ref, oth)
        x = _unpack_bf16(_load_tile_rows(xb.at[cur], bm))
        hid = (_silu(_dot(x, wgb[...])) * _dot(x, wub[...])).astype(BF16)
        _store_tile_rows(yb.at[cur], _pack_halves(_dot(hid, wdb[...])))
        scatter(dst_ref, cur)

    @pl.when(i == n_steps - 1)
    def _():
        wait_gather(n_act % 2)
        wait_scatter((n_act + 1) % 2)

        @pl.when(n_act >= 2)
        def _():
            wait_scatter(n_act % 2)


def _moe_experts(layer, h2p, plan, w_gate, w_up, w_down):
    blk_exp, n_act, run_par, nxt_exp, src, dst = plan
    n_blocks = blk_exp.shape[0]
    bm = MOE_BM
    smem_blk = lambda fn: pl.BlockSpec((1, 1, bm), fn, memory_space=pltpu.SMEM)
    hbm = pl.BlockSpec(memory_space=pl.ANY)
    grid_spec = pltpu.PrefetchScalarGridSpec(
        num_scalar_prefetch=4,
        grid=(n_blocks,),
        in_specs=[smem_blk(lambda i, *_: (i, 0, 0)),
                  smem_blk(lambda i, *_: (jnp.minimum(i + 1, n_blocks - 1), 0, 0)),
                  smem_blk(lambda i, *_: (i, 0, 0)),
                  hbm, hbm, hbm, hbm],
        out_specs=hbm,
        scratch_shapes=[pltpu.VMEM((2, bm * LANE_BLKS, 128), U32), pltpu.VMEM((2, bm * LANE_BLKS, 128), U32),
                        pltpu.VMEM((2, D, EXPERT_FF), F32), pltpu.VMEM((2, D, EXPERT_FF), F32),
                        pltpu.VMEM((2, EXPERT_FF, D), F32),
                        pltpu.VMEM((D, EXPERT_FF), BF16), pltpu.VMEM((D, EXPERT_FF), BF16),
                        pltpu.VMEM((EXPERT_FF, D), BF16),
                        pltpu.SemaphoreType.DMA((2,)), pltpu.SemaphoreType.DMA((2,)),
                        pltpu.SemaphoreType.DMA((2,))],
    )
    src3 = src.reshape(n_blocks, 1, bm)
    return pl.pallas_call(
        functools.partial(_moe_kernel, layer=layer),
        grid_spec=grid_spec,
        out_shape=jax.ShapeDtypeStruct((n_blocks * bm * LANE_BLKS, 128), U32),
        compiler_params=_cparams(("arbitrary",), 48),
        name="moe_experts",
    )(blk_exp, n_act, run_par, nxt_exp, src3, src3, dst.reshape(n_blocks, 1, bm), h2p, w_gate, w_up, w_down)


def _combine_kernel(*refs, final):
    x_ref, h_ref, wg_ref, wu_ref, wd_ref, gt_ref, g2_ref = refs[:7]
    y_refs = refs[7:7 + TOP_K]
    fg_ref = refs[7 + TOP_K] if final else None
    o_ref, sh_ref = refs[-2:]
    tm = x_ref.shape[0]
    xh = _unpack_bf16(_load_tile_rows(h_ref, tm))
    hid = (_silu(_dot(xh, wg_ref[...])) * _dot(xh, wu_ref[...])).astype(BF16)
    sh_ref[...] = _dot(hid, wd_ref[...])
    gt = gt_ref[...]
    for b in range(LANE_BLKS):
        c_lo = slice(b * 128, (b + 1) * 128)
        c_hi = slice(HALF + b * 128, HALF + (b + 1) * 128)
        acc_lo = sh_ref[:, c_lo]
        acc_hi = sh_ref[:, c_hi]
        for k in range(TOP_K):
            lo, hi = _unpack_halves(y_refs[k][pl.ds(b, tm, stride=LANE_BLKS), :])
            acc_lo = acc_lo + lo * gt[:, k:k + 1]
            acc_hi = acc_hi + hi * gt[:, k:k + 1]
        o_ref[:, c_lo] = x_ref[:, c_lo] + g2_ref[:, c_lo] * acc_lo
        o_ref[:, c_hi] = x_ref[:, c_hi] + g2_ref[:, c_hi] * acc_hi
    if final:
        xn = o_ref[...]
        o_ref[...] = xn * lax.rsqrt(jnp.mean(xn * xn, axis=-1, keepdims=True) + EPS) * fg_ref[...]


def _combine(x, h2p, shared_w, gates, mod_l, y_tk, n_tok, final_g, tm):
    nt = n_tok // tm
    group_fn = _group_fn(tm)
    row = pl.BlockSpec((tm, D), lambda i: (i, 0))
    packed = lambda fn: pl.BlockSpec((tm * LANE_BLKS, 128), fn)
    in_specs = [row, packed(lambda i: (i, 0)),
                _resident_spec(D, EXPERT_FF), _resident_spec(D, EXPERT_FF), _resident_spec(EXPERT_FF, D),
                pl.BlockSpec((tm, TOP_K), lambda i: (i, 0)), _mod_spec(5, group_fn)]
    in_specs += [packed(functools.partial(lambda i, k: (k * nt + i, 0), k=k)) for k in range(TOP_K)]
    args = [x, h2p, *shared_w, gates, mod_l] + [y_tk] * TOP_K
    final = final_g is not None
    if final:
        in_specs.append(_const_spec(1, D))
        args.append(final_g)
    return pl.pallas_call(
        functools.partial(_combine_kernel, final=final),
        grid=(nt,),
        in_specs=in_specs,
        out_specs=row,
        out_shape=jax.ShapeDtypeStruct((n_tok, D), F32),
        scratch_shapes=[pltpu.VMEM((tm, D), F32)],
        compiler_params=_cparams(("arbitrary",), 48),
        name="moe_combine",
    )(*args)


def _dispatch_plan(idx_t, rank_t, cnt, n_tok):
    bm = MOE_BM
    n_assign = n_tok * TOP_K
    n_blocks = n_assign // bm + N_EXPERTS
    n_rows = n_blocks * bm
    sizes = cnt[:, 0].astype(I32)
    padded = (sizes + bm - 1) // bm * bm
    pad_end = jnp.cumsum(padded)
    pad_start = pad_end - padded
    blk_first = jnp.arange(n_blocks, dtype=I32) * bm
    blk_exp = jnp.minimum(jnp.sum((pad_end[None, :] <= blk_first[:, None]).astype(I32), axis=1), N_EXPERTS - 1)
    n_act = (pad_end[-1] // bm).astype(I32).reshape(1)
    experts = jnp.arange(N_EXPERTS, dtype=I32)
    slot = jnp.sum(jnp.where(idx_t[:, :, None] == experts, pad_start, 0), axis=-1) + rank_t
    ids = (jnp.arange(n_tok, dtype=I32)[None, :] * TOP_K + jnp.arange(TOP_K, dtype=I32)[:, None] + 1)
    inv = jnp.zeros((n_rows,), I32).at[slot.reshape(-1)].add(ids.reshape(-1)) - 1
    valid = inv >= 0
    tok = inv // TOP_K
    spare = n_assign + jnp.repeat(blk_exp, bm) * bm + (jnp.arange(n_rows, dtype=I32) & (bm - 1))
    src = jnp.where(valid, tok, 0).astype(I32)
    dst = jnp.where(valid, (inv - tok * TOP_K) * n_tok + tok, spare).astype(I32)
    prev = jnp.concatenate([jnp.full((1,), -1, I32), blk_exp[:-1]])
    run_par = (jnp.cumsum((blk_exp != prev).astype(I32)) - 1) % 2
    nxt_blk = pad_end[blk_exp] // bm
    nxt_exp = jnp.where(nxt_blk < n_act[0], blk_exp[jnp.minimum(nxt_blk, n_blocks - 1)], -1).astype(I32)
    return blk_exp, n_act, run_par.astype(I32), nxt_exp, src, dst


def _moe_layer(layer, x, h2p, logits_t, router_bias, mod_l, exp_w, shared_w, final_g):
    n_tok = h2p.shape[0] // LANE_BLKS
    idx_t, gates_t, rank_t, cnt = _route(logits_t, router_bias, TM_ROUTE)
    plan = _dispatch_plan(idx_t, rank_t, cnt, n_tok)
    y_tk = _moe_experts(layer, h2p, plan, *exp_w)
    return _combine(x, h2p, [w[layer].astype(BF16) for w in shared_w], gates_t.T, mod_l, y_tk, n_tok,
                    final_g, TM_COMBINE)


def _rope_tables():
    rows = SEQ // GRID_W
    row = jnp.broadcast_to(jnp.arange(rows)[:, None], (rows, GRID_W)).reshape(-1).astype(F32)
    col = jnp.broadcast_to(jnp.arange(GRID_W)[None, :], (rows, GRID_W)).reshape(-1).astype(F32)
    inv = ROPE_BASE ** (-jnp.arange(0, AXIS_DIM, 2, dtype=F32) / AXIS_DIM)
    ang_r = row[:, None] * inv
    ang_c = col[:, None] * inv
    ang = jnp.concatenate([ang_r, ang_r, ang_c, ang_c], axis=-1)
    cos, sin = jnp.cos(ang), jnp.sin(ang)
    sign = jnp.where((jnp.arange(HEAD_DIM) % AXIS_DIM) < AXIS_DIM // 2, -1.0, 1.0).astype(F32)
    ident = TM_PROJ
    cos_t = jnp.concatenate([jnp.tile(cos, (1, 2)), jnp.ones((ident, 128), F32)], axis=0)
    sin_t = jnp.concatenate([jnp.tile(sin * sign, (1, 2)), jnp.zeros((ident, 128), F32)], axis=0)
    return cos_t, sin_t


def _heads_g_major(w, axis):
    shape = w.shape
    w = w.reshape(shape[:axis] + (N_KV_HEADS, Q_PER_KV, HEAD_DIM) + shape[axis + 1:])
    return jnp.swapaxes(w, axis, axis + 1).reshape(shape)


def kernel(x, c, ctx, c_ctx, ada_w, ada_b, norm1_g, norm2_g, even_w_in, gmlp_ln_g, gmlp_ln_b, gmlp_ws, gmlp_bs, pool_w, pool_scale, even_w_out, odd_w_in, conv_w, conv_b, conv_ln_g, conv_ln_b, attn_sink, odd_w_out, router_w, router_bias, exp_w_gate, exp_w_up, exp_w_down, shared_w_gate, shared_w_up, shared_w_down, final_g):
    mod = _ada_mod(c, c_ctx, ada_w, ada_b)
    row = lambda a: a.reshape(1, -1)
    exp_w = (exp_w_gate, exp_w_up, exp_w_down)
    shared_w = (shared_w_gate, shared_w_up, shared_w_down)

    def router_t(i):
        return router_w[i].T.astype(BF16)

    xl = x.reshape(N_LAT, D)
    xc = ctx.reshape(N_CTX, D)

    uv, z = _even_in(xl, xc, mod[0], row(norm1_g[0]), even_w_in[0].astype(BF16), TM_PROJ)
    y = _even_mix(uv, z, row(gmlp_ln_g[0]), row(gmlp_ln_b[0]), gmlp_ws[0].astype(BF16),
                  gmlp_bs[0].reshape(A_GROUPS, CHUNK, 1), pool_w[0].astype(BF16), row(pool_scale[0]), TM_SEQ)
    x1, h2p, lg = _mix_out(y, 0, y, 1, even_w_out[0].astype(BF16).reshape(2, HALF, D), xl, xc, N_ALL, mod[0],
                           row(norm2_g[0]), router_t(0), TM_PROJ)
    x1 = _moe_layer(0, x1, h2p, lg, router_bias[0], mod[0], exp_w, shared_w, None)

    w_in1 = odd_w_in[0]
    w_main = jnp.concatenate([w_in1[:, :2 * HALF], _heads_g_major(w_in1[:, 2 * HALF:3 * HALF], 1)],
                             axis=1).astype(BF16)
    w_kv = w_in1[:, 3 * HALF:].astype(BF16)
    w_out1 = odd_w_out[0]
    w_out1 = jnp.stack([w_out1[:HALF], _heads_g_major(w_out1[HALF:], 0)], axis=0).astype(BF16)
    cos_t, sin_t = _rope_tables()
    zc, q, k, v = _odd_in(x1, mod[1], row(norm1_g[1]), w_main, w_kv, cos_t, sin_t, TM_PROJ)
    y_conv = _conv_module(zc, N_LAT, conv_w[0], row(conv_b[0]), row(conv_ln_g[0]), row(conv_ln_b[0]), TM_SEQ)
    y_attn = _attention(q, k, v, attn_sink[0].astype(F32))
    x2, h2p, lg = _mix_out(y_conv, 0, y_attn, 0, w_out1, x1, x1, N_LAT, mod[1], row(norm2_g[1]),
                           router_t(1), TM_PROJ)
    out = _moe_layer(1, x2, h2p, lg, router_bias[1], mod[1], exp_w, shared_w, row(final_g))
    return out.reshape(BATCH, SEQ, D)
```

```python
import functools

import jax
import jax.numpy as jnp
from jax import lax
from jax.experimental import pallas as pl
from jax.experimental.pallas import tpu as pltpu

F32 = jnp.float32
BF16 = jnp.bfloat16
U32 = jnp.uint32
I32 = jnp.int32

D = 2048
BATCH = 4
SEQ = 4096
DEPTH = 2
GRID_W = 64
CTX_LEN = 256
HALF = D // 2
CHUNK = 128
A_GROUPS = 4
A_GW = HALF // A_GROUPS
POOL_WINDOWS = (2, 4, 8, 16)
B_GW = HALF // len(POOL_WINDOWS)
CONV_W = 31
HEAD_DIM = 64
N_Q_HEADS = HALF // HEAD_DIM
N_KV_HEADS = 2
Q_PER_KV = N_Q_HEADS // N_KV_HEADS
KV_W = N_KV_HEADS * HEAD_DIM
ATT_BLK = 128
WINDOW = 128
AXIS_DIM = HEAD_DIM // 2
ROPE_BASE = 10000.0
N_EXPERTS = 64
N_EXPERT_GROUPS = 8
PER_GROUP = N_EXPERTS // N_EXPERT_GROUPS
TOPK_GROUPS = 4
TOP_K = 8
EXPERT_FF = 512
ROUTED_SCALE = 2.5
EPS = 1e-6

N_LAT = BATCH * SEQ
N_CTX = BATCH * CTX_LEN
N_ALL = N_LAT + N_CTX
CTX_GROUP = BATCH
HALO = 128
CONV_HALO = 16
MOE_BM = 256

TM_PROJ = 512
TM_SEQ = 256
TM_ROUTE = 512
TM_COMBINE = 256

MIB = 1024 * 1024


def _cparams(sem, vmem_mib):
    return pltpu.CompilerParams(dimension_semantics=sem, vmem_limit_bytes=vmem_mib * MIB)


def _dot(a, b):
    return jnp.dot(a, b, preferred_element_type=F32)


def _dot_nt(a, b):
    return lax.dot_general(a, b, (((1,), (1,)), ((), ())), preferred_element_type=F32)


def _rms_mod(x, g, sh, sc):
    y = x * lax.rsqrt(jnp.mean(x * x, axis=-1, keepdims=True) + EPS) * g
    return y * (1.0 + sc) + sh


def _layer_norm(x, g, b):
    mu = jnp.mean(x, axis=-1, keepdims=True)
    xc = x - mu
    var = jnp.mean(xc * xc, axis=-1, keepdims=True)
    return xc * lax.rsqrt(var + EPS) * g + b


def _gelu(x):
    return 0.5 * x * (1.0 + lax.erf(x * (2.0 ** -0.5)))


def _silu(x):
    return x * jax.nn.sigmoid(x)


def _split_bf16(x):
    hi = x.astype(BF16)
    lo = (x - hi.astype(F32)).astype(BF16)
    return hi, lo


def _pack_halves(y):
    n = y.shape[1] // 2
    lo = lax.bitcast_convert_type(y[:, :n].astype(BF16).astype(F32), U32) >> 16
    hi = lax.bitcast_convert_type(y[:, n:].astype(BF16).astype(F32), U32) & U32(0xFFFF0000)
    return hi | lo


def _unpack_halves(p):
    lo = lax.bitcast_convert_type(p << 16, F32)
    hi = lax.bitcast_convert_type(p & U32(0xFFFF0000), F32)
    return lo, hi


def _unpack_bf16(p):
    lo, hi = _unpack_halves(p)
    return jnp.concatenate([lo.astype(BF16), hi.astype(BF16)], axis=1)


LANE_BLKS = HALF // 128


def _store_tile_rows(ref, packed):
    tm = packed.shape[0]
    for s in range(LANE_BLKS):
        ref[pl.ds(s, tm, stride=LANE_BLKS), :] = packed[:, s * 128:(s + 1) * 128]


def _load_tile_rows(ref, tm):
    return jnp.concatenate([ref[pl.ds(s, tm, stride=LANE_BLKS), :] for s in range(LANE_BLKS)], axis=1)


def _mod_spec(chunk, group_fn):
    return pl.BlockSpec((None, None, 1, D), lambda i, *_: (group_fn(i), chunk, 0, 0))


def _group_fn(tm):
    return lambda i: jnp.minimum(i // (SEQ // tm), CTX_GROUP)


def _const_spec(*shape):
    return pl.BlockSpec(shape, lambda *_: (0,) * len(shape))


def _ada_kernel(c_ref, w_ref, b_ref, o_ref):
    s = _silu(c_ref[...]).astype(BF16)
    o_ref[...] = _dot(s, w_ref[...].astype(BF16)) + b_ref[...]


def _ada_mod(c, c_ctx, ada_w, ada_b):
    tn = 1024
    cv = jnp.zeros((8, D), F32).at[:BATCH].set(c).at[CTX_GROUP].set(c_ctx)
    out = pl.pallas_call(
        _ada_kernel,
        grid=(DEPTH, 6 * D // tn),
        in_specs=[pl.BlockSpec((8, D), lambda l, j: (0, 0)),
                  pl.BlockSpec((None, D, tn), lambda l, j: (l, 0, j)),
                  pl.BlockSpec((None, 1, tn), lambda l, j: (l, 0, j))],
        out_specs=pl.BlockSpec((None, 8, tn), lambda l, j: (l, 0, j)),
        out_shape=jax.ShapeDtypeStruct((DEPTH, 8, 6 * D), F32),
        compiler_params=_cparams(("arbitrary", "arbitrary"), 40),
        name="ada_mod",
    )(cv, ada_w, ada_b.reshape(DEPTH, 1, 6 * D))
    return out.reshape(DEPTH, 8, 6, 1, D)


def _two_stream_specs(tm):
    lat_tiles = N_LAT // tm
    return [pl.BlockSpec((tm, D), lambda i: (jnp.minimum(i, lat_tiles - 1), 0)),
            pl.BlockSpec((tm, D), lambda i: (jnp.maximum(i - lat_tiles, 0), 0))]


def _two_stream_rows(xl_ref, xc_ref):
    tm = xl_ref.shape[0]
    return jnp.where(pl.program_id(0) < N_LAT // tm, xl_ref[...], xc_ref[...])


def _even_in_kernel(xl_ref, xc_ref, g_ref, sh_ref, sc_ref, w_ref, uv_ref, z_ref, h_ref):
    h_ref[...] = _rms_mod(_two_stream_rows(xl_ref, xc_ref), g_ref[...], sh_ref[...], sc_ref[...]).astype(BF16)
    for j in range(2):
        cs = slice(j * HALF, (j + 1) * HALF)
        uv_ref[:, cs] = _gelu(_dot(h_ref[...], w_ref[:, cs])).astype(BF16)
    z_ref[...] = _dot(h_ref[...], w_ref[:, 2 * HALF:])


def _resident_spec(*shape):
    return pl.BlockSpec(shape, lambda *_: (0,) * len(shape), pipeline_mode=pl.Buffered(1))


def _even_in(xl, xc, mod_l, norm_g, w_in, tm):
    n_rows = N_ALL
    group_fn = _group_fn(tm)
    return pl.pallas_call(
        _even_in_kernel,
        grid=(n_rows // tm,),
        in_specs=_two_stream_specs(tm) + [
                  _const_spec(1, D),
                  _mod_spec(0, group_fn), _mod_spec(1, group_fn),
                  _resident_spec(D, 3 * HALF)],
        out_specs=[pl.BlockSpec((tm, 2 * HALF), lambda i: (i, 0)),
                   pl.BlockSpec((tm, HALF), lambda i: (i, 0))],
        out_shape=[jax.ShapeDtypeStruct((n_rows, 2 * HALF), BF16),
                   jax.ShapeDtypeStruct((n_rows, HALF), F32)],
        scratch_shapes=[pltpu.VMEM((tm, D), BF16)],
        compiler_params=_cparams(("arbitrary",), 48),
        name="even_in",
    )(xl, xc, norm_g, mod_l, mod_l, w_in)


def _band(d, w):
    inside = lax.bitcast_convert_type(d + w // 2, U32) < U32(w)
    return jnp.where(inside, 1.0, 0.0).astype(BF16)


def _seq_tile(i, tm):
    lat_tiles = N_LAT // tm
    is_lat = i < lat_tiles
    it = i % (SEQ // tm)
    first = jnp.logical_or(jnp.logical_not(is_lat), it == 0)
    last = jnp.logical_or(jnp.logical_not(is_lat), it == SEQ // tm - 1)
    pos0 = jnp.where(is_lat, it * tm, 0)
    seq_len = jnp.where(is_lat, SEQ, CTX_LEN)
    return first, last, pos0, seq_len


def _even_mix_kernel(u_ref, v_ref, z_ref, zp_ref, zn_ref, lng_ref, lnb_ref, ws_ref, bs_ref,
                     wp_ref, ps_ref, y_ref, *, tm):
    first, last, pos0, seq_len = _seq_tile(pl.program_id(0), tm)
    vn = _layer_norm(v_ref[...].astype(F32), lng_ref[...], lnb_ref[...]).astype(BF16)
    for g in range(A_GROUPS):
        cs = slice(g * A_GW, (g + 1) * A_GW)
        for c in range(tm // CHUNK):
            rs = slice(c * CHUNK, (c + 1) * CHUNK)
            mixed = _dot(ws_ref[g], vn[rs, cs]) + bs_ref[g]
            y_ref[rs, cs] = (u_ref[rs, cs].astype(F32) * mixed).astype(BF16)
    z = z_ref[...]
    zp = jnp.where(first, 0.0, zp_ref[...])
    zn = jnp.where(last, 0.0, zn_ref[...])
    z_hi, z_lo = _split_bf16(z)
    zp_hi, zp_lo = _split_bf16(zp)
    zn_hi, zn_lo = _split_bf16(zn)
    d_main = (lax.broadcasted_iota(I32, (tm, tm), 1) - lax.broadcasted_iota(I32, (tm, tm), 0))
    d_halo = (lax.broadcasted_iota(I32, (tm, HALO), 1) - lax.broadcasted_iota(I32, (tm, HALO), 0))
    pos = pos0 + lax.broadcasted_iota(I32, (tm, 1), 0)
    for g, w in enumerate(POOL_WINDOWS):
        cs = slice(g * B_GW, (g + 1) * B_GW)
        bm_ = _band(d_main, w)
        bp = _band(d_halo - HALO, w)
        bn = _band(d_halo + tm, w)
        tot = (_dot(bm_, z_hi[:, cs]) + _dot(bm_, z_lo[:, cs])
               + _dot(bp, zp_hi[:, cs]) + _dot(bp, zp_lo[:, cs])
               + _dot(bn, zn_hi[:, cs]) + _dot(bn, zn_lo[:, cs]))
        cnt = (jnp.minimum(pos + w // 2, seq_len) - jnp.maximum(pos - w // 2, 0)).astype(F32)
        pooled = (tot / cnt - z[:, cs]).astype(BF16)
        y_ref[:, HALF + g * B_GW:HALF + (g + 1) * B_GW] = (
            _dot(pooled, wp_ref[g]) * ps_ref[:, cs]).astype(BF16)


def _even_mix(uv, z, ln_g, ln_b, ws, bs, wp, ps, tm):
    n_rows = z.shape[0]
    hb = tm // HALO
    n_hblk = n_rows // HALO
    return pl.pallas_call(
        functools.partial(_even_mix_kernel, tm=tm),
        grid=(n_rows // tm,),
        in_specs=[pl.BlockSpec((tm, HALF), lambda i: (i, 0)),
                  pl.BlockSpec((tm, HALF), lambda i: (i, 1)),
                  pl.BlockSpec((tm, HALF), lambda i: (i, 0)),
                  pl.BlockSpec((HALO, HALF), lambda i: (jnp.maximum(i * hb - 1, 0), 0)),
                  pl.BlockSpec((HALO, HALF), lambda i: (jnp.minimum((i + 1) * hb, n_hblk - 1), 0)),
                  _const_spec(1, HALF), _const_spec(1, HALF),
                  _const_spec(A_GROUPS, CHUNK, CHUNK), _const_spec(A_GROUPS, CHUNK, 1),
                  _const_spec(len(POOL_WINDOWS), B_GW, B_GW), _const_spec(1, HALF)],
        out_specs=pl.BlockSpec((tm, D), lambda i: (i, 0)),
        out_shape=jax.ShapeDtypeStruct((n_rows, D), BF16),
        compiler_params=_cparams(("arbitrary",), 40),
        name="even_mix",
    )(uv, uv, z, z, z, ln_g, ln_b, ws, bs, wp, ps)


def _mix_out_kernel(ya_ref, yb_ref, w_ref, xl_ref, xc_ref, g1_ref, n2_ref, sh_ref, sc_ref, rw_ref,
                    xo_ref, hp_ref, lg_ref):
    o = _dot(ya_ref[...], w_ref[0]) + _dot(yb_ref[...], w_ref[1])
    xn = _two_stream_rows(xl_ref, xc_ref) + g1_ref[...] * o
    xo_ref[...] = xn
    h = _rms_mod(xn, n2_ref[...], sh_ref[...], sc_ref[...])
    _store_tile_rows(hp_ref, _pack_halves(h))
    lg_ref[...] = _dot_nt(rw_ref[...], h.astype(BF16))


def _mix_out(ya, ya_col, yb, yb_col, w_out, xl, xc, n_rows, mod_l, norm2_g, rw, tm):
    group_fn = _group_fn(tm)
    return pl.pallas_call(
        _mix_out_kernel,
        grid=(n_rows // tm,),
        in_specs=[pl.BlockSpec((tm, HALF), lambda i: (i, ya_col)),
                  pl.BlockSpec((tm, HALF), lambda i: (i, yb_col)),
                  _resident_spec(2, HALF, D)] + _two_stream_specs(tm) + [
                  _mod_spec(2, group_fn), _const_spec(1, D), _mod_spec(3, group_fn), _mod_spec(4, group_fn),
                  _const_spec(N_EXPERTS, D)],
        out_specs=[pl.BlockSpec((tm, D), lambda i: (i, 0)),
                   pl.BlockSpec((tm * LANE_BLKS, 128), lambda i: (i, 0)),
                   pl.BlockSpec((N_EXPERTS, tm), lambda i: (0, i))],
        out_shape=[jax.ShapeDtypeStruct((n_rows, D), F32),
                   jax.ShapeDtypeStruct((n_rows * LANE_BLKS, 128), U32),
                   jax.ShapeDtypeStruct((N_EXPERTS, n_rows), F32)],
        compiler_params=_cparams(("arbitrary",), 56),
        name="mix_out",
    )(ya, yb, w_out, xl, xc, mod_l, norm2_g, mod_l, mod_l, rw)


def _rope(x, cos, sin_signed, first_half):
    partner = jnp.where(first_half, pltpu.roll(x, 128 - AXIS_DIM // 2, 1), pltpu.roll(x, AXIS_DIM // 2, 1))
    return x * cos + partner * sin_signed


def _odd_in_kernel(x_ref, g_ref, sh_ref, sc_ref, w_ref, wkv_ref, cos_ref, sin_ref,
                   zc_ref, q_ref, k_ref, v_ref, h_ref):
    tm = x_ref.shape[0]
    h_ref[...] = _rms_mod(x_ref[...], g_ref[...], sh_ref[...], sc_ref[...]).astype(BF16)
    first_half = (lax.broadcasted_iota(I32, (tm, 128), 1) % AXIS_DIM) < (AXIS_DIM // 2)
    cos = cos_ref[...]
    sin = sin_ref[...]
    for b in range(HALF // 256):
        a = _dot(h_ref[...], w_ref[:, b * 256:(b + 1) * 256])
        gate = _dot(h_ref[...], w_ref[:, HALF + b * 256:HALF + (b + 1) * 256])
        zc_ref[:, b * 256:(b + 1) * 256] = a * jax.nn.sigmoid(gate)
    for b in range(HALF // 256):
        q = _dot(h_ref[...], w_ref[:, 2 * HALF + b * 256:2 * HALF + (b + 1) * 256])
        for s in range(2):
            cs = slice(b * 256 + s * 128, b * 256 + (s + 1) * 128)
            q_ref[:, cs] = (_rope(q[:, s * 128:(s + 1) * 128], cos, sin, first_half)
                            * (HEAD_DIM ** -0.5)).astype(BF16)
    kv = _dot(h_ref[...], wkv_ref[...])
    k_ref[...] = _rope(kv[:, :KV_W], cos, sin, first_half).astype(BF16)
    v_ref[...] = kv[:, KV_W:].astype(BF16)


def _odd_in(x, mod_l, norm_g, w_main, w_kv, cos_t, sin_t, tm):
    n_rows = x.shape[0]
    group_fn = _group_fn(tm)
    lat_tiles = N_LAT // tm
    pos_blk = lambda i: (jnp.where(i < lat_tiles, i % (SEQ // tm), SEQ // tm), 0)
    row = lambda w: pl.BlockSpec((tm, w), lambda i: (i, 0))
    return pl.pallas_call(
        _odd_in_kernel,
        grid=(n_rows // tm,),
        in_specs=[pl.BlockSpec((tm, D), lambda i: (i, 0)),
                  _const_spec(1, D),
                  _mod_spec(0, group_fn), _mod_spec(1, group_fn),
                  _resident_spec(D, 3 * HALF),
                  _resident_spec(D, 2 * KV_W),
                  pl.BlockSpec((tm, 128), pos_blk), pl.BlockSpec((tm, 128), pos_blk)],
        out_specs=[row(HALF), row(HALF), row(KV_W), row(KV_W)],
        out_shape=[jax.ShapeDtypeStruct((n_rows, HALF), F32),
                   jax.ShapeDtypeStruct((n_rows, HALF), BF16),
                   jax.ShapeDtypeStruct((n_rows, KV_W), BF16),
                   jax.ShapeDtypeStruct((n_rows, KV_W), BF16)],
        scratch_shapes=[pltpu.VMEM((tm, D), BF16)],
        compiler_params=_cparams(("arbitrary",), 48),
        name="odd_in",
    )(x, norm_g, mod_l, mod_l, w_main, w_kv, cos_t, sin_t)


def _conv_kernel(z_ref, zp_ref, zn_ref, w_ref, b_ref, lng_ref, lnb_ref, y_ref, ze_ref, zs_ref, c_ref, *, tm):
    first, last, _, _ = _seq_tile(pl.program_id(0), tm)
    ze_ref[0:CONV_HALO, :] = jnp.where(first, 0.0, zp_ref[...])
    ze_ref[CONV_HALO:CONV_HALO + tm, :] = z_ref[...]
    ze_ref[CONV_HALO + tm:, :] = jnp.where(last, 0.0, zn_ref[...])
    rc = 64
    base = CONV_HALO - CONV_W // 2
    n_sh = zs_ref.shape[1]
    for b in range(8):
        zs_ref[b] = ze_ref[b:b + n_sh, :]

    def lane_block(cb, _):
        cs = pl.ds(pl.multiple_of(cb * 128, 128), 128)
        for r in range(tm // rc):
            acc = jnp.zeros((rc, 128), F32)
            for t in range(CONV_W):
                off = base + t
                start = r * rc + (off // 8) * 8
                acc = acc + w_ref[t:t + 1, cs] * zs_ref[off % 8, start:start + rc, cs]
            c_ref[r * rc:(r + 1) * rc, cs] = acc
        return 0

    lax.fori_loop(0, HALF // 128, lane_block, 0)
    y = _layer_norm(c_ref[...] + b_ref[...], lng_ref[...], lnb_ref[...])
    y_ref[...] = _silu(y).astype(BF16)


def _conv_module(zc, n_rows, conv_w, conv_b, ln_g, ln_b, tm):
    hb = tm // CONV_HALO
    n_hblk = zc.shape[0] // CONV_HALO
    return pl.pallas_call(
        functools.partial(_conv_kernel, tm=tm),
        grid=(n_rows // tm,),
        in_specs=[pl.BlockSpec((tm, HALF), lambda i: (i, 0)),
                  pl.BlockSpec((CONV_HALO, HALF), lambda i: (jnp.maximum(i * hb - 1, 0), 0)),
                  pl.BlockSpec((CONV_HALO, HALF), lambda i: (jnp.minimum((i + 1) * hb, n_hblk - 1), 0)),
                  _const_spec(CONV_W, HALF), _const_spec(1, HALF), _const_spec(1, HALF), _const_spec(1, HALF)],
        out_specs=pl.BlockSpec((tm, HALF), lambda i: (i, 0)),
        out_shape=jax.ShapeDtypeStruct((n_rows, HALF), BF16),
        scratch_shapes=[pltpu.VMEM((tm + 2 * CONV_HALO, HALF), F32),
                        pltpu.VMEM((8, tm + 2 * CONV_HALO - 8, HALF), F32),
                        pltpu.VMEM((tm, HALF), F32)],
        compiler_params=_cparams(("arbitrary",), 40),
        name="conv_module",
    )(zc, zc, zc, conv_w, conv_b, ln_g, ln_b)


def _attn_kernel(sink_ref, q_ref, kp_ref, kc_ref, kn_ref, vp_ref, vc_ref, vn_ref, kx_ref, vx_ref, o_ref):
    i = pl.program_id(1)
    n_keys = 3 * ATT_BLK + CTX_LEN
    kb = jnp.concatenate([kp_ref[...], kc_ref[...], kn_ref[...], kx_ref[...]], axis=0)
    vb = jnp.concatenate([vp_ref[...], vc_ref[...], vn_ref[...], vx_ref[...]], axis=0)
    key_lane = lax.broadcasted_iota(I32, (n_keys, 2 * HEAD_DIM), 1)
    k_head = [jnp.where(key_lane < HEAD_DIM, kb, jnp.zeros_like(kb)),
              jnp.where(key_lane >= HEAD_DIM, kb, jnp.zeros_like(kb))]
    r = lax.broadcasted_iota(I32, (ATT_BLK, n_keys), 0)
    c = lax.broadcasted_iota(I32, (ATT_BLK, n_keys), 1)
    kpos = c + (i - 1) * ATT_BLK
    band_ok = lax.bitcast_convert_type(c - r, U32) <= U32(2 * WINDOW)
    in_seq = lax.bitcast_convert_type(kpos, U32) < U32(SEQ)
    bias = jnp.where(c >= 3 * ATT_BLK, 0.0, jnp.where(band_ok, jnp.where(in_seq, 0.0, -jnp.inf), -jnp.inf))
    gs = 4
    rows = gs * ATT_BLK
    out_lane = lax.broadcasted_iota(I32, (rows, 2 * HEAD_DIM), 1)
    bias_s = jnp.concatenate([bias] * gs, axis=0)
    for g0 in range(0, Q_PER_KV, gs):
        qs = jnp.concatenate([q_ref[:, g * 128:(g + 1) * 128] for g in range(g0, g0 + gs)], axis=0)
        outs = []
        for kvh in range(N_KV_HEADS):
            sk = jnp.concatenate([jnp.full((ATT_BLK, 1), sink_ref[kvh * Q_PER_KV + g], F32)
                                  for g in range(g0, g0 + gs)], axis=0)
            s = _dot_nt(qs, k_head[kvh]) + bias_s
            m = jnp.maximum(jnp.max(s, axis=-1, keepdims=True), sk)
            e = jnp.exp(s - m)
            den = jnp.sum(e, axis=-1, keepdims=True) + jnp.exp(sk - m)
            outs.append(_dot(e.astype(BF16), vb) / den)
        o = jnp.where(out_lane < HEAD_DIM, outs[0], outs[1]).astype(BF16)
        for j in range(gs):
            o_ref[:, (g0 + j) * 128:(g0 + j + 1) * 128] = o[j * ATT_BLK:(j + 1) * ATT_BLK, :]


def _attention(q, k, v, sink):
    nb = SEQ // ATT_BLK
    ctx0 = N_LAT // CTX_LEN
    blk = lambda w, fn: pl.BlockSpec((ATT_BLK, w), fn)
    prev = lambda b, i, s: (b * nb + jnp.maximum(i - 1, 0), 0)
    cur = lambda b, i, s: (b * nb + i, 0)
    nxt = lambda b, i, s: (b * nb + jnp.minimum(i + 1, nb - 1), 0)
    ctx = pl.BlockSpec((CTX_LEN, KV_W), lambda b, i, s: (ctx0 + b, 0))
    grid_spec = pltpu.PrefetchScalarGridSpec(
        num_scalar_prefetch=1,
        grid=(BATCH, nb),
        in_specs=[blk(HALF, cur), blk(KV_W, prev), blk(KV_W, cur), blk(KV_W, nxt),
                  blk(KV_W, prev), blk(KV_W, cur), blk(KV_W, nxt), ctx, ctx],
        out_specs=blk(HALF, cur),
    )
    return pl.pallas_call(
        _attn_kernel,
        grid_spec=grid_spec,
        out_shape=jax.ShapeDtypeStruct((N_LAT, HALF), BF16),
        compiler_params=_cparams(("arbitrary", "arbitrary"), 32),
        name="attention",
    )(sink, q, k, k, k, v, v, v, k, v)


def _first_argmax(x, iota, n):
    m = jnp.max(x, axis=0, keepdims=True)
    first = jnp.min(jnp.where(x == m, iota, n), axis=0, keepdims=True)
    return m, first


def _route_kernel(lg_ref, bias_ref, idx_ref, gate_ref, rank_ref, cnt_ref, carry_ref):
    tm = lg_ref.shape[1]

    @pl.when(pl.program_id(0) == 0)
    def _():
        carry_ref[...] = jnp.zeros_like(carry_ref)

    scores = jax.nn.sigmoid(lg_ref[...])
    biased = scores + bias_ref[...]
    sub = lax.broadcasted_iota(I32, (PER_GROUP, tm), 0)
    blocks = [biased[g * PER_GROUP:(g + 1) * PER_GROUP, :] for g in range(N_EXPERT_GROUPS)]
    gs = []
    for blk in blocks:
        m1, f1 = _first_argmax(blk, sub, PER_GROUP)
        m2 = jnp.max(jnp.where(sub == f1, -jnp.inf, blk), axis=0, keepdims=True)
        gs.append(m1 + m2)
    gs = jnp.concatenate(gs, axis=0)
    giota = lax.broadcasted_iota(I32, (N_EXPERT_GROUPS, tm), 0)
    keep = jnp.zeros((N_EXPERT_GROUPS, tm), F32)
    for _ in range(TOPK_GROUPS):
        _, f = _first_argmax(gs, giota, N_EXPERT_GROUPS)
        hit = giota == f
        keep = jnp.where(hit, 1.0, keep)
        gs = jnp.where(hit, -jnp.inf, gs)
    cur = jnp.concatenate([jnp.where(keep[g:g + 1, :] > 0.0, blocks[g], -jnp.inf)
                           for g in range(N_EXPERT_GROUPS)], axis=0)
    eiota = lax.broadcasted_iota(I32, (N_EXPERTS, tm), 0)
    chosen = jnp.zeros((N_EXPERTS, tm), F32)
    idx, sel = [], []
    for _ in range(TOP_K):
        _, f = _first_argmax(cur, eiota, N_EXPERTS)
        hit = eiota == f
        idx.append(f)
        sel.append(jnp.sum(jnp.where(hit, scores, 0.0), axis=0, keepdims=True))
        cur = jnp.where(hit, -jnp.inf, cur)
        chosen = jnp.where(hit, 1.0, chosen)
    sel = jnp.concatenate(sel, axis=0)
    idx = jnp.concatenate(idx, axis=0)
    gate_ref[...] = sel / jnp.sum(sel, axis=0, keepdims=True) * ROUTED_SCALE
    idx_ref[...] = idx
    before = jnp.where(lax.broadcasted_iota(I32, (tm, tm), 0) < lax.broadcasted_iota(I32, (tm, tm), 1), 1.0, 0.0)
    rank = _dot(chosen.astype(BF16), before.astype(BF16)) + carry_ref[:, 0:1]
    rank_ref[...] = jnp.concatenate(
        [jnp.sum(jnp.where(eiota == idx[k:k + 1, :], rank, 0.0), axis=0, keepdims=True) for k in range(TOP_K)],
        axis=0).astype(I32)
    carry_ref[...] = carry_ref[...] + jnp.sum(chosen, axis=1, keepdims=True)
    cnt_ref[...] = carry_ref[...]


def _route(logits_t, router_bias, tm):
    n_tok = logits_t.shape[1]
    tok = lambda rows: pl.BlockSpec((rows, tm), lambda i: (0, i))
    return pl.pallas_call(
        _route_kernel,
        grid=(n_tok // tm,),
        in_specs=[tok(N_EXPERTS), _const_spec(N_EXPERTS, 1)],
        out_specs=[tok(TOP_K), tok(TOP_K), tok(TOP_K), _const_spec(N_EXPERTS, 128)],
        out_shape=[jax.ShapeDtypeStruct((TOP_K, n_tok), I32),
                   jax.ShapeDtypeStruct((TOP_K, n_tok), F32),
                   jax.ShapeDtypeStruct((TOP_K, n_tok), I32),
                   jax.ShapeDtypeStruct((N_EXPERTS, 128), F32)],
        scratch_shapes=[pltpu.VMEM((N_EXPERTS, 128), F32)],
        compiler_params=_cparams(("arbitrary",), 32),
        name="route",
    )(logits_t, router_bias.astype(F32).reshape(N_EXPERTS, 1))


def _moe_kernel(be_ref, na_ref, rp_ref, nx_ref, src_ref, srcn_ref, dst_ref, h_hbm, wg_hbm, wu_hbm, wd_hbm, y_hbm,
                xb, yb, wgf, wuf, wdf, wgb, wub, wdb, sem_in, sem_out, sem_w, *, layer):
    i = pl.program_id(0)
    n_steps = pl.num_programs(0)
    n_act = na_ref[0]
    cur = i % 2
    oth = 1 - cur
    bm = xb.shape[1] // LANE_BLKS

    def tile(ref, row):
        start = row * LANE_BLKS
        if not isinstance(row, int):
            start = pl.multiple_of(start, LANE_BLKS)
        return ref.at[pl.ds(start, LANE_BLKS), :]

    def gather(idx_ref, s):
        for r in range(bm):
            pltpu.make_async_copy(tile(h_hbm, idx_ref[0, 0, r]), tile(xb.at[s], r), sem_in.at[s]).start()

    def scatter(idx_ref, s):
        for r in range(bm):
            pltpu.make_async_copy(tile(yb.at[s], r), tile(y_hbm, idx_ref[0, 0, r]), sem_out.at[s]).start()

    def wait_gather(s):
        pltpu.make_async_copy(h_hbm.at[pl.ds(0, bm * LANE_BLKS), :], xb.at[s], sem_in.at[s]).wait()

    def wait_scatter(s):
        pltpu.make_async_copy(yb.at[s], y_hbm.at[pl.ds(0, bm * LANE_BLKS), :], sem_out.at[s]).wait()

    def weight_copies(e, s):
        return [pltpu.make_async_copy(w_hbm.at[layer, e], stage.at[s], sem_w.at[s])
                for w_hbm, stage in ((wg_hbm, wgf), (wu_hbm, wuf), (wd_hbm, wdf))]

    @pl.when(i == 0)
    def _():
        gather(src_ref, 0)
        for cp in weight_copies(be_ref[0], 0):
            cp.start(priority=1)

    @pl.when(i < n_act)
    def _():
        wait_gather(cur)

        @pl.when(i >= 2)
        def _():
            wait_scatter(cur)

        @pl.when(jnp.logical_or(i == 0, be_ref[i] != be_ref[jnp.maximum(i - 1, 0)]))
        def _():
            s = rp_ref[i]
            for cp in weight_copies(be_ref[i], s):
                cp.wait()

            @pl.when(nx_ref[i] >= 0)
            def _():
                for cp in weight_copies(nx_ref[i], 1 - s):
                    cp.start(priority=1)

            wgb[...] = wgf[s].astype(BF16)
            wub[...] = wuf[s].astype(BF16)
            wdb[...] = wdf[s].astype(BF16)

        gather(srcn_ref, oth)
        x = _unpack_bf16(_load_tile_rows(xb.at[cur], bm))
        hid = (_silu(_dot(x, wgb[...])) * _dot(x, wub[...])).astype(BF16)
        _store_tile_rows(yb.at[cur], _pack_halves(_dot(hid, wdb[...])))
        scatter(dst_ref, cur)

    @pl.when(i == n_steps - 1)
    def _():
        wait_gather(n_act % 2)
        wait_scatter((n_act + 1) % 2)

        @pl.when(n_act >= 2)
        def _():
            wait_scatter(n_act % 2)


def _moe_experts(layer, h2p, plan, w_gate, w_up, w_down):
    blk_exp, n_act, run_par, nxt_exp, src, dst = plan
    n_blocks = blk_exp.shape[0]
    bm = MOE_BM
    smem_blk = lambda fn: pl.BlockSpec((1, 1, bm), fn, memory_space=pltpu.SMEM)
    hbm = pl.BlockSpec(memory_space=pl.ANY)
    grid_spec = pltpu.PrefetchScalarGridSpec(
        num_scalar_prefetch=4,
        grid=(n_blocks,),
        in_specs=[smem_blk(lambda i, *_: (i, 0, 0)),
                  smem_blk(lambda i, *_: (jnp.minimum(i + 1, n_blocks - 1), 0, 0)),
                  smem_blk(lambda i, *_: (i, 0, 0)),
                  hbm, hbm, hbm, hbm],
        out_specs=hbm,
        scratch_shapes=[pltpu.VMEM((2, bm * LANE_BLKS, 128), U32), pltpu.VMEM((2, bm * LANE_BLKS, 128), U32),
                        pltpu.VMEM((2, D, EXPERT_FF), F32), pltpu.VMEM((2, D, EXPERT_FF), F32),
                        pltpu.VMEM((2, EXPERT_FF, D), F32),
                        pltpu.VMEM((D, EXPERT_FF), BF16), pltpu.VMEM((D, EXPERT_FF), BF16),
                        pltpu.VMEM((EXPERT_FF, D), BF16),
                        pltpu.SemaphoreType.DMA((2,)), pltpu.SemaphoreType.DMA((2,)),
                        pltpu.SemaphoreType.DMA((2,))],
    )
    src3 = src.reshape(n_blocks, 1, bm)
    return pl.pallas_call(
        functools.partial(_moe_kernel, layer=layer),
        grid_spec=grid_spec,
        out_shape=jax.ShapeDtypeStruct((n_blocks * bm * LANE_BLKS, 128), U32),
        compiler_params=_cparams(("arbitrary",), 48),
        name="moe_experts",
    )(blk_exp, n_act, run_par, nxt_exp, src3, src3, dst.reshape(n_blocks, 1, bm), h2p, w_gate, w_up, w_down)


def _combine_kernel(*refs, final):
    x_ref, h_ref, wg_ref, wu_ref, wd_ref, gt_ref, g2_ref = refs[:7]
    y_refs = refs[7:7 + TOP_K]
    fg_ref = refs[7 + TOP_K] if final else None
    o_ref, sh_ref = refs[-2:]
    tm = x_ref.shape[0]
    xh = _unpack_bf16(_load_tile_rows(h_ref, tm))
    hid = (_silu(_dot(xh, wg_ref[...])) * _dot(xh, wu_ref[...])).astype(BF16)
    sh_ref[...] = _dot(hid, wd_ref[...])
    gt = gt_ref[...]
    for b in range(LANE_BLKS):
        c_lo = slice(b * 128, (b + 1) * 128)
        c_hi = slice(HALF + b * 128, HALF + (b + 1) * 128)
        acc_lo = sh_ref[:, c_lo]
        acc_hi = sh_ref[:, c_hi]
        for k in range(TOP_K):
            lo, hi = _unpack_halves(y_refs[k][pl.ds(b, tm, stride=LANE_BLKS), :])
            acc_lo = acc_lo + lo * gt[:, k:k + 1]
            acc_hi = acc_hi + hi * gt[:, k:k + 1]
        o_ref[:, c_lo] = x_ref[:, c_lo] + g2_ref[:, c_lo] * acc_lo
        o_ref[:, c_hi] = x_ref[:, c_hi] + g2_ref[:, c_hi] * acc_hi
    if final:
        xn = o_ref[...]
        o_ref[...] = xn * lax.rsqrt(jnp.mean(xn * xn, axis=-1, keepdims=True) + EPS) * fg_ref[...]


def _combine(x, h2p, shared_w, gates, mod_l, y_tk, n_tok, final_g, tm):
    nt = n_tok // tm
    group_fn = _group_fn(tm)
    row = pl.BlockSpec((tm, D), lambda i: (i, 0))
    packed = lambda fn: pl.BlockSpec((tm * LANE_BLKS, 128), fn)
    in_specs = [row, packed(lambda i: (i, 0)),
                _resident_spec(D, EXPERT_FF), _resident_spec(D, EXPERT_FF), _resident_spec(EXPERT_FF, D),
                pl.BlockSpec((tm, TOP_K), lambda i: (i, 0)), _mod_spec(5, group_fn)]
    in_specs += [packed(functools.partial(lambda i, k: (k * nt + i, 0), k=k)) for k in range(TOP_K)]
    args = [x, h2p, *shared_w, gates, mod_l] + [y_tk] * TOP_K
    final = final_g is not None
    if final:
        in_specs.append(_const_spec(1, D))
        args.append(final_g)
    return pl.pallas_call(
        functools.partial(_combine_kernel, final=final),
        grid=(nt,),
        in_specs=in_specs,
        out_specs=row,
        out_shape=jax.ShapeDtypeStruct((n_tok, D), F32),
        scratch_shapes=[pltpu.VMEM((tm, D), F32)],
        compiler_params=_cparams(("arbitrary",), 48),
        name="moe_combine",
    )(*args)


def _dispatch_plan(idx_t, rank_t, cnt, n_tok):
    bm = MOE_BM
    n_assign = n_tok * TOP_K
    n_blocks = n_assign // bm + N_EXPERTS
    n_rows = n_blocks * bm
    sizes = cnt[:, 0].astype(I32)
    padded = (sizes + bm - 1) // bm * bm
    pad_end = jnp.cumsum(padded)
    pad_start = pad_end - padded
    blk_first = jnp.arange(n_blocks, dtype=I32) * bm
    blk_exp = jnp.minimum(jnp.sum((pad_end[None, :] <= blk_first[:, None]).astype(I32), axis=1), N_EXPERTS - 1)
    n_act = (pad_end[-1] // bm).astype(I32).reshape(1)
    experts = jnp.arange(N_EXPERTS, dtype=I32)
    slot = jnp.sum(jnp.where(idx_t[:, :, None] == experts, pad_start, 0), axis=-1) + rank_t
    ids = (jnp.arange(n_tok, dtype=I32)[None, :] * TOP_K + jnp.arange(TOP_K, dtype=I32)[:, None] + 1)
    inv = (jnp.zeros((n_rows,), I32).at[slot.reshape(-1)].add(ids.reshape(-1)) - 1).reshape(n_blocks, bm)
    valid = inv >= 0
    tok = inv // TOP_K
    spare = n_assign + blk_exp[:, None] * bm + jnp.arange(bm, dtype=I32)[None, :]
    src = jnp.where(valid, tok, 0).astype(I32)
    dst = jnp.where(valid, (inv - tok * TOP_K) * n_tok + tok, spare).astype(I32)
    prev = jnp.concatenate([jnp.full((1,), -1, I32), blk_exp[:-1]])
    run_par = (jnp.cumsum((blk_exp != prev).astype(I32)) - 1) % 2
    nxt_blk = pad_end[blk_exp] // bm
    nxt_exp = jnp.where(nxt_blk < n_act[0], blk_exp[jnp.minimum(nxt_blk, n_blocks - 1)], -1).astype(I32)
    return blk_exp, n_act, run_par.astype(I32), nxt_exp, src, dst


def _moe_layer(layer, x, h2p, logits_t, router_bias, mod_l, exp_w, shared_w, final_g):
    n_tok = h2p.shape[0] // LANE_BLKS
    idx_t, gates_t, rank_t, cnt = _route(logits_t, router_bias, TM_ROUTE)
    plan = _dispatch_plan(idx_t, rank_t, cnt, n_tok)
    y_tk = _moe_experts(layer, h2p, plan, *exp_w)
    return _combine(x, h2p, [w[layer].astype(BF16) for w in shared_w], gates_t.T, mod_l, y_tk, n_tok,
                    final_g, TM_COMBINE)


def _rope_tables():
    rows = SEQ // GRID_W
    row = jnp.broadcast_to(jnp.arange(rows)[:, None], (rows, GRID_W)).reshape(-1).astype(F32)
    col = jnp.broadcast_to(jnp.arange(GRID_W)[None, :], (rows, GRID_W)).reshape(-1).astype(F32)
    inv = ROPE_BASE ** (-jnp.arange(0, AXIS_DIM, 2, dtype=F32) / AXIS_DIM)
    ang_r = row[:, None] * inv
    ang_c = col[:, None] * inv
    ang = jnp.concatenate([ang_r, ang_r, ang_c, ang_c], axis=-1)
    cos, sin = jnp.cos(ang), jnp.sin(ang)
    sign = jnp.where((jnp.arange(HEAD_DIM) % AXIS_DIM) < AXIS_DIM // 2, -1.0, 1.0).astype(F32)
    ident = TM_PROJ
    cos_t = jnp.concatenate([jnp.tile(cos, (1, 2)), jnp.ones((ident, 128), F32)], axis=0)
    sin_t = jnp.concatenate([jnp.tile(sin * sign, (1, 2)), jnp.zeros((ident, 128), F32)], axis=0)
    return cos_t, sin_t


def _heads_g_major(w, axis):
    shape = w.shape
    w = w.reshape(shape[:axis] + (N_KV_HEADS, Q_PER_KV, HEAD_DIM) + shape[axis + 1:])
    return jnp.swapaxes(w, axis, axis + 1).reshape(shape)


def kernel(x, c, ctx, c_ctx, ada_w, ada_b, norm1_g, norm2_g, even_w_in, gmlp_ln_g, gmlp_ln_b, gmlp_ws, gmlp_bs, pool_w, pool_scale, even_w_out, odd_w_in, conv_w, conv_b, conv_ln_g, conv_ln_b, attn_sink, odd_w_out, router_w, router_bias, exp_w_gate, exp_w_up, exp_w_down, shared_w_gate, shared_w_up, shared_w_down, final_g):
    mod = _ada_mod(c, c_ctx, ada_w, ada_b)
    row = lambda a: a.reshape(1, -1)
    exp_w = (exp_w_gate, exp_w_up, exp_w_down)
    shared_w = (shared_w_gate, shared_w_up, shared_w_down)

    def router_t(i):
        return router_w[i].T.astype(BF16)

    xl = x.reshape(N_LAT, D)
    xc = ctx.reshape(N_CTX, D)

    uv, z = _even_in(xl, xc, mod[0], row(norm1_g[0]), even_w_in[0].astype(BF16), TM_PROJ)
    y = _even_mix(uv, z, row(gmlp_ln_g[0]), row(gmlp_ln_b[0]), gmlp_ws[0].astype(BF16),
                  gmlp_bs[0].reshape(A_GROUPS, CHUNK, 1), pool_w[0].astype(BF16), row(pool_scale[0]), TM_SEQ)
    x1, h2p, lg = _mix_out(y, 0, y, 1, even_w_out[0].astype(BF16).reshape(2, HALF, D), xl, xc, N_ALL, mod[0],
                           row(norm2_g[0]), router_t(0), TM_PROJ)
    x1 = _moe_layer(0, x1, h2p, lg, router_bias[0], mod[0], exp_w, shared_w, None)

    w_in1 = odd_w_in[0]
    w_main = jnp.concatenate([w_in1[:, :2 * HALF], _heads_g_major(w_in1[:, 2 * HALF:3 * HALF], 1)],
                             axis=1).astype(BF16)
    w_kv = w_in1[:, 3 * HALF:].astype(BF16)
    w_out1 = odd_w_out[0]
    w_out1 = jnp.stack([w_out1[:HALF], _heads_g_major(w_out1[HALF:], 0)], axis=0).astype(BF16)
    cos_t, sin_t = _rope_tables()
    zc, q, k, v = _odd_in(x1, mod[1], row(norm1_g[1]), w_main, w_kv, cos_t, sin_t, TM_PROJ)
    y_conv = _conv_module(zc, N_LAT, conv_w[0], row(conv_b[0]), row(conv_ln_g[0]), row(conv_ln_b[0]), TM_SEQ)
    y_attn = _attention(q, k, v, attn_sink[0].astype(F32))
    x2, h2p, lg = _mix_out(y_conv, 0, y_attn, 0, w_out1, x1, x1, N_LAT, mod[1], row(norm2_g[1]),
                           router_t(1), TM_PROJ)
    out = _moe_layer(1, x2, h2p, lg, router_bias[1], mod[1], exp_w, shared_w, row(final_g))
    return out.reshape(BATCH, SEQ, D)
```

```python
import functools

import jax
import jax.numpy as jnp
from jax import lax
from jax.experimental import pallas as pl
from jax.experimental.pallas import tpu as pltpu

F32 = jnp.float32
BF16 = jnp.bfloat16
U32 = jnp.uint32
I32 = jnp.int32

D = 2048
BATCH = 4
SEQ = 4096
DEPTH = 2
GRID_W = 64
CTX_LEN = 256
HALF = D // 2
CHUNK = 128
A_GROUPS = 4
A_GW = HALF // A_GROUPS
POOL_WINDOWS = (2, 4, 8, 16)
B_GW = HALF // len(POOL_WINDOWS)
CONV_W = 31
HEAD_DIM = 64
N_Q_HEADS = HALF // HEAD_DIM
N_KV_HEADS = 2
Q_PER_KV = N_Q_HEADS // N_KV_HEADS
KV_W = N_KV_HEADS * HEAD_DIM
ATT_BLK = 128
WINDOW = 128
AXIS_DIM = HEAD_DIM // 2
ROPE_BASE = 10000.0
N_EXPERTS = 64
N_EXPERT_GROUPS = 8
PER_GROUP = N_EXPERTS // N_EXPERT_GROUPS
TOPK_GROUPS = 4
TOP_K = 8
EXPERT_FF = 512
ROUTED_SCALE = 2.5
EPS = 1e-6

N_LAT = BATCH * SEQ
N_CTX = BATCH * CTX_LEN
N_ALL = N_LAT + N_CTX
CTX_GROUP = BATCH
HALO = 128
CONV_HALO = 16
MOE_BM = 256

TM_PROJ = 512
TM_SEQ = 256
TM_ROUTE = 512
TM_COMBINE = 256

MIB = 1024 * 1024


def _cparams(sem, vmem_mib):
    return pltpu.CompilerParams(dimension_semantics=sem, vmem_limit_bytes=vmem_mib * MIB)


def _dot(a, b):
    return jnp.dot(a, b, preferred_element_type=F32)


def _dot_nt(a, b):
    return lax.dot_general(a, b, (((1,), (1,)), ((), ())), preferred_element_type=F32)


def _rms_mod(x, g, sh, sc):
    y = x * lax.rsqrt(jnp.mean(x * x, axis=-1, keepdims=True) + EPS) * g
    return y * (1.0 + sc) + sh


def _layer_norm(x, g, b):
    mu = jnp.mean(x, axis=-1, keepdims=True)
    xc = x - mu
    var = jnp.mean(xc * xc, axis=-1, keepdims=True)
    return xc * lax.rsqrt(var + EPS) * g + b


def _gelu(x):
    return 0.5 * x * (1.0 + lax.erf(x * (2.0 ** -0.5)))


def _silu(x):
    return x * jax.nn.sigmoid(x)


def _split_bf16(x):
    hi = x.astype(BF16)
    lo = (x - hi.astype(F32)).astype(BF16)
    return hi, lo


def _pack_halves(y):
    n = y.shape[1] // 2
    lo = lax.bitcast_convert_type(y[:, :n].astype(BF16).astype(F32), U32) >> 16
    hi = lax.bitcast_convert_type(y[:, n:].astype(BF16).astype(F32), U32) & U32(0xFFFF0000)
    return hi | lo


def _unpack_halves(p):
    lo = lax.bitcast_convert_type(p << 16, F32)
    hi = lax.bitcast_convert_type(p & U32(0xFFFF0000), F32)
    return lo, hi


def _unpack_bf16(p):
    lo, hi = _unpack_halves(p)
    return jnp.concatenate([lo.astype(BF16), hi.astype(BF16)], axis=1)


LANE_BLKS = HALF // 128


def _store_tile_rows(ref, packed):
    tm = packed.shape[0]
    for s in range(LANE_BLKS):
        ref[pl.ds(s, tm, stride=LANE_BLKS), :] = packed[:, s * 128:(s + 1) * 128]


def _load_tile_rows(ref, tm):
    return jnp.concatenate([ref[pl.ds(s, tm, stride=LANE_BLKS), :] for s in range(LANE_BLKS)], axis=1)


def _mod_spec(chunk, group_fn):
    return pl.BlockSpec((None, None, 1, D), lambda i, *_: (group_fn(i), chunk, 0, 0))


def _group_fn(tm):
    return lambda i: jnp.minimum(i // (SEQ // tm), CTX_GROUP)


def _const_spec(*shape):
    return pl.BlockSpec(shape, lambda *_: (0,) * len(shape))


def _ada_kernel(c_ref, w_ref, b_ref, o_ref):
    s = _silu(c_ref[...]).astype(BF16)
    o_ref[...] = _dot(s, w_ref[...].astype(BF16)) + b_ref[...]


def _ada_mod(c, c_ctx, ada_w, ada_b):
    tn = 1024
    cv = jnp.zeros((8, D), F32).at[:BATCH].set(c).at[CTX_GROUP].set(c_ctx)
    out = pl.pallas_call(
        _ada_kernel,
        grid=(DEPTH, 6 * D // tn),
        in_specs=[pl.BlockSpec((8, D), lambda l, j: (0, 0)),
                  pl.BlockSpec((None, D, tn), lambda l, j: (l, 0, j)),
                  pl.BlockSpec((None, 1, tn), lambda l, j: (l, 0, j))],
        out_specs=pl.BlockSpec((None, 8, tn), lambda l, j: (l, 0, j)),
        out_shape=jax.ShapeDtypeStruct((DEPTH, 8, 6 * D), F32),
        compiler_params=_cparams(("arbitrary", "arbitrary"), 40),
        name="ada_mod",
    )(cv, ada_w, ada_b.reshape(DEPTH, 1, 6 * D))
    return out.reshape(DEPTH, 8, 6, 1, D)


def _two_stream_specs(tm):
    lat_tiles = N_LAT // tm
    return [pl.BlockSpec((tm, D), lambda i: (jnp.minimum(i, lat_tiles - 1), 0)),
            pl.BlockSpec((tm, D), lambda i: (jnp.maximum(i - lat_tiles, 0), 0))]


def _two_stream_rows(xl_ref, xc_ref):
    tm = xl_ref.shape[0]
    return jnp.where(pl.program_id(0) < N_LAT // tm, xl_ref[...], xc_ref[...])


def _even_in_kernel(xl_ref, xc_ref, g_ref, sh_ref, sc_ref, w_ref, uv_ref, z_ref, h_ref):
    h_ref[...] = _rms_mod(_two_stream_rows(xl_ref, xc_ref), g_ref[...], sh_ref[...], sc_ref[...]).astype(BF16)
    for j in range(2):
        cs = slice(j * HALF, (j + 1) * HALF)
        uv_ref[:, cs] = _gelu(_dot(h_ref[...], w_ref[:, cs])).astype(BF16)
    z_ref[...] = _dot(h_ref[...], w_ref[:, 2 * HALF:])


def _resident_spec(*shape):
    return pl.BlockSpec(shape, lambda *_: (0,) * len(shape), pipeline_mode=pl.Buffered(1))


def _even_in(xl, xc, mod_l, norm_g, w_in, tm):
    n_rows = N_ALL
    group_fn = _group_fn(tm)
    return pl.pallas_call(
        _even_in_kernel,
        grid=(n_rows // tm,),
        in_specs=_two_stream_specs(tm) + [
                  _const_spec(1, D),
                  _mod_spec(0, group_fn), _mod_spec(1, group_fn),
                  _resident_spec(D, 3 * HALF)],
        out_specs=[pl.BlockSpec((tm, 2 * HALF), lambda i: (i, 0)),
                   pl.BlockSpec((tm, HALF), lambda i: (i, 0))],
        out_shape=[jax.ShapeDtypeStruct((n_rows, 2 * HALF), BF16),
                   jax.ShapeDtypeStruct((n_rows, HALF), F32)],
        scratch_shapes=[pltpu.VMEM((tm, D), BF16)],
        compiler_params=_cparams(("arbitrary",), 48),
        name="even_in",
    )(xl, xc, norm_g, mod_l, mod_l, w_in)


def _band(d, w):
    inside = lax.bitcast_convert_type(d + w // 2, U32) < U32(w)
    return jnp.where(inside, 1.0, 0.0).astype(BF16)


def _seq_tile(i, tm):
    lat_tiles = N_LAT // tm
    is_lat = i < lat_tiles
    it = i % (SEQ // tm)
    first = jnp.logical_or(jnp.logical_not(is_lat), it == 0)
    last = jnp.logical_or(jnp.logical_not(is_lat), it == SEQ // tm - 1)
    pos0 = jnp.where(is_lat, it * tm, 0)
    seq_len = jnp.where(is_lat, SEQ, CTX_LEN)
    return first, last, pos0, seq_len


def _even_mix_kernel(u_ref, v_ref, z_ref, zp_ref, zn_ref, lng_ref, lnb_ref, ws_ref, bs_ref,
                     wp_ref, ps_ref, y_ref, *, tm):
    first, last, pos0, seq_len = _seq_tile(pl.program_id(0), tm)
    vn = _layer_norm(v_ref[...].astype(F32), lng_ref[...], lnb_ref[...]).astype(BF16)
    for g in range(A_GROUPS):
        cs = slice(g * A_GW, (g + 1) * A_GW)
        for c in range(tm // CHUNK):
            rs = slice(c * CHUNK, (c + 1) * CHUNK)
            mixed = _dot(ws_ref[g], vn[rs, cs]) + bs_ref[g]
            y_ref[rs, cs] = (u_ref[rs, cs].astype(F32) * mixed).astype(BF16)
    z = z_ref[...]
    zp = jnp.where(first, 0.0, zp_ref[...])
    zn = jnp.where(last, 0.0, zn_ref[...])
    z_hi, z_lo = _split_bf16(z)
    zp_hi, zp_lo = _split_bf16(zp)
    zn_hi, zn_lo = _split_bf16(zn)
    d_main = (lax.broadcasted_iota(I32, (tm, tm), 1) - lax.broadcasted_iota(I32, (tm, tm), 0))
    d_halo = (lax.broadcasted_iota(I32, (tm, HALO), 1) - lax.broadcasted_iota(I32, (tm, HALO), 0))
    pos = pos0 + lax.broadcasted_iota(I32, (tm, 1), 0)
    for g, w in enumerate(POOL_WINDOWS):
        cs = slice(g * B_GW, (g + 1) * B_GW)
        bm_ = _band(d_main, w)
        bp = _band(d_halo - HALO, w)
        bn = _band(d_halo + tm, w)
        tot = (_dot(bm_, z_hi[:, cs]) + _dot(bm_, z_lo[:, cs])
               + _dot(bp, zp_hi[:, cs]) + _dot(bp, zp_lo[:, cs])
               + _dot(bn, zn_hi[:, cs]) + _dot(bn, zn_lo[:, cs]))
        cnt = (jnp.minimum(pos + w // 2, seq_len) - jnp.maximum(pos - w // 2, 0)).astype(F32)
        pooled = (tot / cnt - z[:, cs]).astype(BF16)
        y_ref[:, HALF + g * B_GW:HALF + (g + 1) * B_GW] = (
            _dot(pooled, wp_ref[g]) * ps_ref[:, cs]).astype(BF16)


def _even_mix(uv, z, ln_g, ln_b, ws, bs, wp, ps, tm):
    n_rows = z.shape[0]
    hb = tm // HALO
    n_hblk = n_rows // HALO
    return pl.pallas_call(
        functools.partial(_even_mix_kernel, tm=tm),
        grid=(n_rows // tm,),
        in_specs=[pl.BlockSpec((tm, HALF), lambda i: (i, 0)),
                  pl.BlockSpec((tm, HALF), lambda i: (i, 1)),
                  pl.BlockSpec((tm, HALF), lambda i: (i, 0)),
                  pl.BlockSpec((HALO, HALF), lambda i: (jnp.maximum(i * hb - 1, 0), 0)),
                  pl.BlockSpec((HALO, HALF), lambda i: (jnp.minimum((i + 1) * hb, n_hblk - 1), 0)),
                  _const_spec(1, HALF), _const_spec(1, HALF),
                  _const_spec(A_GROUPS, CHUNK, CHUNK), _const_spec(A_GROUPS, CHUNK, 1),
                  _const_spec(len(POOL_WINDOWS), B_GW, B_GW), _const_spec(1, HALF)],
        out_specs=pl.BlockSpec((tm, D), lambda i: (i, 0)),
        out_shape=jax.ShapeDtypeStruct((n_rows, D), BF16),
        compiler_params=_cparams(("arbitrary",), 40),
        name="even_mix",
    )(uv, uv, z, z, z, ln_g, ln_b, ws, bs, wp, ps)


def _mix_out_kernel(ya_ref, yb_ref, w_ref, xl_ref, xc_ref, g1_ref, n2_ref, sh_ref, sc_ref, rw_ref,
                    xo_ref, hp_ref, lg_ref):
    o = _dot(ya_ref[...], w_ref[0]) + _dot(yb_ref[...], w_ref[1])
    xn = _two_stream_rows(xl_ref, xc_ref) + g1_ref[...] * o
    xo_ref[...] = xn
    h = _rms_mod(xn, n2_ref[...], sh_ref[...], sc_ref[...])
    _store_tile_rows(hp_ref, _pack_halves(h))
    lg_ref[...] = _dot_nt(rw_ref[...], h.astype(BF16))


def _mix_out(ya, ya_col, yb, yb_col, w_out, xl, xc, n_rows, mod_l, norm2_g, rw, tm):
    group_fn = _group_fn(tm)
    return pl.pallas_call(
        _mix_out_kernel,
        grid=(n_rows // tm,),
        in_specs=[pl.BlockSpec((tm, HALF), lambda i: (i, ya_col)),
                  pl.BlockSpec((tm, HALF), lambda i: (i, yb_col)),
                  _resident_spec(2, HALF, D)] + _two_stream_specs(tm) + [
                  _mod_spec(2, group_fn), _const_spec(1, D), _mod_spec(3, group_fn), _mod_spec(4, group_fn),
                  _const_spec(N_EXPERTS, D)],
        out_specs=[pl.BlockSpec((tm, D), lambda i: (i, 0)),
                   pl.BlockSpec((tm * LANE_BLKS, 128), lambda i: (i, 0)),
                   pl.BlockSpec((N_EXPERTS, tm), lambda i: (0, i))],
        out_shape=[jax.ShapeDtypeStruct((n_rows, D), F32),
                   jax.ShapeDtypeStruct((n_rows * LANE_BLKS, 128), U32),
                   jax.ShapeDtypeStruct((N_EXPERTS, n_rows), F32)],
        compiler_params=_cparams(("arbitrary",), 56),
        name="mix_out",
    )(ya, yb, w_out, xl, xc, mod_l, norm2_g, mod_l, mod_l, rw)


def _rope(x, cos, sin_signed, first_half):
    partner = jnp.where(first_half, pltpu.roll(x, 128 - AXIS_DIM // 2, 1), pltpu.roll(x, AXIS_DIM // 2, 1))
    return x * cos + partner * sin_signed


def _odd_in_kernel(x_ref, g_ref, sh_ref, sc_ref, w_ref, wkv_ref, cos_ref, sin_ref,
                   zc_ref, q_ref, k_ref, v_ref, h_ref):
    tm = x_ref.shape[0]
    h_ref[...] = _rms_mod(x_ref[...], g_ref[...], sh_ref[...], sc_ref[...]).astype(BF16)
    first_half = (lax.broadcasted_iota(I32, (tm, 128), 1) % AXIS_DIM) < (AXIS_DIM // 2)
    cos = cos_ref[...]
    sin = sin_ref[...]
    for b in range(HALF // 256):
        a = _dot(h_ref[...], w_ref[:, b * 256:(b + 1) * 256])
        gate = _dot(h_ref[...], w_ref[:, HALF + b * 256:HALF + (b + 1) * 256])
        zc_ref[:, b * 256:(b + 1) * 256] = a * jax.nn.sigmoid(gate)
    for b in range(HALF // 256):
        q = _dot(h_ref[...], w_ref[:, 2 * HALF + b * 256:2 * HALF + (b + 1) * 256])
        for s in range(2):
            cs = slice(b * 256 + s * 128, b * 256 + (s + 1) * 128)
            q_ref[:, cs] = (_rope(q[:, s * 128:(s + 1) * 128], cos, sin, first_half)
                            * (HEAD_DIM ** -0.5)).astype(BF16)
    kv = _dot(h_ref[...], wkv_ref[...])
    k_ref[...] = _rope(kv[:, :KV_W], cos, sin, first_half).astype(BF16)
    v_ref[...] = kv[:, KV_W:].astype(BF16)


def _odd_in(x, mod_l, norm_g, w_main, w_kv, cos_t, sin_t, tm):
    n_rows = x.shape[0]
    group_fn = _group_fn(tm)
    lat_tiles = N_LAT // tm
    pos_blk = lambda i: (jnp.where(i < lat_tiles, i % (SEQ // tm), SEQ // tm), 0)
    row = lambda w: pl.BlockSpec((tm, w), lambda i: (i, 0))
    return pl.pallas_call(
        _odd_in_kernel,
        grid=(n_rows // tm,),
        in_specs=[pl.BlockSpec((tm, D), lambda i: (i, 0)),
                  _const_spec(1, D),
                  _mod_spec(0, group_fn), _mod_spec(1, group_fn),
                  _resident_spec(D, 3 * HALF),
                  _resident_spec(D, 2 * KV_W),
                  pl.BlockSpec((tm, 128), pos_blk), pl.BlockSpec((tm, 128), pos_blk)],
        out_specs=[row(HALF), row(HALF), row(KV_W), row(KV_W)],
        out_shape=[jax.ShapeDtypeStruct((n_rows, HALF), F32),
                   jax.ShapeDtypeStruct((n_rows, HALF), BF16),
                   jax.ShapeDtypeStruct((n_rows, KV_W), BF16),
                   jax.ShapeDtypeStruct((n_rows, KV_W), BF16)],
        scratch_shapes=[pltpu.VMEM((tm, D), BF16)],
        compiler_params=_cparams(("arbitrary",), 48),
        name="odd_in",
    )(x, norm_g, mod_l, mod_l, w_main, w_kv, cos_t, sin_t)


def _conv_kernel(z_ref, zp_ref, zn_ref, w_ref, b_ref, lng_ref, lnb_ref, y_ref, ze_ref, zs_ref, c_ref, *, tm):
    first, last, _, _ = _seq_tile(pl.program_id(0), tm)
    ze_ref[0:CONV_HALO, :] = jnp.where(first, 0.0, zp_ref[...])
    ze_ref[CONV_HALO:CONV_HALO + tm, :] = z_ref[...]
    ze_ref[CONV_HALO + tm:, :] = jnp.where(last, 0.0, zn_ref[...])
    rc = 64
    base = CONV_HALO - CONV_W // 2
    n_sh = zs_ref.shape[1]
    for b in range(8):
        zs_ref[b] = ze_ref[b:b + n_sh, :]

    def lane_block(cb, _):
        cs = pl.ds(pl.multiple_of(cb * 128, 128), 128)
        for r in range(tm // rc):
            acc = jnp.zeros((rc, 128), F32)
            for t in range(CONV_W):
                off = base + t
                start = r * rc + (off // 8) * 8
                acc = acc + w_ref[t:t + 1, cs] * zs_ref[off % 8, start:start + rc, cs]
            c_ref[r * rc:(r + 1) * rc, cs] = acc
        return 0

    lax.fori_loop(0, HALF // 128, lane_block, 0)
    y = _layer_norm(c_ref[...] + b_ref[...], lng_ref[...], lnb_ref[...])
    y_ref[...] = _silu(y).astype(BF16)


def _conv_module(zc, n_rows, conv_w, conv_b, ln_g, ln_b, tm):
    hb = tm // CONV_HALO
    n_hblk = zc.shape[0] // CONV_HALO
    return pl.pallas_call(
        functools.partial(_conv_kernel, tm=tm),
        grid=(n_rows // tm,),
        in_specs=[pl.BlockSpec((tm, HALF), lambda i: (i, 0)),
                  pl.BlockSpec((CONV_HALO, HALF), lambda i: (jnp.maximum(i * hb - 1, 0), 0)),
                  pl.BlockSpec((CONV_HALO, HALF), lambda i: (jnp.minimum((i + 1) * hb, n_hblk - 1), 0)),
                  _const_spec(CONV_W, HALF), _const_spec(1, HALF), _const_spec(1, HALF), _const_spec(1, HALF)],
        out_specs=pl.BlockSpec((tm, HALF), lambda i: (i, 0)),
        out_shape=jax.ShapeDtypeStruct((n_rows, HALF), BF16),
        scratch_shapes=[pltpu.VMEM((tm + 2 * CONV_HALO, HALF), F32),
                        pltpu.VMEM((8, tm + 2 * CONV_HALO - 8, HALF), F32),
                        pltpu.VMEM((tm, HALF), F32)],
        compiler_params=_cparams(("arbitrary",), 40),
        name="conv_module",
    )(zc, zc, zc, conv_w, conv_b, ln_g, ln_b)


def _attn_kernel(sink_ref, q_ref, kp_ref, kc_ref, kn_ref, vp_ref, vc_ref, vn_ref, kx_ref, vx_ref, o_ref):
    i = pl.program_id(1)
    n_keys = 3 * ATT_BLK + CTX_LEN
    kb = jnp.concatenate([kp_ref[...], kc_ref[...], kn_ref[...], kx_ref[...]], axis=0)
    vb = jnp.concatenate([vp_ref[...], vc_ref[...], vn_ref[...], vx_ref[...]], axis=0)
    key_lane = lax.broadcasted_iota(I32, (n_keys, 2 * HEAD_DIM), 1)
    k_head = [jnp.where(key_lane < HEAD_DIM, kb, jnp.zeros_like(kb)),
              jnp.where(key_lane >= HEAD_DIM, kb, jnp.zeros_like(kb))]
    r = lax.broadcasted_iota(I32, (ATT_BLK, n_keys), 0)
    c = lax.broadcasted_iota(I32, (ATT_BLK, n_keys), 1)
    kpos = c + (i - 1) * ATT_BLK
    band_ok = lax.bitcast_convert_type(c - r, U32) <= U32(2 * WINDOW)
    in_seq = lax.bitcast_convert_type(kpos, U32) < U32(SEQ)
    bias = jnp.where(c >= 3 * ATT_BLK, 0.0, jnp.where(band_ok, jnp.where(in_seq, 0.0, -jnp.inf), -jnp.inf))
    gs = 4
    rows = gs * ATT_BLK
    out_lane = lax.broadcasted_iota(I32, (rows, 2 * HEAD_DIM), 1)
    bias_s = jnp.concatenate([bias] * gs, axis=0)
    for g0 in range(0, Q_PER_KV, gs):
        qs = jnp.concatenate([q_ref[:, g * 128:(g + 1) * 128] for g in range(g0, g0 + gs)], axis=0)
        outs = []
        for kvh in range(N_KV_HEADS):
            sk = jnp.concatenate([jnp.full((ATT_BLK, 1), sink_ref[kvh * Q_PER_KV + g], F32)
                                  for g in range(g0, g0 + gs)], axis=0)
            s = _dot_nt(qs, k_head[kvh]) + bias_s
            m = jnp.maximum(jnp.max(s, axis=-1, keepdims=True), sk)
            e = jnp.exp(s - m)
            den = jnp.sum(e, axis=-1, keepdims=True) + jnp.exp(sk - m)
            outs.append(_dot(e.astype(BF16), vb) / den)
        o = jnp.where(out_lane < HEAD_DIM, outs[0], outs[1]).astype(BF16)
        for j in range(gs):
            o_ref[:, (g0 + j) * 128:(g0 + j + 1) * 128] = o[j * ATT_BLK:(j + 1) * ATT_BLK, :]


def _attention(q, k, v, sink):
    nb = SEQ // ATT_BLK
    ctx0 = N_LAT // CTX_LEN
    blk = lambda w, fn: pl.BlockSpec((ATT_BLK, w), fn)
    prev = lambda b, i, s: (b * nb + jnp.maximum(i - 1, 0), 0)
    cur = lambda b, i, s: (b * nb + i, 0)
    nxt = lambda b, i, s: (b * nb + jnp.minimum(i + 1, nb - 1), 0)
    ctx = pl.BlockSpec((CTX_LEN, KV_W), lambda b, i, s: (ctx0 + b, 0))
    grid_spec = pltpu.PrefetchScalarGridSpec(
        num_scalar_prefetch=1,
        grid=(BATCH, nb),
        in_specs=[blk(HALF, cur), blk(KV_W, prev), blk(KV_W, cur), blk(KV_W, nxt),
                  blk(KV_W, prev), blk(KV_W, cur), blk(KV_W, nxt), ctx, ctx],
        out_specs=blk(HALF, cur),
    )
    return pl.pallas_call(
        _attn_kernel,
        grid_spec=grid_spec,
        out_shape=jax.ShapeDtypeStruct((N_LAT, HALF), BF16),
        compiler_params=_cparams(("arbitrary", "arbitrary"), 32),
        name="attention",
    )(sink, q, k, k, k, v, v, v, k, v)


def _first_argmax(x, iota, n):
    m = jnp.max(x, axis=0, keepdims=True)
    first = jnp.min(jnp.where(x == m, iota, n), axis=0, keepdims=True)
    return m, first


def _route_kernel(lg_ref, bias_ref, idx_ref, gate_ref, rank_ref, cnt_ref, carry_ref):
    tm = lg_ref.shape[1]

    @pl.when(pl.program_id(0) == 0)
    def _():
        carry_ref[...] = jnp.zeros_like(carry_ref)

    scores = jax.nn.sigmoid(lg_ref[...])
    biased = scores + bias_ref[...]
    sub = lax.broadcasted_iota(I32, (PER_GROUP, tm), 0)
    blocks = [biased[g * PER_GROUP:(g + 1) * PER_GROUP, :] for g in range(N_EXPERT_GROUPS)]
    gs = []
    for blk in blocks:
        m1, f1 = _first_argmax(blk, sub, PER_GROUP)
        m2 = jnp.max(jnp.where(sub == f1, -jnp.inf, blk), axis=0, keepdims=True)
        gs.append(m1 + m2)
    gs = jnp.concatenate(gs, axis=0)
    giota = lax.broadcasted_iota(I32, (N_EXPERT_GROUPS, tm), 0)
    keep = jnp.zeros((N_EXPERT_GROUPS, tm), F32)
    for _ in range(TOPK_GROUPS):
        _, f = _first_argmax(gs, giota, N_EXPERT_GROUPS)
        hit = giota == f
        keep = jnp.where(hit, 1.0, keep)
        gs = jnp.where(hit, -jnp.inf, gs)
    cur = jnp.concatenate([jnp.where(keep[g:g + 1, :] > 0.0, blocks[g], -jnp.inf)
                           for g in range(N_EXPERT_GROUPS)], axis=0)
    eiota = lax.broadcasted_iota(I32, (N_EXPERTS, tm), 0)
    chosen = jnp.zeros((N_EXPERTS, tm), F32)
    idx, sel = [], []
    for _ in range(TOP_K):
        _, f = _first_argmax(cur, eiota, N_EXPERTS)
        hit = eiota == f
        idx.append(f)
        sel.append(jnp.sum(jnp.where(hit, scores, 0.0), axis=0, keepdims=True))
        cur = jnp.where(hit, -jnp.inf, cur)
        chosen = jnp.where(hit, 1.0, chosen)
    sel = jnp.concatenate(sel, axis=0)
    idx = jnp.concatenate(idx, axis=0)
    gate_ref[...] = sel / jnp.sum(sel, axis=0, keepdims=True) * ROUTED_SCALE
    idx_ref[...] = idx
    before = jnp.where(lax.broadcasted_iota(I32, (tm, tm), 0) < lax.broadcasted_iota(I32, (tm, tm), 1), 1.0, 0.0)
    rank = _dot(chosen.astype(BF16), before.astype(BF16)) + carry_ref[:, 0:1]
    rank_ref[...] = jnp.concatenate(
        [jnp.sum(jnp.where(eiota == idx[k:k + 1, :], rank, 0.0), axis=0, keepdims=True) for k in range(TOP_K)],
        axis=0).astype(I32)
    carry_ref[...] = carry_ref[...] + jnp.sum(chosen, axis=1, keepdims=True)
    cnt_ref[...] = carry_ref[...]


def _route(logits_t, router_bias, tm):
    n_tok = logits_t.shape[1]
    tok = lambda rows: pl.BlockSpec((rows, tm), lambda i: (0, i))
    return pl.pallas_call(
        _route_kernel,
        grid=(n_tok // tm,),
        in_specs=[tok(N_EXPERTS), _const_spec(N_EXPERTS, 1)],
        out_specs=[tok(TOP_K), tok(TOP_K), tok(TOP_K), _const_spec(N_EXPERTS, 128)],
        out_shape=[jax.ShapeDtypeStruct((TOP_K, n_tok), I32),
                   jax.ShapeDtypeStruct((TOP_K, n_tok), F32),
                   jax.ShapeDtypeStruct((TOP_K, n_tok), I32),
                   jax.ShapeDtypeStruct((N_EXPERTS, 128), F32)],
        scratch_shapes=[pltpu.VMEM((N_EXPERTS, 128), F32)],
        compiler_params=_cparams(("arbitrary",), 32),
        name="route",
    )(logits_t, router_bias.astype(F32).reshape(N_EXPERTS, 1))


def _moe_kernel(be_ref, na_ref, rp_ref, nx_ref, src_ref, srcn_ref, dst_ref, h_hbm, wg_hbm, wu_hbm, wd_hbm, y_hbm,
                xb, yb, wgf, wuf, wdf, wgb, wub, wdb, sem_in, sem_out, sem_w, *, layer):
    i = pl.program_id(0)
    n_steps = pl.num_programs(0)
    n_act = na_ref[0]
    cur = i % 2
    oth = 1 - cur
    bm = xb.shape[1] // LANE_BLKS

    def tile(ref, row):
        start = row * LANE_BLKS
        if not isinstance(row, int):
            start = pl.multiple_of(start, LANE_BLKS)
        return ref.at[pl.ds(start, LANE_BLKS), :]

    def gather(idx_ref, s):
        for r in range(bm):
            pltpu.make_async_copy(tile(h_hbm, idx_ref[0, 0, r]), tile(xb.at[s], r), sem_in.at[s]).start()

    def scatter(idx_ref, s):
        for r in range(bm):
            pltpu.make_async_copy(tile(yb.at[s], r), tile(y_hbm, idx_ref[0, 0, r]), sem_out.at[s]).start()

    def wait_gather(s):
        pltpu.make_async_copy(h_hbm.at[pl.ds(0, bm * LANE_BLKS), :], xb.at[s], sem_in.at[s]).wait()

    def wait_scatter(s):
        pltpu.make_async_copy(yb.at[s], y_hbm.at[pl.ds(0, bm * LANE_BLKS), :], sem_out.at[s]).wait()

    def weight_copies(e, s):
        return [pltpu.make_async_copy(w_hbm.at[layer, e], stage.at[s], sem_w.at[s])
                for w_hbm, stage in ((wg_hbm, wgf), (wu_hbm, wuf), (wd_hbm, wdf))]

    @pl.when(i == 0)
    def _():
        gather(src_ref, 0)
        for cp in weight_copies(be_ref[0], 0):
            cp.start(priority=1)

    @pl.when(i < n_act)
    def _():
        wait_gather(cur)

        @pl.when(i >= 2)
        def _():
            wait_scatter(cur)

        @pl.when(jnp.logical_or(i == 0, be_ref[i] != be_ref[jnp.maximum(i - 1, 0)]))
        def _():
            e = be_ref[i]
            s = rp_ref[e]
            for cp in weight_copies(e, s):
                cp.wait()

            @pl.when(nx_ref[e] >= 0)
            def _():
                for cp in weight_copies(nx_ref[e], 1 - s):
                    cp.start(priority=1)

            wgb[...] = wgf[s].astype(BF16)
            wub[...] = wuf[s].astype(BF16)
            wdb[...] = wdf[s].astype(BF16)

        gather(srcn_ref, oth)
        x = _unpack_bf16(_load_tile_rows(xb.at[cur], bm))
        hid = (_silu(_dot(x, wgb[...])) * _dot(x, wub[...])).astype(BF16)
        _store_tile_rows(yb.at[cur], _pack_halves(_dot(hid, wdb[...])))
        scatter(dst_ref, cur)

    @pl.when(i == n_steps - 1)
    def _():
        wait_gather(n_act % 2)
        wait_scatter((n_act + 1) % 2)

        @pl.when(n_act >= 2)
        def _():
            wait_scatter(n_act % 2)


def _moe_experts(layer, h2p, plan, w_gate, w_up, w_down):
    blk_exp, n_act, run_par, nxt_exp, src, dst = plan
    n_blocks = blk_exp.shape[0]
    bm = MOE_BM
    smem_blk = lambda fn: pl.BlockSpec((1, 1, bm), fn, memory_space=pltpu.SMEM)
    hbm = pl.BlockSpec(memory_space=pl.ANY)
    grid_spec = pltpu.PrefetchScalarGridSpec(
        num_scalar_prefetch=4,
        grid=(n_blocks,),
        in_specs=[smem_blk(lambda i, *_: (i, 0, 0)),
                  smem_blk(lambda i, *_: (jnp.minimum(i + 1, n_blocks - 1), 0, 0)),
                  smem_blk(lambda i, *_: (i, 0, 0)),
                  hbm, hbm, hbm, hbm],
        out_specs=hbm,
        scratch_shapes=[pltpu.VMEM((2, bm * LANE_BLKS, 128), U32), pltpu.VMEM((2, bm * LANE_BLKS, 128), U32),
                        pltpu.VMEM((2, D, EXPERT_FF), F32), pltpu.VMEM((2, D, EXPERT_FF), F32),
                        pltpu.VMEM((2, EXPERT_FF, D), F32),
                        pltpu.VMEM((D, EXPERT_FF), BF16), pltpu.VMEM((D, EXPERT_FF), BF16),
                        pltpu.VMEM((EXPERT_FF, D), BF16),
                        pltpu.SemaphoreType.DMA((2,)), pltpu.SemaphoreType.DMA((2,)),
                        pltpu.SemaphoreType.DMA((2,))],
    )
    src3 = src.reshape(n_blocks, 1, bm)
    return pl.pallas_call(
        functools.partial(_moe_kernel, layer=layer),
        grid_spec=grid_spec,
        out_shape=jax.ShapeDtypeStruct((n_blocks * bm * LANE_BLKS, 128), U32),
        compiler_params=_cparams(("arbitrary",), 48),
        name="moe_experts",
    )(blk_exp, n_act, run_par, nxt_exp, src3, src3, dst.reshape(n_blocks, 1, bm), h2p, w_gate, w_up, w_down)


def _combine_kernel(*refs, final):
    x_ref, h_ref, wg_ref, wu_ref, wd_ref, gt_ref, g2_ref = refs[:7]
    y_refs = refs[7:7 + TOP_K]
    fg_ref = refs[7 + TOP_K] if final else None
    o_ref, sh_ref = refs[-2:]
    tm = x_ref.shape[0]
    xh = _unpack_bf16(_load_tile_rows(h_ref, tm))
    hid = (_silu(_dot(xh, wg_ref[...])) * _dot(xh, wu_ref[...])).astype(BF16)
    sh_ref[...] = _dot(hid, wd_ref[...])
    gt = gt_ref[...]
    for b in range(LANE_BLKS):
        c_lo = slice(b * 128, (b + 1) * 128)
        c_hi = slice(HALF + b * 128, HALF + (b + 1) * 128)
        acc_lo = sh_ref[:, c_lo]
        acc_hi = sh_ref[:, c_hi]
        for k in range(TOP_K):
            lo, hi = _unpack_halves(y_refs[k][pl.ds(b, tm, stride=LANE_BLKS), :])
            acc_lo = acc_lo + lo * gt[:, k:k + 1]
            acc_hi = acc_hi + hi * gt[:, k:k + 1]
        o_ref[:, c_lo] = x_ref[:, c_lo] + g2_ref[:, c_lo] * acc_lo
        o_ref[:, c_hi] = x_ref[:, c_hi] + g2_ref[:, c_hi] * acc_hi
    if final:
        xn = o_ref[...]
        o_ref[...] = xn * lax.rsqrt(jnp.mean(xn * xn, axis=-1, keepdims=True) + EPS) * fg_ref[...]


def _combine(x, h2p, shared_w, gates, mod_l, y_tk, n_tok, final_g, tm):
    nt = n_tok // tm
    group_fn = _group_fn(tm)
    row = pl.BlockSpec((tm, D), lambda i: (i, 0))
    packed = lambda fn: pl.BlockSpec((tm * LANE_BLKS, 128), fn)
    in_specs = [row, packed(lambda i: (i, 0)),
                _resident_spec(D, EXPERT_FF), _resident_spec(D, EXPERT_FF), _resident_spec(EXPERT_FF, D),
                pl.BlockSpec((tm, TOP_K), lambda i: (i, 0)), _mod_spec(5, group_fn)]
    in_specs += [packed(functools.partial(lambda i, k: (k * nt + i, 0), k=k)) for k in range(TOP_K)]
    args = [x, h2p, *shared_w, gates, mod_l] + [y_tk] * TOP_K
    final = final_g is not None
    if final:
        in_specs.append(_const_spec(1, D))
        args.append(final_g)
    return pl.pallas_call(
        functools.partial(_combine_kernel, final=final),
        grid=(nt,),
        in_specs=in_specs,
        out_specs=row,
        out_shape=jax.ShapeDtypeStruct((n_tok, D), F32),
        scratch_shapes=[pltpu.VMEM((tm, D), F32)],
        compiler_params=_cparams(("arbitrary",), 48),
        name="moe_combine",
    )(*args)


def _dispatch_plan(idx_t, rank_t, cnt, n_tok):
    bm = MOE_BM
    n_assign = n_tok * TOP_K
    n_blocks = n_assign // bm + N_EXPERTS
    n_rows = n_blocks * bm
    sizes = cnt[:, 0].astype(I32)
    padded = (sizes + bm - 1) // bm * bm
    pad_end = jnp.cumsum(padded)
    pad_start = pad_end - padded
    blk_first = jnp.arange(n_blocks, dtype=I32) * bm
    blk_exp = jnp.minimum(jnp.sum((pad_end[None, :] <= blk_first[:, None]).astype(I32), axis=1), N_EXPERTS - 1)
    n_act = (pad_end[-1] // bm).astype(I32).reshape(1)
    experts = jnp.arange(N_EXPERTS, dtype=I32)
    slot = jnp.sum(jnp.where(idx_t[:, :, None] == experts, pad_start, 0), axis=-1) + rank_t
    ids = (jnp.arange(n_tok, dtype=I32)[None, :] * TOP_K + jnp.arange(TOP_K, dtype=I32)[:, None] + 1)
    inv = (jnp.zeros((n_rows,), I32).at[slot.reshape(-1)].add(ids.reshape(-1)) - 1).reshape(n_blocks, bm)
    valid = inv >= 0
    tok = inv // TOP_K
    spare = n_assign + blk_exp[:, None] * bm + jnp.arange(bm, dtype=I32)[None, :]
    src = jnp.where(valid, tok, 0).astype(I32)
    dst = jnp.where(valid, (inv - tok * TOP_K) * n_tok + tok, spare).astype(I32)
    used = sizes > 0
    run_par = ((jnp.cumsum(used.astype(I32)) - 1) % 2).astype(I32)
    first_used_from = lax.cummin(jnp.where(used, experts, N_EXPERTS), axis=0, reverse=True)
    nxt = jnp.concatenate([first_used_from[1:], jnp.full((1,), N_EXPERTS, I32)])
    nxt_exp = jnp.where(nxt < N_EXPERTS, nxt, -1).astype(I32)
    return blk_exp, n_act, run_par, nxt_exp, src, dst


def _moe_layer(layer, x, h2p, logits_t, router_bias, mod_l, exp_w, shared_w, final_g):
    n_tok = h2p.shape[0] // LANE_BLKS
    idx_t, gates_t, rank_t, cnt = _route(logits_t, router_bias, TM_ROUTE)
    plan = _dispatch_plan(idx_t, rank_t, cnt, n_tok)
    y_tk = _moe_experts(layer, h2p, plan, *exp_w)
    return _combine(x, h2p, [w[layer].astype(BF16) for w in shared_w], gates_t.T, mod_l, y_tk, n_tok,
                    final_g, TM_COMBINE)


def _rope_tables():
    rows = SEQ // GRID_W
    row = jnp.broadcast_to(jnp.arange(rows)[:, None], (rows, GRID_W)).reshape(-1).astype(F32)
    col = jnp.broadcast_to(jnp.arange(GRID_W)[None, :], (rows, GRID_W)).reshape(-1).astype(F32)
    inv = ROPE_BASE ** (-jnp.arange(0, AXIS_DIM, 2, dtype=F32) / AXIS_DIM)
    ang_r = row[:, None] * inv
    ang_c = col[:, None] * inv
    ang = jnp.concatenate([ang_r, ang_r, ang_c, ang_c], axis=-1)
    cos, sin = jnp.cos(ang), jnp.sin(ang)
    sign = jnp.where((jnp.arange(HEAD_DIM) % AXIS_DIM) < AXIS_DIM // 2, -1.0, 1.0).astype(F32)
    ident = TM_PROJ
    cos_t = jnp.concatenate([jnp.tile(cos, (1, 2)), jnp.ones((ident, 128), F32)], axis=0)
    sin_t = jnp.concatenate([jnp.tile(sin * sign, (1, 2)), jnp.zeros((ident, 128), F32)], axis=0)
    return cos_t, sin_t


def _heads_g_major(w, axis):
    shape = w.shape
    w = w.reshape(shape[:axis] + (N_KV_HEADS, Q_PER_KV, HEAD_DIM) + shape[axis + 1:])
    return jnp.swapaxes(w, axis, axis + 1).reshape(shape)


def kernel(x, c, ctx, c_ctx, ada_w, ada_b, norm1_g, norm2_g, even_w_in, gmlp_ln_g, gmlp_ln_b, gmlp_ws, gmlp_bs, pool_w, pool_scale, even_w_out, odd_w_in, conv_w, conv_b, conv_ln_g, conv_ln_b, attn_sink, odd_w_out, router_w, router_bias, exp_w_gate, exp_w_up, exp_w_down, shared_w_gate, shared_w_up, shared_w_down, final_g):
    mod = _ada_mod(c, c_ctx, ada_w, ada_b)
    row = lambda a: a.reshape(1, -1)
    exp_w = (exp_w_gate, exp_w_up, exp_w_down)
    shared_w = (shared_w_gate, shared_w_up, shared_w_down)

    def router_t(i):
        return router_w[i].T.astype(BF16)

    xl = x.reshape(N_LAT, D)
    xc = ctx.reshape(N_CTX, D)

    uv, z = _even_in(xl, xc, mod[0], row(norm1_g[0]), even_w_in[0].astype(BF16), TM_PROJ)
    y = _even_mix(uv, z, row(gmlp_ln_g[0]), row(gmlp_ln_b[0]), gmlp_ws[0].astype(BF16),
                  gmlp_bs[0].reshape(A_GROUPS, CHUNK, 1), pool_w[0].astype(BF16), row(pool_scale[0]), TM_SEQ)
    x1, h2p, lg = _mix_out(y, 0, y, 1, even_w_out[0].astype(BF16).reshape(2, HALF, D), xl, xc, N_ALL, mod[0],
                           row(norm2_g[0]), router_t(0), TM_PROJ)
    x1 = _moe_layer(0, x1, h2p, lg, router_bias[0], mod[0], exp_w, shared_w, None)

    w_in1 = odd_w_in[0]
    w_main = jnp.concatenate([w_in1[:, :2 * HALF], _heads_g_major(w_in1[:, 2 * HALF:3 * HALF], 1)],
                             axis=1).astype(BF16)
    w_kv = w_in1[:, 3 * HALF:].astype(BF16)
    w_out1 = odd_w_out[0]
    w_out1 = jnp.stack([w_out1[:HALF], _heads_g_major(w_out1[HALF:], 0)], axis=0).astype(BF16)
    cos_t, sin_t = _rope_tables()
    zc, q, k, v = _odd_in(x1, mod[1], row(norm1_g[1]), w_main, w_kv, cos_t, sin_t, TM_PROJ)
    y_conv = _conv_module(zc, N_LAT, conv_w[0], row(conv_b[0]), row(conv_ln_g[0]), row(conv_ln_b[0]), TM_SEQ)
    y_attn = _attention(q, k, v, attn_sink[0].astype(F32))
    x2, h2p, lg = _mix_out(y_conv, 0, y_attn, 0, w_out1, x1, x1, N_LAT, mod[1], row(norm2_g[1]),
                           router_t(1), TM_PROJ)
    out = _moe_layer(1, x2, h2p, lg, router_bias[1], mod[1], exp_w, shared_w, row(final_g))
    return out.reshape(BATCH, SEQ, D)
```

```python
import functools

import jax
import jax.numpy as jnp
from jax import lax
from jax.experimental import pallas as pl
from jax.experimental.pallas import tpu as pltpu

F32 = jnp.float32
BF16 = jnp.bfloat16
U32 = jnp.uint32
I32 = jnp.int32

D = 2048
BATCH = 4
SEQ = 4096
DEPTH = 2
GRID_W = 64
CTX_LEN = 256
HALF = D // 2
CHUNK = 128
A_GROUPS = 4
A_GW = HALF // A_GROUPS
POOL_WINDOWS = (2, 4, 8, 16)
B_GW = HALF // len(POOL_WINDOWS)
CONV_W = 31
HEAD_DIM = 64
N_Q_HEADS = HALF // HEAD_DIM
N_KV_HEADS = 2
Q_PER_KV = N_Q_HEADS // N_KV_HEADS
KV_W = N_KV_HEADS * HEAD_DIM
ATT_BLK = 128
WINDOW = 128
AXIS_DIM = HEAD_DIM // 2
ROPE_BASE = 10000.0
N_EXPERTS = 64
N_EXPERT_GROUPS = 8
PER_GROUP = N_EXPERTS // N_EXPERT_GROUPS
TOPK_GROUPS = 4
TOP_K = 8
EXPERT_FF = 512
ROUTED_SCALE = 2.5
EPS = 1e-6

N_LAT = BATCH * SEQ
N_CTX = BATCH * CTX_LEN
N_ALL = N_LAT + N_CTX
CTX_GROUP = BATCH
HALO = 128
CONV_HALO = 16
MOE_BM = 256

TM_PROJ = 512
TM_SEQ = 256
TM_ROUTE = 512
TM_COMBINE = 256

MIB = 1024 * 1024


def _cparams(sem, vmem_mib):
    return pltpu.CompilerParams(dimension_semantics=sem, vmem_limit_bytes=vmem_mib * MIB)


def _dot(a, b):
    return jnp.dot(a, b, preferred_element_type=F32)


def _dot_nt(a, b):
    return lax.dot_general(a, b, (((1,), (1,)), ((), ())), preferred_element_type=F32)


def _rms_mod(x, g, sh, sc):
    y = x * lax.rsqrt(jnp.mean(x * x, axis=-1, keepdims=True) + EPS) * g
    return y * (1.0 + sc) + sh


def _layer_norm(x, g, b):
    mu = jnp.mean(x, axis=-1, keepdims=True)
    xc = x - mu
    var = jnp.mean(xc * xc, axis=-1, keepdims=True)
    return xc * lax.rsqrt(var + EPS) * g + b


def _gelu(x):
    return 0.5 * x * (1.0 + lax.erf(x * (2.0 ** -0.5)))


def _silu(x):
    return x * jax.nn.sigmoid(x)


def _split_bf16(x):
    hi = x.astype(BF16)
    lo = (x - hi.astype(F32)).astype(BF16)
    return hi, lo


def _pack_halves(y):
    n = y.shape[1] // 2
    lo = lax.bitcast_convert_type(y[:, :n].astype(BF16).astype(F32), U32) >> 16
    hi = lax.bitcast_convert_type(y[:, n:].astype(BF16).astype(F32), U32) & U32(0xFFFF0000)
    return hi | lo


def _unpack_halves(p):
    lo = lax.bitcast_convert_type(p << 16, F32)
    hi = lax.bitcast_convert_type(p & U32(0xFFFF0000), F32)
    return lo, hi


def _unpack_bf16(p):
    lo, hi = _unpack_halves(p)
    return jnp.concatenate([lo.astype(BF16), hi.astype(BF16)], axis=1)


LANE_BLKS = HALF // 128


def _store_tile_rows(ref, packed):
    tm = packed.shape[0]
    for s in range(LANE_BLKS):
        ref[pl.ds(s, tm, stride=LANE_BLKS), :] = packed[:, s * 128:(s + 1) * 128]


def _load_tile_rows(ref, tm):
    return jnp.concatenate([ref[pl.ds(s, tm, stride=LANE_BLKS), :] for s in range(LANE_BLKS)], axis=1)


def _mod_spec(chunk, group_fn):
    return pl.BlockSpec((None, None, 1, D), lambda i, *_: (group_fn(i), chunk, 0, 0))


def _group_fn(tm):
    return lambda i: jnp.minimum(i // (SEQ // tm), CTX_GROUP)


def _const_spec(*shape):
    return pl.BlockSpec(shape, lambda *_: (0,) * len(shape))


def _ada_kernel(c_ref, w_ref, b_ref, o_ref):
    s = _silu(c_ref[...]).astype(BF16)
    o_ref[...] = _dot(s, w_ref[...].astype(BF16)) + b_ref[...]


def _ada_mod(c, c_ctx, ada_w, ada_b):
    tn = 1024
    cv = jnp.zeros((8, D), F32).at[:BATCH].set(c).at[CTX_GROUP].set(c_ctx)
    out = pl.pallas_call(
        _ada_kernel,
        grid=(DEPTH, 6 * D // tn),
        in_specs=[pl.BlockSpec((8, D), lambda l, j: (0, 0)),
                  pl.BlockSpec((None, D, tn), lambda l, j: (l, 0, j)),
                  pl.BlockSpec((None, 1, tn), lambda l, j: (l, 0, j))],
        out_specs=pl.BlockSpec((None, 8, tn), lambda l, j: (l, 0, j)),
        out_shape=jax.ShapeDtypeStruct((DEPTH, 8, 6 * D), F32),
        compiler_params=_cparams(("arbitrary", "arbitrary"), 40),
        name="ada_mod",
    )(cv, ada_w, ada_b.reshape(DEPTH, 1, 6 * D))
    return out.reshape(DEPTH, 8, 6, 1, D)


def _two_stream_specs(tm):
    lat_tiles = N_LAT // tm
    return [pl.BlockSpec((tm, D), lambda i: (jnp.minimum(i, lat_tiles - 1), 0)),
            pl.BlockSpec((tm, D), lambda i: (jnp.maximum(i - lat_tiles, 0), 0))]


def _two_stream_rows(xl_ref, xc_ref):
    tm = xl_ref.shape[0]
    return jnp.where(pl.program_id(0) < N_LAT // tm, xl_ref[...], xc_ref[...])


def _even_in_kernel(xl_ref, xc_ref, g_ref, sh_ref, sc_ref, w_ref, uv_ref, z_ref, h_ref):
    h_ref[...] = _rms_mod(_two_stream_rows(xl_ref, xc_ref), g_ref[...], sh_ref[...], sc_ref[...]).astype(BF16)
    for j in range(2):
        cs = slice(j * HALF, (j + 1) * HALF)
        uv_ref[:, cs] = _gelu(_dot(h_ref[...], w_ref[:, cs])).astype(BF16)
    z_ref[...] = _dot(h_ref[...], w_ref[:, 2 * HALF:])


def _resident_spec(*shape):
    return pl.BlockSpec(shape, lambda *_: (0,) * len(shape), pipeline_mode=pl.Buffered(1))


def _even_in(xl, xc, mod_l, norm_g, w_in, tm):
    n_rows = N_ALL
    group_fn = _group_fn(tm)
    return pl.pallas_call(
        _even_in_kernel,
        grid=(n_rows // tm,),
        in_specs=_two_stream_specs(tm) + [
                  _const_spec(1, D),
                  _mod_spec(0, group_fn), _mod_spec(1, group_fn),
                  _resident_spec(D, 3 * HALF)],
        out_specs=[pl.BlockSpec((tm, 2 * HALF), lambda i: (i, 0)),
                   pl.BlockSpec((tm, HALF), lambda i: (i, 0))],
        out_shape=[jax.ShapeDtypeStruct((n_rows, 2 * HALF), BF16),
                   jax.ShapeDtypeStruct((n_rows, HALF), F32)],
        scratch_shapes=[pltpu.VMEM((tm, D), BF16)],
        compiler_params=_cparams(("arbitrary",), 48),
        name="even_in",
    )(xl, xc, norm_g, mod_l, mod_l, w_in)


def _band(d, w):
    inside = lax.bitcast_convert_type(d + w // 2, U32) < U32(w)
    return jnp.where(inside, 1.0, 0.0).astype(BF16)


def _seq_tile(i, tm):
    lat_tiles = N_LAT // tm
    is_lat = i < lat_tiles
    it = i % (SEQ // tm)
    first = jnp.logical_or(jnp.logical_not(is_lat), it == 0)
    last = jnp.logical_or(jnp.logical_not(is_lat), it == SEQ // tm - 1)
    pos0 = jnp.where(is_lat, it * tm, 0)
    seq_len = jnp.where(is_lat, SEQ, CTX_LEN)
    return first, last, pos0, seq_len


def _even_mix_kernel(u_ref, v_ref, z_ref, zp_ref, zn_ref, lng_ref, lnb_ref, ws_ref, bs_ref,
                     wp_ref, ps_ref, y_ref, *, tm):
    first, last, pos0, seq_len = _seq_tile(pl.program_id(0), tm)
    vn = _layer_norm(v_ref[...].astype(F32), lng_ref[...], lnb_ref[...]).astype(BF16)
    for g in range(A_GROUPS):
        cs = slice(g * A_GW, (g + 1) * A_GW)
        for c in range(tm // CHUNK):
            rs = slice(c * CHUNK, (c + 1) * CHUNK)
            mixed = _dot(ws_ref[g], vn[rs, cs]) + bs_ref[g]
            y_ref[rs, cs] = (u_ref[rs, cs].astype(F32) * mixed).astype(BF16)
    z = z_ref[...]
    zp = jnp.where(first, 0.0, zp_ref[...])
    zn = jnp.where(last, 0.0, zn_ref[...])
    z_hi, z_lo = _split_bf16(z)
    zp_hi, zp_lo = _split_bf16(zp)
    zn_hi, zn_lo = _split_bf16(zn)
    d_main = (lax.broadcasted_iota(I32, (tm, tm), 1) - lax.broadcasted_iota(I32, (tm, tm), 0))
    d_halo = (lax.broadcasted_iota(I32, (tm, HALO), 1) - lax.broadcasted_iota(I32, (tm, HALO), 0))
    pos = pos0 + lax.broadcasted_iota(I32, (tm, 1), 0)
    for g, w in enumerate(POOL_WINDOWS):
        cs = slice(g * B_GW, (g + 1) * B_GW)
        bm_ = _band(d_main, w)
        bp = _band(d_halo - HALO, w)
        bn = _band(d_halo + tm, w)
        band = jnp.concatenate([bm_, bp, bn], axis=1)
        tot = (_dot(band, jnp.concatenate([z_hi[:, cs], zp_hi[:, cs], zn_hi[:, cs]], axis=0))
               + _dot(band, jnp.concatenate([z_lo[:, cs], zp_lo[:, cs], zn_lo[:, cs]], axis=0)))
        cnt = (jnp.minimum(pos + w // 2, seq_len) - jnp.maximum(pos - w // 2, 0)).astype(F32)
        pooled = (tot / cnt - z[:, cs]).astype(BF16)
        y_ref[:, HALF + g * B_GW:HALF + (g + 1) * B_GW] = (
            _dot(pooled, wp_ref[g]) * ps_ref[:, cs]).astype(BF16)


def _even_mix(uv, z, ln_g, ln_b, ws, bs, wp, ps, tm):
    n_rows = z.shape[0]
    hb = tm // HALO
    n_hblk = n_rows // HALO
    return pl.pallas_call(
        functools.partial(_even_mix_kernel, tm=tm),
        grid=(n_rows // tm,),
        in_specs=[pl.BlockSpec((tm, HALF), lambda i: (i, 0)),
                  pl.BlockSpec((tm, HALF), lambda i: (i, 1)),
                  pl.BlockSpec((tm, HALF), lambda i: (i, 0)),
                  pl.BlockSpec((HALO, HALF), lambda i: (jnp.maximum(i * hb - 1, 0), 0)),
                  pl.BlockSpec((HALO, HALF), lambda i: (jnp.minimum((i + 1) * hb, n_hblk - 1), 0)),
                  _const_spec(1, HALF), _const_spec(1, HALF),
                  _const_spec(A_GROUPS, CHUNK, CHUNK), _const_spec(A_GROUPS, CHUNK, 1),
                  _const_spec(len(POOL_WINDOWS), B_GW, B_GW), _const_spec(1, HALF)],
        out_specs=pl.BlockSpec((tm, D), lambda i: (i, 0)),
        out_shape=jax.ShapeDtypeStruct((n_rows, D), BF16),
        compiler_params=_cparams(("arbitrary",), 40),
        name="even_mix",
    )(uv, uv, z, z, z, ln_g, ln_b, ws, bs, wp, ps)


def _mix_out_kernel(ya_ref, yb_ref, w_ref, xl_ref, xc_ref, g1_ref, n2_ref, sh_ref, sc_ref, rw_ref,
                    xo_ref, hp_ref, lg_ref):
    o = _dot(ya_ref[...], w_ref[0]) + _dot(yb_ref[...], w_ref[1])
    xn = _two_stream_rows(xl_ref, xc_ref) + g1_ref[...] * o
    xo_ref[...] = xn
    h = _rms_mod(xn, n2_ref[...], sh_ref[...], sc_ref[...])
    _store_tile_rows(hp_ref, _pack_halves(h))
    lg_ref[...] = _dot_nt(rw_ref[...], h.astype(BF16))


def _mix_out(ya, ya_col, yb, yb_col, w_out, xl, xc, n_rows, mod_l, norm2_g, rw, tm):
    group_fn = _group_fn(tm)
    return pl.pallas_call(
        _mix_out_kernel,
        grid=(n_rows // tm,),
        in_specs=[pl.BlockSpec((tm, HALF), lambda i: (i, ya_col)),
                  pl.BlockSpec((tm, HALF), lambda i: (i, yb_col)),
                  _resident_spec(2, HALF, D)] + _two_stream_specs(tm) + [
                  _mod_spec(2, group_fn), _const_spec(1, D), _mod_spec(3, group_fn), _mod_spec(4, group_fn),
                  _const_spec(N_EXPERTS, D)],
        out_specs=[pl.BlockSpec((tm, D), lambda i: (i, 0)),
                   pl.BlockSpec((tm * LANE_BLKS, 128), lambda i: (i, 0)),
                   pl.BlockSpec((N_EXPERTS, tm), lambda i: (0, i))],
        out_shape=[jax.ShapeDtypeStruct((n_rows, D), F32),
                   jax.ShapeDtypeStruct((n_rows * LANE_BLKS, 128), U32),
                   jax.ShapeDtypeStruct((N_EXPERTS, n_rows), F32)],
        compiler_params=_cparams(("arbitrary",), 56),
        name="mix_out",
    )(ya, yb, w_out, xl, xc, mod_l, norm2_g, mod_l, mod_l, rw)


def _rope(x, cos, sin_signed, first_half):
    partner = jnp.where(first_half, pltpu.roll(x, 128 - AXIS_DIM // 2, 1), pltpu.roll(x, AXIS_DIM // 2, 1))
    return x * cos + partner * sin_signed


def _odd_in_kernel(x_ref, g_ref, sh_ref, sc_ref, w_ref, wkv_ref, cos_ref, sin_ref,
                   zc_ref, q_ref, k_ref, v_ref, h_ref):
    tm = x_ref.shape[0]
    h_ref[...] = _rms_mod(x_ref[...], g_ref[...], sh_ref[...], sc_ref[...]).astype(BF16)
    first_half = (lax.broadcasted_iota(I32, (tm, 128), 1) % AXIS_DIM) < (AXIS_DIM // 2)
    cos = cos_ref[...]
    sin = sin_ref[...]
    for b in range(HALF // 256):
        a = _dot(h_ref[...], w_ref[:, b * 256:(b + 1) * 256])
        gate = _dot(h_ref[...], w_ref[:, HALF + b * 256:HALF + (b + 1) * 256])
        zc_ref[:, b * 256:(b + 1) * 256] = a * jax.nn.sigmoid(gate)
    for b in range(HALF // 256):
        q = _dot(h_ref[...], w_ref[:, 2 * HALF + b * 256:2 * HALF + (b + 1) * 256])
        for s in range(2):
            cs = slice(b * 256 + s * 128, b * 256 + (s + 1) * 128)
            q_ref[:, cs] = (_rope(q[:, s * 128:(s + 1) * 128], cos, sin, first_half)
                            * (HEAD_DIM ** -0.5)).astype(BF16)
    kv = _dot(h_ref[...], wkv_ref[...])
    k_ref[...] = _rope(kv[:, :KV_W], cos, sin, first_half).astype(BF16)
    v_ref[...] = kv[:, KV_W:].astype(BF16)


def _odd_in(x, mod_l, norm_g, w_main, w_kv, cos_t, sin_t, tm):
    n_rows = x.shape[0]
    group_fn = _group_fn(tm)
    lat_tiles = N_LAT // tm
    pos_blk = lambda i: (jnp.where(i < lat_tiles, i % (SEQ // tm), SEQ // tm), 0)
    row = lambda w: pl.BlockSpec((tm, w), lambda i: (i, 0))
    return pl.pallas_call(
        _odd_in_kernel,
        grid=(n_rows // tm,),
        in_specs=[pl.BlockSpec((tm, D), lambda i: (i, 0)),
                  _const_spec(1, D),
                  _mod_spec(0, group_fn), _mod_spec(1, group_fn),
                  _resident_spec(D, 3 * HALF),
                  _resident_spec(D, 2 * KV_W),
                  pl.BlockSpec((tm, 128), pos_blk), pl.BlockSpec((tm, 128), pos_blk)],
        out_specs=[row(HALF), row(HALF), row(KV_W), row(KV_W)],
        out_shape=[jax.ShapeDtypeStruct((n_rows, HALF), F32),
                   jax.ShapeDtypeStruct((n_rows, HALF), BF16),
                   jax.ShapeDtypeStruct((n_rows, KV_W), BF16),
                   jax.ShapeDtypeStruct((n_rows, KV_W), BF16)],
        scratch_shapes=[pltpu.VMEM((tm, D), BF16)],
        compiler_params=_cparams(("arbitrary",), 48),
        name="odd_in",
    )(x, norm_g, mod_l, mod_l, w_main, w_kv, cos_t, sin_t)


def _conv_kernel(z_ref, zp_ref, zn_ref, w_ref, b_ref, lng_ref, lnb_ref, y_ref, ze_ref, zs_ref, c_ref, *, tm):
    first, last, _, _ = _seq_tile(pl.program_id(0), tm)
    ze_ref[0:CONV_HALO, :] = jnp.where(first, 0.0, zp_ref[...])
    ze_ref[CONV_HALO:CONV_HALO + tm, :] = z_ref[...]
    ze_ref[CONV_HALO + tm:, :] = jnp.where(last, 0.0, zn_ref[...])
    rc = 64
    base = CONV_HALO - CONV_W // 2
    n_sh = zs_ref.shape[1]
    for b in range(8):
        zs_ref[b] = ze_ref[b:b + n_sh, :]

    def lane_block(cb, _):
        cs = pl.ds(pl.multiple_of(cb * 128, 128), 128)
        for r in range(tm // rc):
            acc = jnp.zeros((rc, 128), F32)
            for t in range(CONV_W):
                off = base + t
                start = r * rc + (off // 8) * 8
                acc = acc + w_ref[t:t + 1, cs] * zs_ref[off % 8, start:start + rc, cs]
            c_ref[r * rc:(r + 1) * rc, cs] = acc
        return 0

    lax.fori_loop(0, HALF // 128, lane_block, 0)
    y = _layer_norm(c_ref[...] + b_ref[...], lng_ref[...], lnb_ref[...])
    y_ref[...] = _silu(y).astype(BF16)


def _conv_module(zc, n_rows, conv_w, conv_b, ln_g, ln_b, tm):
    hb = tm // CONV_HALO
    n_hblk = zc.shape[0] // CONV_HALO
    return pl.pallas_call(
        functools.partial(_conv_kernel, tm=tm),
        grid=(n_rows // tm,),
        in_specs=[pl.BlockSpec((tm, HALF), lambda i: (i, 0)),
                  pl.BlockSpec((CONV_HALO, HALF), lambda i: (jnp.maximum(i * hb - 1, 0), 0)),
                  pl.BlockSpec((CONV_HALO, HALF), lambda i: (jnp.minimum((i + 1) * hb, n_hblk - 1), 0)),
                  _const_spec(CONV_W, HALF), _const_spec(1, HALF), _const_spec(1, HALF), _const_spec(1, HALF)],
        out_specs=pl.BlockSpec((tm, HALF), lambda i: (i, 0)),
        out_shape=jax.ShapeDtypeStruct((n_rows, HALF), BF16),
        scratch_shapes=[pltpu.VMEM((tm + 2 * CONV_HALO, HALF), F32),
                        pltpu.VMEM((8, tm + 2 * CONV_HALO - 8, HALF), F32),
                        pltpu.VMEM((tm, HALF), F32)],
        compiler_params=_cparams(("arbitrary",), 40),
        name="conv_module",
    )(zc, zc, zc, conv_w, conv_b, ln_g, ln_b)


def _attn_kernel(sink_ref, q_ref, kp_ref, kc_ref, kn_ref, vp_ref, vc_ref, vn_ref, kx_ref, vx_ref, o_ref):
    i = pl.program_id(1)
    n_keys = 3 * ATT_BLK + CTX_LEN
    kb = jnp.concatenate([kp_ref[...], kc_ref[...], kn_ref[...], kx_ref[...]], axis=0)
    vb = jnp.concatenate([vp_ref[...], vc_ref[...], vn_ref[...], vx_ref[...]], axis=0)
    key_lane = lax.broadcasted_iota(I32, (n_keys, 2 * HEAD_DIM), 1)
    k_head = [jnp.where(key_lane < HEAD_DIM, kb, jnp.zeros_like(kb)),
              jnp.where(key_lane >= HEAD_DIM, kb, jnp.zeros_like(kb))]
    r = lax.broadcasted_iota(I32, (ATT_BLK, n_keys), 0)
    c = lax.broadcasted_iota(I32, (ATT_BLK, n_keys), 1)
    kpos = c + (i - 1) * ATT_BLK
    band_ok = lax.bitcast_convert_type(c - r, U32) <= U32(2 * WINDOW)
    in_seq = lax.bitcast_convert_type(kpos, U32) < U32(SEQ)
    bias = jnp.where(c >= 3 * ATT_BLK, 0.0, jnp.where(band_ok, jnp.where(in_seq, 0.0, -jnp.inf), -jnp.inf))
    gs = 4
    rows = gs * ATT_BLK
    out_lane = lax.broadcasted_iota(I32, (rows, 2 * HEAD_DIM), 1)
    bias_s = jnp.concatenate([bias] * gs, axis=0)
    for g0 in range(0, Q_PER_KV, gs):
        qs = jnp.concatenate([q_ref[:, g * 128:(g + 1) * 128] for g in range(g0, g0 + gs)], axis=0)
        outs = []
        for kvh in range(N_KV_HEADS):
            sk = jnp.concatenate([jnp.full((ATT_BLK, 1), sink_ref[kvh * Q_PER_KV + g], F32)
                                  for g in range(g0, g0 + gs)], axis=0)
            s = _dot_nt(qs, k_head[kvh]) + bias_s
            m = jnp.maximum(jnp.max(s, axis=-1, keepdims=True), sk)
            e = jnp.exp(s - m)
            den = jnp.sum(e, axis=-1, keepdims=True) + jnp.exp(sk - m)
            outs.append(_dot(e.astype(BF16), vb) / den)
        o = jnp.where(out_lane < HEAD_DIM, outs[0], outs[1]).astype(BF16)
        for j in range(gs):
            o_ref[:, (g0 + j) * 128:(g0 + j + 1) * 128] = o[j * ATT_BLK:(j + 1) * ATT_BLK, :]


def _attention(q, k, v, sink):
    nb = SEQ // ATT_BLK
    ctx0 = N_LAT // CTX_LEN
    blk = lambda w, fn: pl.BlockSpec((ATT_BLK, w), fn)
    prev = lambda b, i, s: (b * nb + jnp.maximum(i - 1, 0), 0)
    cur = lambda b, i, s: (b * nb + i, 0)
    nxt = lambda b, i, s: (b * nb + jnp.minimum(i + 1, nb - 1), 0)
    ctx = pl.BlockSpec((CTX_LEN, KV_W), lambda b, i, s: (ctx0 + b, 0))
    grid_spec = pltpu.PrefetchScalarGridSpec(
        num_scalar_prefetch=1,
        grid=(BATCH, nb),
        in_specs=[blk(HALF, cur), blk(KV_W, prev), blk(KV_W, cur), blk(KV_W, nxt),
                  blk(KV_W, prev), blk(KV_W, cur), blk(KV_W, nxt), ctx, ctx],
        out_specs=blk(HALF, cur),
    )
    return pl.pallas_call(
        _attn_kernel,
        grid_spec=grid_spec,
        out_shape=jax.ShapeDtypeStruct((N_LAT, HALF), BF16),
        compiler_params=_cparams(("arbitrary", "arbitrary"), 32),
        name="attention",
    )(sink, q, k, k, k, v, v, v, k, v)


def _first_argmax(x, iota, n):
    m = jnp.max(x, axis=0, keepdims=True)
    first = jnp.min(jnp.where(x == m, iota, n), axis=0, keepdims=True)
    return m, first


def _route_kernel(lg_ref, bias_ref, idx_ref, gate_ref, rank_ref, cnt_ref, carry_ref):
    tm = lg_ref.shape[1]

    @pl.when(pl.program_id(0) == 0)
    def _():
        carry_ref[...] = jnp.zeros_like(carry_ref)

    scores = jax.nn.sigmoid(lg_ref[...])
    biased = scores + bias_ref[...]
    sub = lax.broadcasted_iota(I32, (PER_GROUP, tm), 0)
    blocks = [biased[g * PER_GROUP:(g + 1) * PER_GROUP, :] for g in range(N_EXPERT_GROUPS)]
    gs = []
    for blk in blocks:
        m1, f1 = _first_argmax(blk, sub, PER_GROUP)
        m2 = jnp.max(jnp.where(sub == f1, -jnp.inf, blk), axis=0, keepdims=True)
        gs.append(m1 + m2)
    gs = jnp.concatenate(gs, axis=0)
    giota = lax.broadcasted_iota(I32, (N_EXPERT_GROUPS, tm), 0)
    keep = jnp.zeros((N_EXPERT_GROUPS, tm), F32)
    for _ in range(TOPK_GROUPS):
        _, f = _first_argmax(gs, giota, N_EXPERT_GROUPS)
        hit = giota == f
        keep = jnp.where(hit, 1.0, keep)
        gs = jnp.where(hit, -jnp.inf, gs)
    cur = jnp.concatenate([jnp.where(keep[g:g + 1, :] > 0.0, blocks[g], -jnp.inf)
                           for g in range(N_EXPERT_GROUPS)], axis=0)
    eiota = lax.broadcasted_iota(I32, (N_EXPERTS, tm), 0)
    chosen = jnp.zeros((N_EXPERTS, tm), F32)
    idx, sel = [], []
    for _ in range(TOP_K):
        _, f = _first_argmax(cur, eiota, N_EXPERTS)
        hit = eiota == f
        idx.append(f)
        sel.append(jnp.sum(jnp.where(hit, scores, 0.0), axis=0, keepdims=True))
        cur = jnp.where(hit, -jnp.inf, cur)
        chosen = jnp.where(hit, 1.0, chosen)
    sel = jnp.concatenate(sel, axis=0)
    idx = jnp.concatenate(idx, axis=0)
    gate_ref[...] = sel / jnp.sum(sel, axis=0, keepdims=True) * ROUTED_SCALE
    idx_ref[...] = idx
    before = jnp.where(lax.broadcasted_iota(I32, (tm, tm), 0) < lax.broadcasted_iota(I32, (tm, tm), 1), 1.0, 0.0)
    rank = _dot(chosen.astype(BF16), before.astype(BF16)) + carry_ref[:, 0:1]
    rank_ref[...] = jnp.concatenate(
        [jnp.sum(jnp.where(eiota == idx[k:k + 1, :], rank, 0.0), axis=0, keepdims=True) for k in range(TOP_K)],
        axis=0).astype(I32)
    carry_ref[...] = carry_ref[...] + jnp.sum(chosen, axis=1, keepdims=True)
    cnt_ref[...] = carry_ref[...]


def _route(logits_t, router_bias, tm):
    n_tok = logits_t.shape[1]
    tok = lambda rows: pl.BlockSpec((rows, tm), lambda i: (0, i))
    return pl.pallas_call(
        _route_kernel,
        grid=(n_tok // tm,),
        in_specs=[tok(N_EXPERTS), _const_spec(N_EXPERTS, 1)],
        out_specs=[tok(TOP_K), tok(TOP_K), tok(TOP_K), _const_spec(N_EXPERTS, 128)],
        out_shape=[jax.ShapeDtypeStruct((TOP_K, n_tok), I32),
                   jax.ShapeDtypeStruct((TOP_K, n_tok), F32),
                   jax.ShapeDtypeStruct((TOP_K, n_tok), I32),
                   jax.ShapeDtypeStruct((N_EXPERTS, 128), F32)],
        scratch_shapes=[pltpu.VMEM((N_EXPERTS, 128), F32)],
        compiler_params=_cparams(("arbitrary",), 32),
        name="route",
    )(logits_t, router_bias.astype(F32).reshape(N_EXPERTS, 1))


def _moe_kernel(be_ref, na_ref, rp_ref, nx_ref, src_ref, srcn_ref, dst_ref, h_hbm, wg_hbm, wu_hbm, wd_hbm, y_hbm,
                xb, yb, wgf, wuf, wdf, wgb, wub, wdb, sem_in, sem_out, sem_w, *, layer):
    i = pl.program_id(0)
    n_steps = pl.num_programs(0)
    n_act = na_ref[0]
    cur = i % 2
    oth = 1 - cur
    bm = xb.shape[1] // LANE_BLKS

    def tile(ref, row):
        start = row * LANE_BLKS
        if not isinstance(row, int):
            start = pl.multiple_of(start, LANE_BLKS)
        return ref.at[pl.ds(start, LANE_BLKS), :]

    def gather(idx_ref, s):
        for r in range(bm):
            pltpu.make_async_copy(tile(h_hbm, idx_ref[0, 0, r]), tile(xb.at[s], r), sem_in.at[s]).start()

    def scatter(idx_ref, s):
        for r in range(bm):
            pltpu.make_async_copy(tile(yb.at[s], r), tile(y_hbm, idx_ref[0, 0, r]), sem_out.at[s]).start()

    def wait_gather(s):
        pltpu.make_async_copy(h_hbm.at[pl.ds(0, bm * LANE_BLKS), :], xb.at[s], sem_in.at[s]).wait()

    def wait_scatter(s):
        pltpu.make_async_copy(yb.at[s], y_hbm.at[pl.ds(0, bm * LANE_BLKS), :], sem_out.at[s]).wait()

    def weight_copies(e, s):
        return [pltpu.make_async_copy(w_hbm.at[layer, e], stage.at[s], sem_w.at[s])
                for w_hbm, stage in ((wg_hbm, wgf), (wu_hbm, wuf), (wd_hbm, wdf))]

    @pl.when(i == 0)
    def _():
        gather(src_ref, 0)
        for cp in weight_copies(be_ref[0], 0):
            cp.start(priority=1)

    @pl.when(i < n_act)
    def _():
        wait_gather(cur)

        @pl.when(i >= 2)
        def _():
            wait_scatter(cur)

        @pl.when(jnp.logical_or(i == 0, be_ref[i] != be_ref[jnp.maximum(i - 1, 0)]))
        def _():
            e = be_ref[i]
            s = rp_ref[e]
            for cp in weight_copies(e, s):
                cp.wait()

            @pl.when(nx_ref[e] >= 0)
            def _():
                for cp in weight_copies(nx_ref[e], 1 - s):
                    cp.start(priority=1)

            wgb[...] = wgf[s].astype(BF16)
            wub[...] = wuf[s].astype(BF16)
            wdb[...] = wdf[s].astype(BF16)

        gather(srcn_ref, oth)
        x = _unpack_bf16(_load_tile_rows(xb.at[cur], bm))
        hid = (_silu(_dot(x, wgb[...])) * _dot(x, wub[...])).astype(BF16)
        _store_tile_rows(yb.at[cur], _pack_halves(_dot(hid, wdb[...])))
        scatter(dst_ref, cur)

    @pl.when(i == n_steps - 1)
    def _():
        wait_gather(n_act % 2)
        wait_scatter((n_act + 1) % 2)

        @pl.when(n_act >= 2)
        def _():
            wait_scatter(n_act % 2)


def _moe_experts(layer, h2p, plan, w_gate, w_up, w_down):
    blk_exp, n_act, run_par, nxt_exp, src, dst = plan
    n_blocks = blk_exp.shape[0]
    bm = MOE_BM
    smem_blk = lambda fn: pl.BlockSpec((1, 1, bm), fn, memory_space=pltpu.SMEM)
    hbm = pl.BlockSpec(memory_space=pl.ANY)
    grid_spec = pltpu.PrefetchScalarGridSpec(
        num_scalar_prefetch=4,
        grid=(n_blocks,),
        in_specs=[smem_blk(lambda i, *_: (i, 0, 0)),
                  smem_blk(lambda i, *_: (jnp.minimum(i + 1, n_blocks - 1), 0, 0)),
                  smem_blk(lambda i, *_: (i, 0, 0)),
                  hbm, hbm, hbm, hbm],
        out_specs=hbm,
        scratch_shapes=[pltpu.VMEM((2, bm * LANE_BLKS, 128), U32), pltpu.VMEM((2, bm * LANE_BLKS, 128), U32),
                        pltpu.VMEM((2, D, EXPERT_FF), F32), pltpu.VMEM((2, D, EXPERT_FF), F32),
                        pltpu.VMEM((2, EXPERT_FF, D), F32),
                        pltpu.VMEM((D, EXPERT_FF), BF16), pltpu.VMEM((D, EXPERT_FF), BF16),
                        pltpu.VMEM((EXPERT_FF, D), BF16),
                        pltpu.SemaphoreType.DMA((2,)), pltpu.SemaphoreType.DMA((2,)),
                        pltpu.SemaphoreType.DMA((2,))],
    )
    src3 = src.reshape(n_blocks, 1, bm)
    return pl.pallas_call(
        functools.partial(_moe_kernel, layer=layer),
        grid_spec=grid_spec,
        out_shape=jax.ShapeDtypeStruct((n_blocks * bm * LANE_BLKS, 128), U32),
        compiler_params=_cparams(("arbitrary",), 48),
        name="moe_experts",
    )(blk_exp, n_act, run_par, nxt_exp, src3, src3, dst.reshape(n_blocks, 1, bm), h2p, w_gate, w_up, w_down)


def _combine_kernel(*refs, final):
    x_ref, h_ref, wg_ref, wu_ref, wd_ref, gt_ref, g2_ref = refs[:7]
    y_refs = refs[7:7 + TOP_K]
    fg_ref = refs[7 + TOP_K] if final else None
    o_ref, sh_ref = refs[-2:]
    tm = x_ref.shape[0]
    xh = _unpack_bf16(_load_tile_rows(h_ref, tm))
    hid = (_silu(_dot(xh, wg_ref[...])) * _dot(xh, wu_ref[...])).astype(BF16)
    sh_ref[...] = _dot(hid, wd_ref[...])
    gt = gt_ref[...]
    for b in range(LANE_BLKS):
        c_lo = slice(b * 128, (b + 1) * 128)
        c_hi = slice(HALF + b * 128, HALF + (b + 1) * 128)
        acc_lo = sh_ref[:, c_lo]
        acc_hi = sh_ref[:, c_hi]
        for k in range(TOP_K):
            lo, hi = _unpack_halves(y_refs[k][pl.ds(b, tm, stride=LANE_BLKS), :])
            acc_lo = acc_lo + lo * gt[:, k:k + 1]
            acc_hi = acc_hi + hi * gt[:, k:k + 1]
        o_ref[:, c_lo] = x_ref[:, c_lo] + g2_ref[:, c_lo] * acc_lo
        o_ref[:, c_hi] = x_ref[:, c_hi] + g2_ref[:, c_hi] * acc_hi
    if final:
        xn = o_ref[...]
        o_ref[...] = xn * lax.rsqrt(jnp.mean(xn * xn, axis=-1, keepdims=True) + EPS) * fg_ref[...]


def _combine(x, h2p, shared_w, gates, mod_l, y_tk, n_tok, final_g, tm):
    nt = n_tok // tm
    group_fn = _group_fn(tm)
    row = pl.BlockSpec((tm, D), lambda i: (i, 0))
    packed = lambda fn: pl.BlockSpec((tm * LANE_BLKS, 128), fn)
    in_specs = [row, packed(lambda i: (i, 0)),
                _resident_spec(D, EXPERT_FF), _resident_spec(D, EXPERT_FF), _resident_spec(EXPERT_FF, D),
                pl.BlockSpec((tm, TOP_K), lambda i: (i, 0)), _mod_spec(5, group_fn)]
    in_specs += [packed(functools.partial(lambda i, k: (k * nt + i, 0), k=k)) for k in range(TOP_K)]
    args = [x, h2p, *shared_w, gates, mod_l] + [y_tk] * TOP_K
    final = final_g is not None
    if final:
        in_specs.append(_const_spec(1, D))
        args.append(final_g)
    return pl.pallas_call(
        functools.partial(_combine_kernel, final=final),
        grid=(nt,),
        in_specs=in_specs,
        out_specs=row,
        out_shape=jax.ShapeDtypeStruct((n_tok, D), F32),
        scratch_shapes=[pltpu.VMEM((tm, D), F32)],
        compiler_params=_cparams(("arbitrary",), 48),
        name="moe_combine",
    )(*args)


def _dispatch_plan(idx_t, rank_t, cnt, n_tok):
    bm = MOE_BM
    n_assign = n_tok * TOP_K
    n_blocks = n_assign // bm + N_EXPERTS
    n_rows = n_blocks * bm
    sizes = cnt[:, 0].astype(I32)
    padded = (sizes + bm - 1) // bm * bm
    pad_end = jnp.cumsum(padded)
    pad_start = pad_end - padded
    blk_first = jnp.arange(n_blocks, dtype=I32) * bm
    blk_exp = jnp.minimum(jnp.sum((pad_end[None, :] <= blk_first[:, None]).astype(I32), axis=1), N_EXPERTS - 1)
    n_act = (pad_end[-1] // bm).astype(I32).reshape(1)
    experts = jnp.arange(N_EXPERTS, dtype=I32)
    slot = jnp.sum(jnp.where(idx_t[:, :, None] == experts, pad_start, 0), axis=-1) + rank_t
    ids = (jnp.arange(n_tok, dtype=I32)[None, :] * TOP_K + jnp.arange(TOP_K, dtype=I32)[:, None] + 1)
    inv = (jnp.zeros((n_rows,), I32).at[slot.reshape(-1)].add(ids.reshape(-1)) - 1).reshape(n_blocks, bm)
    valid = inv >= 0
    tok = inv // TOP_K
    spare = n_assign + blk_exp[:, None] * bm + jnp.arange(bm, dtype=I32)[None, :]
    src = jnp.where(valid, tok, 0).astype(I32)
    dst = jnp.where(valid, (inv - tok * TOP_K) * n_tok + tok, spare).astype(I32)
    used = sizes > 0
    run_par = ((jnp.cumsum(used.astype(I32)) - 1) % 2).astype(I32)
    first_used_from = lax.cummin(jnp.where(used, experts, N_EXPERTS), axis=0, reverse=True)
    nxt = jnp.concatenate([first_used_from[1:], jnp.full((1,), N_EXPERTS, I32)])
    nxt_exp = jnp.where(nxt < N_EXPERTS, nxt, -1).astype(I32)
    return blk_exp, n_act, run_par, nxt_exp, src, dst


def _moe_layer(layer, x, h2p, logits_t, router_bias, mod_l, exp_w, shared_w, final_g):
    n_tok = h2p.shape[0] // LANE_BLKS
    idx_t, gates_t, rank_t, cnt = _route(logits_t, router_bias, TM_ROUTE)
    plan = _dispatch_plan(idx_t, rank_t, cnt, n_tok)
    y_tk = _moe_experts(layer, h2p, plan, *exp_w)
    return _combine(x, h2p, [w[layer].astype(BF16) for w in shared_w], gates_t.T, mod_l, y_tk, n_tok,
                    final_g, TM_COMBINE)


def _rope_tables():
    rows = SEQ // GRID_W
    row = jnp.broadcast_to(jnp.arange(rows)[:, None], (rows, GRID_W)).reshape(-1).astype(F32)
    col = jnp.broadcast_to(jnp.arange(GRID_W)[None, :], (rows, GRID_W)).reshape(-1).astype(F32)
    inv = ROPE_BASE ** (-jnp.arange(0, AXIS_DIM, 2, dtype=F32) / AXIS_DIM)
    ang_r = row[:, None] * inv
    ang_c = col[:, None] * inv
    ang = jnp.concatenate([ang_r, ang_r, ang_c, ang_c], axis=-1)
    cos, sin = jnp.cos(ang), jnp.sin(ang)
    sign = jnp.where((jnp.arange(HEAD_DIM) % AXIS_DIM) < AXIS_DIM // 2, -1.0, 1.0).astype(F32)
    ident = TM_PROJ
    cos_t = jnp.concatenate([jnp.tile(cos, (1, 2)), jnp.ones((ident, 128), F32)], axis=0)
    sin_t = jnp.concatenate([jnp.tile(sin * sign, (1, 2)), jnp.zeros((ident, 128), F32)], axis=0)
    return cos_t, sin_t


def _heads_g_major(w, axis):
    shape = w.shape
    w = w.reshape(shape[:axis] + (N_KV_HEADS, Q_PER_KV, HEAD_DIM) + shape[axis + 1:])
    return jnp.swapaxes(w, axis, axis + 1).reshape(shape)


def kernel(x, c, ctx, c_ctx, ada_w, ada_b, norm1_g, norm2_g, even_w_in, gmlp_ln_g, gmlp_ln_b, gmlp_ws, gmlp_bs, pool_w, pool_scale, even_w_out, odd_w_in, conv_w, conv_b, conv_ln_g, conv_ln_b, attn_sink, odd_w_out, router_w, router_bias, exp_w_gate, exp_w_up, exp_w_down, shared_w_gate, shared_w_up, shared_w_down, final_g):
    mod = _ada_mod(c, c_ctx, ada_w, ada_b)
    row = lambda a: a.reshape(1, -1)
    exp_w = (exp_w_gate, exp_w_up, exp_w_down)
    shared_w = (shared_w_gate, shared_w_up, shared_w_down)

    def router_t(i):
        return router_w[i].T.astype(BF16)

    xl = x.reshape(N_LAT, D)
    xc = ctx.reshape(N_CTX, D)

    uv, z = _even_in(xl, xc, mod[0], row(norm1_g[0]), even_w_in[0].astype(BF16), TM_PROJ)
    y = _even_mix(uv, z, row(gmlp_ln_g[0]), row(gmlp_ln_b[0]), gmlp_ws[0].astype(BF16),
                  gmlp_bs[0].reshape(A_GROUPS, CHUNK, 1), pool_w[0].astype(BF16), row(pool_scale[0]), TM_SEQ)
    x1, h2p, lg = _mix_out(y, 0, y, 1, even_w_out[0].astype(BF16).reshape(2, HALF, D), xl, xc, N_ALL, mod[0],
                           row(norm2_g[0]), router_t(0), TM_PROJ)
    x1 = _moe_layer(0, x1, h2p, lg, router_bias[0], mod[0], exp_w, shared_w, None)

    w_in1 = odd_w_in[0]
    w_main = jnp.concatenate([w_in1[:, :2 * HALF], _heads_g_major(w_in1[:, 2 * HALF:3 * HALF], 1)],
                             axis=1).astype(BF16)
    w_kv = w_in1[:, 3 * HALF:].astype(BF16)
    w_out1 = odd_w_out[0]
    w_out1 = jnp.stack([w_out1[:HALF], _heads_g_major(w_out1[HALF:], 0)], axis=0).astype(BF16)
    cos_t, sin_t = _rope_tables()
    zc, q, k, v = _odd_in(x1, mod[1], row(norm1_g[1]), w_main, w_kv, cos_t, sin_t, TM_PROJ)
    y_conv = _conv_module(zc, N_LAT, conv_w[0], row(conv_b[0]), row(conv_ln_g[0]), row(conv_ln_b[0]), TM_SEQ)
    y_attn = _attention(q, k, v, attn_sink[0].astype(F32))
    x2, h2p, lg = _mix_out(y_conv, 0, y_attn, 0, w_out1, x1, x1, N_LAT, mod[1], row(norm2_g[1]),
                           router_t(1), TM_PROJ)
    out = _moe_layer(1, x2, h2p, lg, router_bias[1], mod[1], exp_w, shared_w, row(final_g))
    return out.reshape(BATCH, SEQ, D)
```

```python
import functools

import jax
import jax.numpy as jnp
from jax import lax
from jax.experimental import pallas as pl
from jax.experimental.pallas import tpu as pltpu

F32 = jnp.float32
BF16 = jnp.bfloat16
U32 = jnp.uint32
I32 = jnp.int32

D = 2048
BATCH = 4
SEQ = 4096
DEPTH = 2
GRID_W = 64
CTX_LEN = 256
HALF = D // 2
CHUNK = 128
A_GROUPS = 4
A_GW = HALF // A_GROUPS
POOL_WINDOWS = (2, 4, 8, 16)
B_GW = HALF // len(POOL_WINDOWS)
CONV_W = 31
HEAD_DIM = 64
N_Q_HEADS = HALF // HEAD_DIM
N_KV_HEADS = 2
Q_PER_KV = N_Q_HEADS // N_KV_HEADS
KV_W = N_KV_HEADS * HEAD_DIM
ATT_BLK = 128
WINDOW = 128
AXIS_DIM = HEAD_DIM // 2
ROPE_BASE = 10000.0
N_EXPERTS = 64
N_EXPERT_GROUPS = 8
PER_GROUP = N_EXPERTS // N_EXPERT_GROUPS
TOPK_GROUPS = 4
TOP_K = 8
EXPERT_FF = 512
ROUTED_SCALE = 2.5
EPS = 1e-6

N_LAT = BATCH * SEQ
N_CTX = BATCH * CTX_LEN
N_ALL = N_LAT + N_CTX
CTX_GROUP = BATCH
HALO = 128
CONV_HALO = 16
MOE_BM = 256

TM_PROJ = 512
TM_SEQ = 256
TM_ROUTE = 512
TM_COMBINE = 256

MIB = 1024 * 1024


def _cparams(sem, vmem_mib):
    return pltpu.CompilerParams(dimension_semantics=sem, vmem_limit_bytes=vmem_mib * MIB)


def _dot(a, b):
    return jnp.dot(a, b, preferred_element_type=F32)


def _dot_nt(a, b):
    return lax.dot_general(a, b, (((1,), (1,)), ((), ())), preferred_element_type=F32)


def _rms_mod(x, g, sh, sc):
    y = x * lax.rsqrt(jnp.mean(x * x, axis=-1, keepdims=True) + EPS) * g
    return y * (1.0 + sc) + sh


def _layer_norm(x, g, b):
    mu = jnp.mean(x, axis=-1, keepdims=True)
    xc = x - mu
    var = jnp.mean(xc * xc, axis=-1, keepdims=True)
    return xc * lax.rsqrt(var + EPS) * g + b


def _gelu(x):
    return 0.5 * x * (1.0 + lax.erf(x * (2.0 ** -0.5)))


def _silu(x):
    return x * jax.nn.sigmoid(x)


def _split_bf16(x):
    hi = x.astype(BF16)
    lo = (x - hi.astype(F32)).astype(BF16)
    return hi, lo


def _pack_halves(y):
    n = y.shape[1] // 2
    lo = lax.bitcast_convert_type(y[:, :n].astype(BF16).astype(F32), U32) >> 16
    hi = lax.bitcast_convert_type(y[:, n:].astype(BF16).astype(F32), U32) & U32(0xFFFF0000)
    return hi | lo


def _unpack_halves(p):
    lo = lax.bitcast_convert_type(p << 16, F32)
    hi = lax.bitcast_convert_type(p & U32(0xFFFF0000), F32)
    return lo, hi


def _unpack_bf16(p):
    lo, hi = _unpack_halves(p)
    return jnp.concatenate([lo.astype(BF16), hi.astype(BF16)], axis=1)


LANE_BLKS = HALF // 128


def _store_tile_rows(ref, packed):
    tm = packed.shape[0]
    for s in range(LANE_BLKS):
        ref[pl.ds(s, tm, stride=LANE_BLKS), :] = packed[:, s * 128:(s + 1) * 128]


def _load_tile_rows(ref, tm):
    return jnp.concatenate([ref[pl.ds(s, tm, stride=LANE_BLKS), :] for s in range(LANE_BLKS)], axis=1)


def _mod_spec(chunk, group_fn):
    return pl.BlockSpec((None, None, 1, D), lambda i, *_: (group_fn(i), chunk, 0, 0))


def _group_fn(tm):
    return lambda i: jnp.minimum(i // (SEQ // tm), CTX_GROUP)


def _const_spec(*shape):
    return pl.BlockSpec(shape, lambda *_: (0,) * len(shape))


def _ada_kernel(c_ref, w_ref, b_ref, o_ref):
    s = _silu(c_ref[...]).astype(BF16)
    o_ref[...] = _dot(s, w_ref[...].astype(BF16)) + b_ref[...]


def _ada_mod(c, c_ctx, ada_w, ada_b):
    tn = 1024
    cv = jnp.zeros((8, D), F32).at[:BATCH].set(c).at[CTX_GROUP].set(c_ctx)
    out = pl.pallas_call(
        _ada_kernel,
        grid=(DEPTH, 6 * D // tn),
        in_specs=[pl.BlockSpec((8, D), lambda l, j: (0, 0)),
                  pl.BlockSpec((None, D, tn), lambda l, j: (l, 0, j)),
                  pl.BlockSpec((None, 1, tn), lambda l, j: (l, 0, j))],
        out_specs=pl.BlockSpec((None, 8, tn), lambda l, j: (l, 0, j)),
        out_shape=jax.ShapeDtypeStruct((DEPTH, 8, 6 * D), F32),
        compiler_params=_cparams(("arbitrary", "arbitrary"), 40),
        name="ada_mod",
    )(cv, ada_w, ada_b.reshape(DEPTH, 1, 6 * D))
    return out.reshape(DEPTH, 8, 6, 1, D)


def _two_stream_specs(tm):
    lat_tiles = N_LAT // tm
    return [pl.BlockSpec((tm, D), lambda i: (jnp.minimum(i, lat_tiles - 1), 0)),
            pl.BlockSpec((tm, D), lambda i: (jnp.maximum(i - lat_tiles, 0), 0))]


def _two_stream_rows(xl_ref, xc_ref):
    tm = xl_ref.shape[0]
    return jnp.where(pl.program_id(0) < N_LAT // tm, xl_ref[...], xc_ref[...])


def _even_in_kernel(xl_ref, xc_ref, g_ref, sh_ref, sc_ref, w_ref, uv_ref, z_ref, h_ref):
    h_ref[...] = _rms_mod(_two_stream_rows(xl_ref, xc_ref), g_ref[...], sh_ref[...], sc_ref[...]).astype(BF16)
    for j in range(2):
        cs = slice(j * HALF, (j + 1) * HALF)
        uv_ref[:, cs] = _gelu(_dot(h_ref[...], w_ref[:, cs])).astype(BF16)
    z_ref[...] = _dot(h_ref[...], w_ref[:, 2 * HALF:])


def _resident_spec(*shape):
    return pl.BlockSpec(shape, lambda *_: (0,) * len(shape), pipeline_mode=pl.Buffered(1))


def _even_in(xl, xc, mod_l, norm_g, w_in, tm):
    n_rows = N_ALL
    group_fn = _group_fn(tm)
    return pl.pallas_call(
        _even_in_kernel,
        grid=(n_rows // tm,),
        in_specs=_two_stream_specs(tm) + [
                  _const_spec(1, D),
                  _mod_spec(0, group_fn), _mod_spec(1, group_fn),
                  _resident_spec(D, 3 * HALF)],
        out_specs=[pl.BlockSpec((tm, 2 * HALF), lambda i: (i, 0)),
                   pl.BlockSpec((tm, HALF), lambda i: (i, 0))],
        out_shape=[jax.ShapeDtypeStruct((n_rows, 2 * HALF), BF16),
                   jax.ShapeDtypeStruct((n_rows, HALF), F32)],
        scratch_shapes=[pltpu.VMEM((tm, D), BF16)],
        compiler_params=_cparams(("arbitrary",), 48),
        name="even_in",
    )(xl, xc, norm_g, mod_l, mod_l, w_in)


def _band(d, w):
    inside = lax.bitcast_convert_type(d + w // 2, U32) < U32(w)
    return jnp.where(inside, 1.0, 0.0).astype(BF16)


def _seq_tile(i, tm):
    lat_tiles = N_LAT // tm
    is_lat = i < lat_tiles
    it = i % (SEQ // tm)
    first = jnp.logical_or(jnp.logical_not(is_lat), it == 0)
    last = jnp.logical_or(jnp.logical_not(is_lat), it == SEQ // tm - 1)
    pos0 = jnp.where(is_lat, it * tm, 0)
    seq_len = jnp.where(is_lat, SEQ, CTX_LEN)
    return first, last, pos0, seq_len


def _even_mix_kernel(u_ref, v_ref, z_ref, zp_ref, zn_ref, lng_ref, lnb_ref, ws_ref, bs_ref,
                     wp_ref, ps_ref, y_ref, *, tm):
    first, last, pos0, seq_len = _seq_tile(pl.program_id(0), tm)
    vn = _layer_norm(v_ref[...].astype(F32), lng_ref[...], lnb_ref[...]).astype(BF16)
    for g in range(A_GROUPS):
        cs = slice(g * A_GW, (g + 1) * A_GW)
        for c in range(tm // CHUNK):
            rs = slice(c * CHUNK, (c + 1) * CHUNK)
            mixed = _dot(ws_ref[g], vn[rs, cs]) + bs_ref[g]
            y_ref[rs, cs] = (u_ref[rs, cs].astype(F32) * mixed).astype(BF16)
    z = z_ref[...]
    zp = jnp.where(first, 0.0, zp_ref[...])
    zn = jnp.where(last, 0.0, zn_ref[...])
    z_hi, z_lo = _split_bf16(z)
    zp_hi, zp_lo = _split_bf16(zp)
    zn_hi, zn_lo = _split_bf16(zn)
    d_main = (lax.broadcasted_iota(I32, (tm, tm), 1) - lax.broadcasted_iota(I32, (tm, tm), 0))
    d_halo = (lax.broadcasted_iota(I32, (tm, HALO), 1) - lax.broadcasted_iota(I32, (tm, HALO), 0))
    pos = pos0 + lax.broadcasted_iota(I32, (tm, 1), 0)
    for g, w in enumerate(POOL_WINDOWS):
        cs = slice(g * B_GW, (g + 1) * B_GW)
        bm_ = _band(d_main, w)
        bp = _band(d_halo - HALO, w)
        bn = _band(d_halo + tm, w)
        band = jnp.concatenate([bm_, bp, bn], axis=1)
        tot = (_dot(band, jnp.concatenate([z_hi[:, cs], zp_hi[:, cs], zn_hi[:, cs]], axis=0))
               + _dot(band, jnp.concatenate([z_lo[:, cs], zp_lo[:, cs], zn_lo[:, cs]], axis=0)))
        cnt = (jnp.minimum(pos + w // 2, seq_len) - jnp.maximum(pos - w // 2, 0)).astype(F32)
        pooled = (tot / cnt - z[:, cs]).astype(BF16)
        y_ref[:, HALF + g * B_GW:HALF + (g + 1) * B_GW] = (
            _dot(pooled, wp_ref[g]) * ps_ref[:, cs]).astype(BF16)


def _even_mix(uv, z, ln_g, ln_b, ws, bs, wp, ps, tm):
    n_rows = z.shape[0]
    hb = tm // HALO
    n_hblk = n_rows // HALO
    return pl.pallas_call(
        functools.partial(_even_mix_kernel, tm=tm),
        grid=(n_rows // tm,),
        in_specs=[pl.BlockSpec((tm, HALF), lambda i: (i, 0)),
                  pl.BlockSpec((tm, HALF), lambda i: (i, 1)),
                  pl.BlockSpec((tm, HALF), lambda i: (i, 0)),
                  pl.BlockSpec((HALO, HALF), lambda i: (jnp.maximum(i * hb - 1, 0), 0)),
                  pl.BlockSpec((HALO, HALF), lambda i: (jnp.minimum((i + 1) * hb, n_hblk - 1), 0)),
                  _const_spec(1, HALF), _const_spec(1, HALF),
                  _const_spec(A_GROUPS, CHUNK, CHUNK), _const_spec(A_GROUPS, CHUNK, 1),
                  _const_spec(len(POOL_WINDOWS), B_GW, B_GW), _const_spec(1, HALF)],
        out_specs=pl.BlockSpec((tm, D), lambda i: (i, 0)),
        out_shape=jax.ShapeDtypeStruct((n_rows, D), BF16),
        compiler_params=_cparams(("arbitrary",), 40),
        name="even_mix",
    )(uv, uv, z, z, z, ln_g, ln_b, ws, bs, wp, ps)


def _mix_out_kernel(ya_ref, yb_ref, w_ref, xl_ref, xc_ref, g1_ref, n2_ref, sh_ref, sc_ref, rw_ref,
                    xo_ref, hp_ref, lg_ref):
    o = _dot(jnp.concatenate([ya_ref[...], yb_ref[...]], axis=1), w_ref[...].reshape(D, D))
    xn = _two_stream_rows(xl_ref, xc_ref) + g1_ref[...] * o
    xo_ref[...] = xn
    h = _rms_mod(xn, n2_ref[...], sh_ref[...], sc_ref[...])
    _store_tile_rows(hp_ref, _pack_halves(h))
    lg_ref[...] = _dot_nt(rw_ref[...], h.astype(BF16))


def _mix_out(ya, ya_col, yb, yb_col, w_out, xl, xc, n_rows, mod_l, norm2_g, rw, tm):
    group_fn = _group_fn(tm)
    return pl.pallas_call(
        _mix_out_kernel,
        grid=(n_rows // tm,),
        in_specs=[pl.BlockSpec((tm, HALF), lambda i: (i, ya_col)),
                  pl.BlockSpec((tm, HALF), lambda i: (i, yb_col)),
                  _resident_spec(2, HALF, D)] + _two_stream_specs(tm) + [
                  _mod_spec(2, group_fn), _const_spec(1, D), _mod_spec(3, group_fn), _mod_spec(4, group_fn),
                  _const_spec(N_EXPERTS, D)],
        out_specs=[pl.BlockSpec((tm, D), lambda i: (i, 0)),
                   pl.BlockSpec((tm * LANE_BLKS, 128), lambda i: (i, 0)),
                   pl.BlockSpec((N_EXPERTS, tm), lambda i: (0, i))],
        out_shape=[jax.ShapeDtypeStruct((n_rows, D), F32),
                   jax.ShapeDtypeStruct((n_rows * LANE_BLKS, 128), U32),
                   jax.ShapeDtypeStruct((N_EXPERTS, n_rows), F32)],
        compiler_params=_cparams(("arbitrary",), 56),
        name="mix_out",
    )(ya, yb, w_out, xl, xc, mod_l, norm2_g, mod_l, mod_l, rw)


def _rope(x, cos, sin_signed, first_half):
    partner = jnp.where(first_half, pltpu.roll(x, 128 - AXIS_DIM // 2, 1), pltpu.roll(x, AXIS_DIM // 2, 1))
    return x * cos + partner * sin_signed


def _odd_in_kernel(x_ref, g_ref, sh_ref, sc_ref, w_ref, wkv_ref, cos_ref, sin_ref,
                   zc_ref, q_ref, k_ref, v_ref, h_ref):
    tm = x_ref.shape[0]
    h_ref[...] = _rms_mod(x_ref[...], g_ref[...], sh_ref[...], sc_ref[...]).astype(BF16)
    first_half = (lax.broadcasted_iota(I32, (tm, 128), 1) % AXIS_DIM) < (AXIS_DIM // 2)
    cos = cos_ref[...]
    sin = sin_ref[...]
    for b in range(HALF // 256):
        a = _dot(h_ref[...], w_ref[:, b * 256:(b + 1) * 256])
        gate = _dot(h_ref[...], w_ref[:, HALF + b * 256:HALF + (b + 1) * 256])
        zc_ref[:, b * 256:(b + 1) * 256] = a * jax.nn.sigmoid(gate)
    for b in range(HALF // 256):
        q = _dot(h_ref[...], w_ref[:, 2 * HALF + b * 256:2 * HALF + (b + 1) * 256])
        for s in range(2):
            cs = slice(b * 256 + s * 128, b * 256 + (s + 1) * 128)
            q_ref[:, cs] = (_rope(q[:, s * 128:(s + 1) * 128], cos, sin, first_half)
                            * (HEAD_DIM ** -0.5)).astype(BF16)
    kv = _dot(h_ref[...], wkv_ref[...])
    k_ref[...] = _rope(kv[:, :KV_W], cos, sin, first_half).astype(BF16)
    v_ref[...] = kv[:, KV_W:].astype(BF16)


def _odd_in(x, mod_l, norm_g, w_main, w_kv, cos_t, sin_t, tm):
    n_rows = x.shape[0]
    group_fn = _group_fn(tm)
    lat_tiles = N_LAT // tm
    pos_blk = lambda i: (jnp.where(i < lat_tiles, i % (SEQ // tm), SEQ // tm), 0)
    row = lambda w: pl.BlockSpec((tm, w), lambda i: (i, 0))
    return pl.pallas_call(
        _odd_in_kernel,
        grid=(n_rows // tm,),
        in_specs=[pl.BlockSpec((tm, D), lambda i: (i, 0)),
                  _const_spec(1, D),
                  _mod_spec(0, group_fn), _mod_spec(1, group_fn),
                  _resident_spec(D, 3 * HALF),
                  _resident_spec(D, 2 * KV_W),
                  pl.BlockSpec((tm, 128), pos_blk), pl.BlockSpec((tm, 128), pos_blk)],
        out_specs=[row(HALF), row(HALF), row(KV_W), row(KV_W)],
        out_shape=[jax.ShapeDtypeStruct((n_rows, HALF), F32),
                   jax.ShapeDtypeStruct((n_rows, HALF), BF16),
                   jax.ShapeDtypeStruct((n_rows, KV_W), BF16),
                   jax.ShapeDtypeStruct((n_rows, KV_W), BF16)],
        scratch_shapes=[pltpu.VMEM((tm, D), BF16)],
        compiler_params=_cparams(("arbitrary",), 48),
        name="odd_in",
    )(x, norm_g, mod_l, mod_l, w_main, w_kv, cos_t, sin_t)


def _conv_kernel(z_ref, zp_ref, zn_ref, w_ref, b_ref, lng_ref, lnb_ref, y_ref, ze_ref, zs_ref, c_ref, *, tm):
    first, last, _, _ = _seq_tile(pl.program_id(0), tm)
    ze_ref[0:CONV_HALO, :] = jnp.where(first, 0.0, zp_ref[...])
    ze_ref[CONV_HALO:CONV_HALO + tm, :] = z_ref[...]
    ze_ref[CONV_HALO + tm:, :] = jnp.where(last, 0.0, zn_ref[...])
    rc = 64
    base = CONV_HALO - CONV_W // 2
    n_sh = zs_ref.shape[1]
    for b in range(8):
        zs_ref[b] = ze_ref[b:b + n_sh, :]

    def lane_block(cb, _):
        cs = pl.ds(pl.multiple_of(cb * 128, 128), 128)
        for r in range(tm // rc):
            acc = jnp.zeros((rc, 128), F32)
            for t in range(CONV_W):
                off = base + t
                start = r * rc + (off // 8) * 8
                acc = acc + w_ref[t:t + 1, cs] * zs_ref[off % 8, start:start + rc, cs]
            c_ref[r * rc:(r + 1) * rc, cs] = acc
        return 0

    lax.fori_loop(0, HALF // 128, lane_block, 0)
    y = _layer_norm(c_ref[...] + b_ref[...], lng_ref[...], lnb_ref[...])
    y_ref[...] = _silu(y).astype(BF16)


def _conv_module(zc, n_rows, conv_w, conv_b, ln_g, ln_b, tm):
    hb = tm // CONV_HALO
    n_hblk = zc.shape[0] // CONV_HALO
    return pl.pallas_call(
        functools.partial(_conv_kernel, tm=tm),
        grid=(n_rows // tm,),
        in_specs=[pl.BlockSpec((tm, HALF), lambda i: (i, 0)),
                  pl.BlockSpec((CONV_HALO, HALF), lambda i: (jnp.maximum(i * hb - 1, 0), 0)),
                  pl.BlockSpec((CONV_HALO, HALF), lambda i: (jnp.minimum((i + 1) * hb, n_hblk - 1), 0)),
                  _const_spec(CONV_W, HALF), _const_spec(1, HALF), _const_spec(1, HALF), _const_spec(1, HALF)],
        out_specs=pl.BlockSpec((tm, HALF), lambda i: (i, 0)),
        out_shape=jax.ShapeDtypeStruct((n_rows, HALF), BF16),
        scratch_shapes=[pltpu.VMEM((tm + 2 * CONV_HALO, HALF), F32),
                        pltpu.VMEM((8, tm + 2 * CONV_HALO - 8, HALF), F32),
                        pltpu.VMEM((tm, HALF), F32)],
        compiler_params=_cparams(("arbitrary",), 40),
        name="conv_module",
    )(zc, zc, zc, conv_w, conv_b, ln_g, ln_b)


def _attn_kernel(sink_ref, q_ref, kp_ref, kc_ref, kn_ref, vp_ref, vc_ref, vn_ref, kx_ref, vx_ref, o_ref):
    i = pl.program_id(1)
    n_keys = 3 * ATT_BLK + CTX_LEN
    kb = jnp.concatenate([kp_ref[...], kc_ref[...], kn_ref[...], kx_ref[...]], axis=0)
    vb = jnp.concatenate([vp_ref[...], vc_ref[...], vn_ref[...], vx_ref[...]], axis=0)
    key_lane = lax.broadcasted_iota(I32, (n_keys, 2 * HEAD_DIM), 1)
    k_head = [jnp.where(key_lane < HEAD_DIM, kb, jnp.zeros_like(kb)),
              jnp.where(key_lane >= HEAD_DIM, kb, jnp.zeros_like(kb))]
    r = lax.broadcasted_iota(I32, (ATT_BLK, n_keys), 0)
    c = lax.broadcasted_iota(I32, (ATT_BLK, n_keys), 1)
    kpos = c + (i - 1) * ATT_BLK
    band_ok = lax.bitcast_convert_type(c - r, U32) <= U32(2 * WINDOW)
    in_seq = lax.bitcast_convert_type(kpos, U32) < U32(SEQ)
    bias = jnp.where(c >= 3 * ATT_BLK, 0.0, jnp.where(band_ok, jnp.where(in_seq, 0.0, -jnp.inf), -jnp.inf))
    gs = 4
    rows = gs * ATT_BLK
    out_lane = lax.broadcasted_iota(I32, (rows, 2 * HEAD_DIM), 1)
    bias_s = jnp.concatenate([bias] * gs, axis=0)
    for g0 in range(0, Q_PER_KV, gs):
        qs = jnp.concatenate([q_ref[:, g * 128:(g + 1) * 128] for g in range(g0, g0 + gs)], axis=0)
        outs = []
        for kvh in range(N_KV_HEADS):
            sk = jnp.concatenate([jnp.full((ATT_BLK, 1), sink_ref[kvh * Q_PER_KV + g], F32)
                                  for g in range(g0, g0 + gs)], axis=0)
            s = _dot_nt(qs, k_head[kvh]) + bias_s
            m = jnp.maximum(jnp.max(s, axis=-1, keepdims=True), sk)
            e = jnp.exp(s - m)
            den = jnp.sum(e, axis=-1, keepdims=True) + jnp.exp(sk - m)
            outs.append(_dot(e.astype(BF16), vb) / den)
        o = jnp.where(out_lane < HEAD_DIM, outs[0], outs[1]).astype(BF16)
        for j in range(gs):
            o_ref[:, (g0 + j) * 128:(g0 + j + 1) * 128] = o[j * ATT_BLK:(j + 1) * ATT_BLK, :]


def _attention(q, k, v, sink):
    nb = SEQ // ATT_BLK
    ctx0 = N_LAT // CTX_LEN
    blk = lambda w, fn: pl.BlockSpec((ATT_BLK, w), fn)
    prev = lambda b, i, s: (b * nb + jnp.maximum(i - 1, 0), 0)
    cur = lambda b, i, s: (b * nb + i, 0)
    nxt = lambda b, i, s: (b * nb + jnp.minimum(i + 1, nb - 1), 0)
    ctx = pl.BlockSpec((CTX_LEN, KV_W), lambda b, i, s: (ctx0 + b, 0))
    grid_spec = pltpu.PrefetchScalarGridSpec(
        num_scalar_prefetch=1,
        grid=(BATCH, nb),
        in_specs=[blk(HALF, cur), blk(KV_W, prev), blk(KV_W, cur), blk(KV_W, nxt),
                  blk(KV_W, prev), blk(KV_W, cur), blk(KV_W, nxt), ctx, ctx],
        out_specs=blk(HALF, cur),
    )
    return pl.pallas_call(
        _attn_kernel,
        grid_spec=grid_spec,
        out_shape=jax.ShapeDtypeStruct((N_LAT, HALF), BF16),
        compiler_params=_cparams(("arbitrary", "arbitrary"), 32),
        name="attention",
    )(sink, q, k, k, k, v, v, v, k, v)


def _first_argmax(x, iota, n):
    m = jnp.max(x, axis=0, keepdims=True)
    first = jnp.min(jnp.where(x == m, iota, n), axis=0, keepdims=True)
    return m, first


def _route_kernel(lg_ref, bias_ref, idx_ref, gate_ref, rank_ref, cnt_ref, carry_ref):
    tm = lg_ref.shape[1]

    @pl.when(pl.program_id(0) == 0)
    def _():
        carry_ref[...] = jnp.zeros_like(carry_ref)

    scores = jax.nn.sigmoid(lg_ref[...])
    biased = scores + bias_ref[...]
    sub = lax.broadcasted_iota(I32, (PER_GROUP, tm), 0)
    blocks = [biased[g * PER_GROUP:(g + 1) * PER_GROUP, :] for g in range(N_EXPERT_GROUPS)]
    gs = []
    for blk in blocks:
        m1, f1 = _first_argmax(blk, sub, PER_GROUP)
        m2 = jnp.max(jnp.where(sub == f1, -jnp.inf, blk), axis=0, keepdims=True)
        gs.append(m1 + m2)
    gs = jnp.concatenate(gs, axis=0)
    giota = lax.broadcasted_iota(I32, (N_EXPERT_GROUPS, tm), 0)
    keep = jnp.zeros((N_EXPERT_GROUPS, tm), F32)
    for _ in range(TOPK_GROUPS):
        _, f = _first_argmax(gs, giota, N_EXPERT_GROUPS)
        hit = giota == f
        keep = jnp.where(hit, 1.0, keep)
        gs = jnp.where(hit, -jnp.inf, gs)
    cur = jnp.concatenate([jnp.where(keep[g:g + 1, :] > 0.0, blocks[g], -jnp.inf)
                           for g in range(N_EXPERT_GROUPS)], axis=0)
    eiota = lax.broadcasted_iota(I32, (N_EXPERTS, tm), 0)
    chosen = jnp.zeros((N_EXPERTS, tm), F32)
    idx, sel = [], []
    for _ in range(TOP_K):
        _, f = _first_argmax(cur, eiota, N_EXPERTS)
        hit = eiota == f
        idx.append(f)
        sel.append(jnp.sum(jnp.where(hit, scores, 0.0), axis=0, keepdims=True))
        cur = jnp.where(hit, -jnp.inf, cur)
        chosen = jnp.where(hit, 1.0, chosen)
    sel = jnp.concatenate(sel, axis=0)
    idx = jnp.concatenate(idx, axis=0)
    gate_ref[...] = sel / jnp.sum(sel, axis=0, keepdims=True) * ROUTED_SCALE
    idx_ref[...] = idx
    before = jnp.where(lax.broadcasted_iota(I32, (tm, tm), 0) < lax.broadcasted_iota(I32, (tm, tm), 1), 1.0, 0.0)
    rank = _dot(chosen.astype(BF16), before.astype(BF16)) + carry_ref[:, 0:1]
    rank_ref[...] = jnp.concatenate(
        [jnp.sum(jnp.where(eiota == idx[k:k + 1, :], rank, 0.0), axis=0, keepdims=True) for k in range(TOP_K)],
        axis=0).astype(I32)
    carry_ref[...] = carry_ref[...] + jnp.sum(chosen, axis=1, keepdims=True)
    cnt_ref[...] = carry_ref[...]


def _route(logits_t, router_bias, tm):
    n_tok = logits_t.shape[1]
    tok = lambda rows: pl.BlockSpec((rows, tm), lambda i: (0, i))
    return pl.pallas_call(
        _route_kernel,
        grid=(n_tok // tm,),
        in_specs=[tok(N_EXPERTS), _const_spec(N_EXPERTS, 1)],
        out_specs=[tok(TOP_K), tok(TOP_K), tok(TOP_K), _const_spec(N_EXPERTS, 128)],
        out_shape=[jax.ShapeDtypeStruct((TOP_K, n_tok), I32),
                   jax.ShapeDtypeStruct((TOP_K, n_tok), F32),
                   jax.ShapeDtypeStruct((TOP_K, n_tok), I32),
                   jax.ShapeDtypeStruct((N_EXPERTS, 128), F32)],
        scratch_shapes=[pltpu.VMEM((N_EXPERTS, 128), F32)],
        compiler_params=_cparams(("arbitrary",), 32),
        name="route",
    )(logits_t, router_bias.astype(F32).reshape(N_EXPERTS, 1))


def _moe_kernel(be_ref, na_ref, rp_ref, nx_ref, src_ref, srcn_ref, dst_ref, h_hbm, wg_hbm, wu_hbm, wd_hbm, y_hbm,
                xb, yb, wgf, wuf, wdf, wgb, wub, wdb, sem_in, sem_out, sem_w, *, layer):
    i = pl.program_id(0)
    n_steps = pl.num_programs(0)
    n_act = na_ref[0]
    cur = i % 2
    oth = 1 - cur
    bm = xb.shape[1] // LANE_BLKS

    def tile(ref, row):
        start = row * LANE_BLKS
        if not isinstance(row, int):
            start = pl.multiple_of(start, LANE_BLKS)
        return ref.at[pl.ds(start, LANE_BLKS), :]

    def gather(idx_ref, s):
        for r in range(bm):
            pltpu.make_async_copy(tile(h_hbm, idx_ref[0, 0, r]), tile(xb.at[s], r), sem_in.at[s]).start()

    def scatter(idx_ref, s):
        for r in range(bm):
            pltpu.make_async_copy(tile(yb.at[s], r), tile(y_hbm, idx_ref[0, 0, r]), sem_out.at[s]).start()

    def wait_gather(s):
        pltpu.make_async_copy(h_hbm.at[pl.ds(0, bm * LANE_BLKS), :], xb.at[s], sem_in.at[s]).wait()

    def wait_scatter(s):
        pltpu.make_async_copy(yb.at[s], y_hbm.at[pl.ds(0, bm * LANE_BLKS), :], sem_out.at[s]).wait()

    def weight_copies(e, s):
        return [pltpu.make_async_copy(w_hbm.at[layer, e], stage.at[s], sem_w.at[s])
                for w_hbm, stage in ((wg_hbm, wgf), (wu_hbm, wuf), (wd_hbm, wdf))]

    @pl.when(i == 0)
    def _():
        gather(src_ref, 0)
        for cp in weight_copies(be_ref[0], 0):
            cp.start(priority=1)

    @pl.when(i < n_act)
    def _():
        wait_gather(cur)

        @pl.when(i >= 2)
        def _():
            wait_scatter(cur)

        @pl.when(jnp.logical_or(i == 0, be_ref[i] != be_ref[jnp.maximum(i - 1, 0)]))
        def _():
            e = be_ref[i]
            s = rp_ref[e]
            for cp in weight_copies(e, s):
                cp.wait()

            @pl.when(nx_ref[e] >= 0)
            def _():
                for cp in weight_copies(nx_ref[e], 1 - s):
                    cp.start(priority=1)

            wgb[...] = wgf[s].astype(BF16)
            wub[...] = wuf[s].astype(BF16)
            wdb[...] = wdf[s].astype(BF16)

        gather(srcn_ref, oth)
        x = _unpack_bf16(_load_tile_rows(xb.at[cur], bm))
        hid = (_silu(_dot(x, wgb[...])) * _dot(x, wub[...])).astype(BF16)
        _store_tile_rows(yb.at[cur], _pack_halves(_dot(hid, wdb[...])))
        scatter(dst_ref, cur)

    @pl.when(i == n_steps - 1)
    def _():
        wait_gather(n_act % 2)
        wait_scatter((n_act + 1) % 2)

        @pl.when(n_act >= 2)
        def _():
            wait_scatter(n_act % 2)


def _moe_experts(layer, h2p, plan, w_gate, w_up, w_down):
    blk_exp, n_act, run_par, nxt_exp, src, dst = plan
    n_blocks = blk_exp.shape[0]
    bm = MOE_BM
    smem_blk = lambda fn: pl.BlockSpec((1, 1, bm), fn, memory_space=pltpu.SMEM)
    hbm = pl.BlockSpec(memory_space=pl.ANY)
    grid_spec = pltpu.PrefetchScalarGridSpec(
        num_scalar_prefetch=4,
        grid=(n_blocks,),
        in_specs=[smem_blk(lambda i, *_: (i, 0, 0)),
                  smem_blk(lambda i, *_: (jnp.minimum(i + 1, n_blocks - 1), 0, 0)),
                  smem_blk(lambda i, *_: (i, 0, 0)),
                  hbm, hbm, hbm, hbm],
        out_specs=hbm,
        scratch_shapes=[pltpu.VMEM((2, bm * LANE_BLKS, 128), U32), pltpu.VMEM((2, bm * LANE_BLKS, 128), U32),
                        pltpu.VMEM((2, D, EXPERT_FF), F32), pltpu.VMEM((2, D, EXPERT_FF), F32),
                        pltpu.VMEM((2, EXPERT_FF, D), F32),
                        pltpu.VMEM((D, EXPERT_FF), BF16), pltpu.VMEM((D, EXPERT_FF), BF16),
                        pltpu.VMEM((EXPERT_FF, D), BF16),
                        pltpu.SemaphoreType.DMA((2,)), pltpu.SemaphoreType.DMA((2,)),
                        pltpu.SemaphoreType.DMA((2,))],
    )
    src3 = src.reshape(n_blocks, 1, bm)
    return pl.pallas_call(
        functools.partial(_moe_kernel, layer=layer),
        grid_spec=grid_spec,
        out_shape=jax.ShapeDtypeStruct((n_blocks * bm * LANE_BLKS, 128), U32),
        compiler_params=_cparams(("arbitrary",), 48),
        name="moe_experts",
    )(blk_exp, n_act, run_par, nxt_exp, src3, src3, dst.reshape(n_blocks, 1, bm), h2p, w_gate, w_up, w_down)


def _combine_kernel(*refs, final):
    x_ref, h_ref, wg_ref, wu_ref, wd_ref, gt_ref, g2_ref = refs[:7]
    y_refs = refs[7:7 + TOP_K]
    fg_ref = refs[7 + TOP_K] if final else None
    o_ref, sh_ref = refs[-2:]
    tm = x_ref.shape[0]
    xh = _unpack_bf16(_load_tile_rows(h_ref, tm))
    hid = (_silu(_dot(xh, wg_ref[...])) * _dot(xh, wu_ref[...])).astype(BF16)
    sh_ref[...] = _dot(hid, wd_ref[...])
    gt = gt_ref[...]
    for b in range(LANE_BLKS):
        c_lo = slice(b * 128, (b + 1) * 128)
        c_hi = slice(HALF + b * 128, HALF + (b + 1) * 128)
        acc_lo = sh_ref[:, c_lo]
        acc_hi = sh_ref[:, c_hi]
        for k in range(TOP_K):
            lo, hi = _unpack_halves(y_refs[k][pl.ds(b, tm, stride=LANE_BLKS), :])
            acc_lo = acc_lo + lo * gt[:, k:k + 1]
            acc_hi = acc_hi + hi * gt[:, k:k + 1]
        o_ref[:, c_lo] = x_ref[:, c_lo] + g2_ref[:, c_lo] * acc_lo
        o_ref[:, c_hi] = x_ref[:, c_hi] + g2_ref[:, c_hi] * acc_hi
    if final:
        xn = o_ref[...]
        o_ref[...] = xn * lax.rsqrt(jnp.mean(xn * xn, axis=-1, keepdims=True) + EPS) * fg_ref[...]


def _combine(x, h2p, shared_w, gates, mod_l, y_tk, n_tok, final_g, tm):
    nt = n_tok // tm
    group_fn = _group_fn(tm)
    row = pl.BlockSpec((tm, D), lambda i: (i, 0))
    packed = lambda fn: pl.BlockSpec((tm * LANE_BLKS, 128), fn)
    in_specs = [row, packed(lambda i: (i, 0)),
                _resident_spec(D, EXPERT_FF), _resident_spec(D, EXPERT_FF), _resident_spec(EXPERT_FF, D),
                pl.BlockSpec((tm, TOP_K), lambda i: (i, 0)), _mod_spec(5, group_fn)]
    in_specs += [packed(functools.partial(lambda i, k: (k * nt + i, 0), k=k)) for k in range(TOP_K)]
    args = [x, h2p, *shared_w, gates, mod_l] + [y_tk] * TOP_K
    final = final_g is not None
    if final:
        in_specs.append(_const_spec(1, D))
        args.append(final_g)
    return pl.pallas_call(
        functools.partial(_combine_kernel, final=final),
        grid=(nt,),
        in_specs=in_specs,
        out_specs=row,
        out_shape=jax.ShapeDtypeStruct((n_tok, D), F32),
        scratch_shapes=[pltpu.VMEM((tm, D), F32)],
        compiler_params=_cparams(("arbitrary",), 48),
        name="moe_combine",
    )(*args)


def _dispatch_plan(idx_t, rank_t, cnt, n_tok):
    bm = MOE_BM
    n_assign = n_tok * TOP_K
    n_blocks = n_assign // bm + N_EXPERTS
    n_rows = n_blocks * bm
    sizes = cnt[:, 0].astype(I32)
    padded = (sizes + bm - 1) // bm * bm
    pad_end = jnp.cumsum(padded)
    pad_start = pad_end - padded
    blk_first = jnp.arange(n_blocks, dtype=I32) * bm
    blk_exp = jnp.minimum(jnp.sum((pad_end[None, :] <= blk_first[:, None]).astype(I32), axis=1), N_EXPERTS - 1)
    n_act = (pad_end[-1] // bm).astype(I32).reshape(1)
    experts = jnp.arange(N_EXPERTS, dtype=I32)
    slot = jnp.sum(jnp.where(idx_t[:, :, None] == experts, pad_start, 0), axis=-1) + rank_t
    ids = (jnp.arange(n_tok, dtype=I32)[None, :] * TOP_K + jnp.arange(TOP_K, dtype=I32)[:, None] + 1)
    inv = (jnp.zeros((n_rows,), I32).at[slot.reshape(-1)].add(ids.reshape(-1)) - 1).reshape(n_blocks, bm)
    valid = inv >= 0
    tok = inv // TOP_K
    spare = n_assign + blk_exp[:, None] * bm + jnp.arange(bm, dtype=I32)[None, :]
    src = jnp.where(valid, tok, 0).astype(I32)
    dst = jnp.where(valid, (inv - tok * TOP_K) * n_tok + tok, spare).astype(I32)
    used = sizes > 0
    run_par = ((jnp.cumsum(used.astype(I32)) - 1) % 2).astype(I32)
    first_used_from = lax.cummin(jnp.where(used, experts, N_EXPERTS), axis=0, reverse=True)
    nxt = jnp.concatenate([first_used_from[1:], jnp.full((1,), N_EXPERTS, I32)])
    nxt_exp = jnp.where(nxt < N_EXPERTS, nxt, -1).astype(I32)
    return blk_exp, n_act, run_par, nxt_exp, src, dst


def _moe_layer(layer, x, h2p, logits_t, router_bias, mod_l, exp_w, shared_w, final_g):
    n_tok = h2p.shape[0] // LANE_BLKS
    idx_t, gates_t, rank_t, cnt = _route(logits_t, router_bias, TM_ROUTE)
    plan = _dispatch_plan(idx_t, rank_t, cnt, n_tok)
    y_tk = _moe_experts(layer, h2p, plan, *exp_w)
    return _combine(x, h2p, [w[layer].astype(BF16) for w in shared_w], gates_t.T, mod_l, y_tk, n_tok,
                    final_g, TM_COMBINE)


def _rope_tables():
    rows = SEQ // GRID_W
    row = jnp.broadcast_to(jnp.arange(rows)[:, None], (rows, GRID_W)).reshape(-1).astype(F32)
    col = jnp.broadcast_to(jnp.arange(GRID_W)[None, :], (rows, GRID_W)).reshape(-1).astype(F32)
    inv = ROPE_BASE ** (-jnp.arange(0, AXIS_DIM, 2, dtype=F32) / AXIS_DIM)
    ang_r = row[:, None] * inv
    ang_c = col[:, None] * inv
    ang = jnp.concatenate([ang_r, ang_r, ang_c, ang_c], axis=-1)
    cos, sin = jnp.cos(ang), jnp.sin(ang)
    sign = jnp.where((jnp.arange(HEAD_DIM) % AXIS_DIM) < AXIS_DIM // 2, -1.0, 1.0).astype(F32)
    ident = TM_PROJ
    cos_t = jnp.concatenate([jnp.tile(cos, (1, 2)), jnp.ones((ident, 128), F32)], axis=0)
    sin_t = jnp.concatenate([jnp.tile(sin * sign, (1, 2)), jnp.zeros((ident, 128), F32)], axis=0)
    return cos_t, sin_t


def _heads_g_major(w, axis):
    shape = w.shape
    w = w.reshape(shape[:axis] + (N_KV_HEADS, Q_PER_KV, HEAD_DIM) + shape[axis + 1:])
    return jnp.swapaxes(w, axis, axis + 1).reshape(shape)


def kernel(x, c, ctx, c_ctx, ada_w, ada_b, norm1_g, norm2_g, even_w_in, gmlp_ln_g, gmlp_ln_b, gmlp_ws, gmlp_bs, pool_w, pool_scale, even_w_out, odd_w_in, conv_w, conv_b, conv_ln_g, conv_ln_b, attn_sink, odd_w_out, router_w, router_bias, exp_w_gate, exp_w_up, exp_w_down, shared_w_gate, shared_w_up, shared_w_down, final_g):
    mod = _ada_mod(c, c_ctx, ada_w, ada_b)
    row = lambda a: a.reshape(1, -1)
    exp_w = (exp_w_gate, exp_w_up, exp_w_down)
    shared_w = (shared_w_gate, shared_w_up, shared_w_down)

    def router_t(i):
        return router_w[i].T.astype(BF16)

    xl = x.reshape(N_LAT, D)
    xc = ctx.reshape(N_CTX, D)

    uv, z = _even_in(xl, xc, mod[0], row(norm1_g[0]), even_w_in[0].astype(BF16), TM_PROJ)
    y = _even_mix(uv, z, row(gmlp_ln_g[0]), row(gmlp_ln_b[0]), gmlp_ws[0].astype(BF16),
                  gmlp_bs[0].reshape(A_GROUPS, CHUNK, 1), pool_w[0].astype(BF16), row(pool_scale[0]), TM_SEQ)
    x1, h2p, lg = _mix_out(y, 0, y, 1, even_w_out[0].astype(BF16).reshape(2, HALF, D), xl, xc, N_ALL, mod[0],
                           row(norm2_g[0]), router_t(0), TM_PROJ)
    x1 = _moe_layer(0, x1, h2p, lg, router_bias[0], mod[0], exp_w, shared_w, None)

    w_in1 = odd_w_in[0]
    w_main = jnp.concatenate([w_in1[:, :2 * HALF], _heads_g_major(w_in1[:, 2 * HALF:3 * HALF], 1)],
                             axis=1).astype(BF16)
    w_kv = w_in1[:, 3 * HALF:].astype(BF16)
    w_out1 = odd_w_out[0]
    w_out1 = jnp.stack([w_out1[:HALF], _heads_g_major(w_out1[HALF:], 0)], axis=0).astype(BF16)
    cos_t, sin_t = _rope_tables()
    zc, q, k, v = _odd_in(x1, mod[1], row(norm1_g[1]), w_main, w_kv, cos_t, sin_t, TM_PROJ)
    y_conv = _conv_module(zc, N_LAT, conv_w[0], row(conv_b[0]), row(conv_ln_g[0]), row(conv_ln_b[0]), TM_SEQ)
    y_attn = _attention(q, k, v, attn_sink[0].astype(F32))
    x2, h2p, lg = _mix_out(y_conv, 0, y_attn, 0, w_out1, x1, x1, N_LAT, mod[1], row(norm2_g[1]),
                           router_t(1), TM_PROJ)
    out = _moe_layer(1, x2, h2p, lg, router_bias[1], mod[1], exp_w, shared_w, row(final_g))
    return out.reshape(BATCH, SEQ, D)
```
